```python
import math
import jax, jax.numpy as jnp
from jax import lax
import numpy as np

D_MODEL = 1024
BATCH = 8
SEQ = 4096
DEPTH = 2

CHUNK = 64
N_EVEN = (DEPTH + 1) // 2
N_ODD = DEPTH // 2
A_WIDTH = D_MODEL // 2
A_KEY = 128
A_HEADS = A_WIDTH // A_KEY
A_VAL = A_WIDTH // A_HEADS
B_WIDTH = D_MODEL // 2
B_HEAD = 64
B_HEADS = B_WIDTH // B_HEAD
B_DECAY_LORA = 32
B_AAA_LORA = 32
B_GATE_LORA = 96
B_SHIFT_COLS = 3 * B_WIDTH + B_DECAY_LORA + B_AAA_LORA + B_GATE_LORA
EVEN_IN = 4 * A_WIDTH + B_SHIFT_COLS
C_WIDTH = D_MODEL // 4
C_GROUP = 16
C_GROUPS = C_WIDTH // C_GROUP
C_STATE = 64
D_WIDTH = D_MODEL - C_WIDTH
D_HEADS = 4
D_HEAD = D_WIDTH // D_HEADS
D_CONV = 4
ODD_IN = C_WIDTH + 4 * D_WIDTH + 2 * D_HEADS
FFN_HIDDEN = ((8 * D_MODEL // 3 + 255) // 256) * 256

RMS_EPS = 1e-6
GN_EPS = 64e-5
M_INIT = -1e30
F32 = jnp.float32

kernel_name = "hybrid_hgrn2_rwkv7_s5_mlstm_trunk"


def rms_norm(x, g):
    xf = x.astype(F32)
    y = xf * lax.rsqrt(jnp.mean(xf * xf, axis=-1, keepdims=True) + RMS_EPS)
    return (y * g.astype(F32)).astype(x.dtype)


def head_rms_norm(y, g, n_heads):
    bn, s, w = y.shape
    yh = y.reshape(bn, s, n_heads, w // n_heads)
    yh = yh * lax.rsqrt(jnp.mean(yh * yh, axis=-1, keepdims=True) + RMS_EPS)
    return yh.reshape(bn, s, w) * g.astype(F32)


def split_cols(u, sizes):
    return jnp.split(u, np.cumsum(sizes)[:-1].tolist(), axis=-1)


def token_shift(u):
    return jnp.pad(u, ((0, 0), (1, 0), (0, 0)))[:, :-1]


def causal_depthwise_conv(u, w):
    k, c = w.shape
    return lax.conv_general_dilated(
        u, w[:, None, :].astype(u.dtype), window_strides=(1,), padding=[(k - 1, 0)],
        dimension_numbers=("NWC", "WIO", "NWC"), feature_group_count=c)


def to_chunks(t):
    bn, s, h, d = t.shape
    return t.reshape(bn, s // CHUNK, CHUNK, h, d).transpose(1, 0, 3, 2, 4)


def from_chunks(t):
    nc, bn, h, l, d = t.shape
    return t.transpose(1, 0, 3, 2, 4).reshape(bn, nc * l, h, d)


def gate_chunks(t):
    bn, s, h = t.shape
    return t.reshape(bn, s // CHUNK, CHUNK, h).transpose(1, 0, 3, 2)


def hgrn2_mix(q, f_pre, i, g, lb, gain):
    bn, s, _ = q.shape
    q = jax.nn.silu(q.astype(F32))
    lb = lb.astype(F32)
    f = lb + (1.0 - lb) * jax.nn.sigmoid(f_pre.astype(F32))
    logf = jnp.log(f)
    k = 1.0 - f
    heads = lambda t: to_chunks(t.reshape(bn, s, A_HEADS, -1))
    mask = jnp.tril(jnp.ones((CHUNK, CHUNK), dtype=bool))[:, :, None]

    def chunk_step(state, inp):
        qc, kc, lfc, vc = inp
        b = jnp.cumsum(lfc, axis=2)
        diff = b[:, :, :, None, :] - b[:, :, None, :, :]
        decay = jnp.exp(jnp.where(mask, diff, -jnp.inf))
        att = jnp.einsum("bhtk,bhsk,bhtsk->bhts", qc, kc, decay)
        o = (jnp.einsum("bhtk,bhkv->bhtv", qc * jnp.exp(b), state)
             + jnp.einsum("bhts,bhsv->bhtv", att, vc))
        b_last = b[:, :, -1:, :]
        state = (jnp.exp(b_last[:, :, 0, :])[..., None] * state
                 + jnp.einsum("bhsk,bhsv->bhkv", kc * jnp.exp(b_last - b), vc))
        return state, o

    state0 = jnp.zeros((bn, A_HEADS, A_KEY, A_VAL), F32)
    _, o = lax.scan(chunk_step, state0,
                    (heads(q), heads(k), heads(logf), heads(i.astype(F32))))
    o = from_chunks(o).reshape(bn, s, A_WIDTH)
    return head_rms_norm(o, gain, A_HEADS) * jax.nn.silu(g.astype(F32))


def rwkv7_mix(u, mu, w0, w2, a0, a2, g2, k_k, k_a, r_k, ln_w, ln_b):
    bn, s, _ = u.shape
    u = u.astype(F32)
    u = u + (token_shift(u) - u) * mu.astype(F32)
    r, k, v, wd, ad, gd = split_cols(
        u, (B_WIDTH, B_WIDTH, B_WIDTH, B_DECAY_LORA, B_AAA_LORA, B_GATE_LORA))
    w_log = -jax.nn.softplus(-(w0 + jnp.tanh(wd) @ w2)) - 0.5
    decay = jnp.exp(-jnp.exp(w_log))
    a = jax.nn.sigmoid(a0 + ad @ a2)
    g = jax.nn.sigmoid(gd) @ g2
    heads = lambda t: t.reshape(bn, s, B_HEADS, B_HEAD)
    kk = heads(k * k_k)
    kk = kk / jnp.maximum(jnp.sqrt(jnp.sum(kk * kk, axis=-1, keepdims=True)), 1e-12)
    k = heads(k * (1.0 + (a - 1.0) * k_a))
    r, v, decay, a = heads(r), heads(v), heads(decay), heads(a)

    def step(state, inp):
        r_t, w_t, k_t, v_t, a_t, b_t = inp
        sa = jnp.einsum("bhvk,bhk->bhv", state, a_t)
        state = (state * w_t[:, :, None, :] + sa[..., None] * b_t[:, :, None, :]
                 + v_t[..., None] * k_t[:, :, None, :])
        return state, jnp.einsum("bhvk,bhk->bhv", state, r_t)

    tm = lambda t: t.transpose(1, 0, 2, 3)
    state0 = jnp.zeros((bn, B_HEADS, B_HEAD, B_HEAD), F32)
    _, y = lax.scan(step, state0,
                    (tm(r), tm(decay), tm(k), tm(v), tm(-kk), tm(kk * a)))
    y = y.transpose(1, 0, 2, 3)
    mean = jnp.mean(y, axis=-1, keepdims=True)
    var = jnp.mean(jnp.square(y - mean), axis=-1, keepdims=True)
    y = ((y - mean) * lax.rsqrt(var + GN_EPS) * ln_w.astype(F32).reshape(B_HEADS, B_HEAD)
         + ln_b.astype(F32).reshape(B_HEADS, B_HEAD))
    y = y + jnp.sum(r * k * r_k.astype(F32).reshape(B_HEADS, B_HEAD), axis=-1, keepdims=True) * v
    return y.reshape(bn, s, B_WIDTH) * g


def _complex_affine_combine(e1, e2):
    a1r, a1i, b1r, b1i = e1
    a2r, a2i, b2r, b2i = e2
    return (a2r * a1r - a2i * a1i,
            a2r * a1i + a2i * a1r,
            a2r * b1r - a2i * b1i + b2r,
            a2r * b1i + a2i * b1r + b2i)


def s5_mix(u, lam_re, lam_im, log_step, b_re, b_im, c_re, c_im, d_skip, glu_w, glu_b):
    bn, s, _ = u.shape
    u = u.astype(F32)
    lam_re, lam_im = lam_re.astype(F32), lam_im.astype(F32)
    step = jnp.exp(log_step.astype(F32))[:, None]
    mag = jnp.exp(lam_re * step)
    lb_re, lb_im = mag * jnp.cos(lam_im * step), mag * jnp.sin(lam_im * step)
    den = lam_re * lam_re + lam_im * lam_im
    nr, ni = lb_re - 1.0, lb_im
    coef_re = (nr * lam_re + ni * lam_im) / den
    coef_im = (ni * lam_re - nr * lam_im) / den
    b_re, b_im = b_re.astype(F32), b_im.astype(F32)
    bb_re = coef_re[..., None] * b_re - coef_im[..., None] * b_im
    bb_im = coef_re[..., None] * b_im + coef_im[..., None] * b_re
    ug = u.reshape(bn, s, C_GROUPS, C_GROUP)
    xr = jnp.einsum("bsgh,gph->sbgp", ug, bb_re)
    xi = jnp.einsum("bsgh,gph->sbgp", ug, bb_im)
    a_re = jnp.broadcast_to(lb_re, (s, 1, C_GROUPS, C_STATE))
    a_im = jnp.broadcast_to(lb_im, (s, 1, C_GROUPS, C_STATE))
    _, _, hr, hi = lax.associative_scan(_complex_affine_combine, (a_re, a_im, xr, xi), axis=0)
    y = (jnp.einsum("sbgp,ghp->bsgh", hr, c_re.astype(F32))
         - jnp.einsum("sbgp,ghp->bsgh", hi, c_im.astype(F32)))
    y = y.reshape(bn, s, C_WIDTH) + d_skip.astype(F32) * u
    z = jax.nn.gelu(y)
    return z * jax.nn.sigmoid(z @ glu_w.astype(F32) + glu_b.astype(F32))


def mlstm_mix(q, k, v, o, i_pre, f_pre, conv_q, conv_k, gain):
    bn, s, _ = q.shape
    q = jax.nn.silu(causal_depthwise_conv(q.astype(F32), conv_q.astype(F32)))
    k = jax.nn.silu(causal_depthwise_conv(k.astype(F32), conv_k.astype(F32))) / math.sqrt(D_HEAD)
    heads = lambda t: to_chunks(t.reshape(bn, s, D_HEADS, D_HEAD))
    log_f = jax.nn.log_sigmoid(f_pre.astype(F32))
    log_i = i_pre.astype(F32)
    mask = jnp.tril(jnp.ones((CHUNK, CHUNK), dtype=bool))

    def chunk_step(carry, inp):
        c_st, n_st, m_st = carry
        qc, kc, vc, lfc, ic = inp
        b = jnp.cumsum(lfc, axis=-1)
        dmat = jnp.where(mask, b[..., :, None] - b[..., None, :] + ic[..., None, :], -jnp.inf)
        e_inter = b + m_st[..., None]
        m_t = jnp.maximum(e_inter, jnp.max(dmat, axis=-1))
        w_intra = jnp.exp(dmat - m_t[..., None])
        s_inter = jnp.exp(e_inter - m_t)
        qk = jnp.einsum("bhtd,bhsd->bhts", qc, kc) * w_intra
        num = (s_inter[..., None] * jnp.einsum("bhtd,bhdv->bhtv", qc, c_st)
               + jnp.einsum("bhts,bhsv->bhtv", qk, vc))
        den = s_inter * jnp.einsum("bhtd,bhd->bht", qc, n_st) + jnp.sum(qk, axis=-1)
        h = num / jnp.maximum(jnp.abs(den), jnp.exp(-m_t))[..., None]
        m_new = m_t[..., -1]
        w_last = jnp.exp(b[..., -1:] - b + ic - m_new[..., None])
        d_prev = jnp.exp(b[..., -1] + m_st - m_new)
        c_new = d_prev[..., None, None] * c_st + jnp.einsum("bhs,bhsd,bhsv->bhdv", w_last, kc, vc)
        n_new = d_prev[..., None] * n_st + jnp.einsum("bhs,bhsd->bhd", w_last, kc)
        return (c_new, n_new, m_new), h

    carry0 = (jnp.zeros((bn, D_HEADS, D_HEAD, D_HEAD), F32),
              jnp.zeros((bn, D_HEADS, D_HEAD), F32),
              jnp.full((bn, D_HEADS), M_INIT, F32))
    _, h = lax.scan(chunk_step, carry0,
                    (heads(q), heads(k), heads(v.astype(F32)), gate_chunks(log_f), gate_chunks(log_i)))
    h = from_chunks(h).reshape(bn, s, D_WIDTH)
    return head_rms_norm(h, gain, D_HEADS) * jax.nn.sigmoid(o.astype(F32))


def swiglu(h, w_gate, w_up, w_down):
    return (jax.nn.silu(h @ w_gate) * (h @ w_up)) @ w_down


def setup_inputs(seed: int = 0) -> dict:
    key = jax.random.key(seed)
    ks = iter(jax.random.split(key, 64))
    nrm = lambda shape, scale: jax.random.normal(next(ks), shape, F32) * scale
    d = D_MODEL
    lam_im0 = jnp.float32(np.pi) * jnp.arange(C_STATE, dtype=F32)
    return {
        "x": nrm((BATCH, SEQ, d), 1.0),
        "norm_mix": 1.0 + nrm((DEPTH, d), 0.02),
        "norm_ffn": 1.0 + nrm((DEPTH, d), 0.02),
        "norm_final": 1.0 + nrm((d,), 0.02),
        "w_in_even": nrm((N_EVEN, d, EVEN_IN), d ** -0.5),
        "w_out_even": nrm((N_EVEN, A_WIDTH + B_WIDTH, d), (A_WIDTH + B_WIDTH) ** -0.5),
        "lb_table": nrm((DEPTH + 1, A_WIDTH), 0.1),
        "a_norm": 1.0 + nrm((N_EVEN, A_WIDTH), 0.02),
        "b_mu": jax.random.uniform(next(ks), (N_EVEN, B_SHIFT_COLS), F32),
        "b_w0": jnp.linspace(-6.0, -1.0, B_WIDTH, dtype=F32) + nrm((N_EVEN, B_WIDTH), 0.1),
        "b_w2": nrm((N_EVEN, B_DECAY_LORA, B_WIDTH), 0.1 * B_DECAY_LORA ** -0.5),
        "b_a0": nrm((N_EVEN, B_WIDTH), 0.1),
        "b_a2": nrm((N_EVEN, B_AAA_LORA, B_WIDTH), 0.3 * B_AAA_LORA ** -0.5),
        "b_g2": nrm((N_EVEN, B_GATE_LORA, B_WIDTH), B_GATE_LORA ** -0.5),
        "b_kk": 0.85 + nrm((N_EVEN, B_WIDTH), 0.02),
        "b_ka": 1.0 + nrm((N_EVEN, B_WIDTH), 0.02),
        "b_rk": nrm((N_EVEN, B_WIDTH), 0.1),
        "b_ln_w": 1.0 + nrm((N_EVEN, B_WIDTH), 0.02),
        "b_ln_b": nrm((N_EVEN, B_WIDTH), 0.02),
        "w_in_odd": nrm((N_ODD, d, ODD_IN), d ** -0.5),
        "w_out_odd": nrm((N_ODD, C_WIDTH + D_WIDTH, d), (C_WIDTH + D_WIDTH) ** -0.5),
        "c_lam_re": -0.5 + nrm((N_ODD, C_GROUPS, C_STATE), 0.01),
        "c_lam_im": lam_im0 + nrm((N_ODD, C_GROUPS, C_STATE), 0.01),
        "c_log_step": jax.random.uniform(next(ks), (N_ODD, C_GROUPS), F32,
                                         minval=math.log(1e-3), maxval=math.log(1e-1)),
        "c_b_re": nrm((N_ODD, C_GROUPS, C_STATE, C_GROUP), (2 * C_GROUP) ** -0.5),
        "c_b_im": nrm((N_ODD, C_GROUPS, C_STATE, C_GROUP), (2 * C_GROUP) ** -0.5),
        "c_c_re": nrm((N_ODD, C_GROUPS, C_GROUP, C_STATE), C_STATE ** -0.5),
        "c_c_im": nrm((N_ODD, C_GROUPS, C_GROUP, C_STATE), C_STATE ** -0.5),
        "c_d": nrm((N_ODD, C_WIDTH), 1.0),
        "c_glu_w": nrm((N_ODD, C_WIDTH, C_WIDTH), C_WIDTH ** -0.5),
        "c_glu_b": nrm((N_ODD, C_WIDTH), 0.02),
        "d_conv_q": nrm((N_ODD, D_CONV, D_WIDTH), D_CONV ** -0.5),
        "d_conv_k": nrm((N_ODD, D_CONV, D_WIDTH), D_CONV ** -0.5),
        "d_i_bias": nrm((N_ODD, D_HEADS), 0.1),
        "d_f_bias": jnp.linspace(3.0, 6.0, D_HEADS, dtype=F32) + nrm((N_ODD, D_HEADS), 0.1),
        "d_norm": 1.0 + nrm((N_ODD, D_WIDTH), 0.02),
        "ffn_gate": nrm((DEPTH, d, FFN_HIDDEN), d ** -0.5),
        "ffn_up": nrm((DEPTH, d, FFN_HIDDEN), d ** -0.5),
        "ffn_down": nrm((DEPTH, FFN_HIDDEN, d), FFN_HIDDEN ** -0.5),
    }


def reference(x, norm_mix, norm_ffn, norm_final, w_in_even, w_out_even, lb_table, a_norm,
              b_mu, b_w0, b_w2, b_a0, b_a2, b_g2, b_kk, b_ka, b_rk, b_ln_w, b_ln_b,
              w_in_odd, w_out_odd, c_lam_re, c_lam_im, c_log_step, c_b_re, c_b_im,
              c_c_re, c_c_im, c_d, c_glu_w, c_glu_b, d_conv_q, d_conv_k, d_i_bias,
              d_f_bias, d_norm, ffn_gate, ffn_up, ffn_down):
    lower_bounds = jnp.cumsum(jax.nn.softmax(lb_table.astype(F32), axis=0), axis=0)
    for l in range(DEPTH):
        h = rms_norm(x, norm_mix[l])
        if l % 2 == 0:
            e = l // 2
            u = h @ w_in_even[e]
            aq, af, ai, ag, ub = split_cols(u, (A_WIDTH, A_WIDTH, A_WIDTH, A_WIDTH, B_SHIFT_COLS))
            ya = hgrn2_mix(aq, af, ai, ag, lower_bounds[l], a_norm[e])
            yb = rwkv7_mix(ub, b_mu[e], b_w0[e], b_w2[e], b_a0[e], b_a2[e], b_g2[e],
                           b_kk[e], b_ka[e], b_rk[e], b_ln_w[e], b_ln_b[e])
            y = jnp.concatenate([ya, yb], axis=-1).astype(x.dtype) @ w_out_even[e]
        else:
            o_ = l // 2
            u = h @ w_in_odd[o_]
            uc, dq, dk, dv, do, di, df = split_cols(
                u, (C_WIDTH, D_WIDTH, D_WIDTH, D_WIDTH, D_WIDTH, D_HEADS, D_HEADS))
            yc = s5_mix(uc, c_lam_re[o_], c_lam_im[o_], c_log_step[o_], c_b_re[o_], c_b_im[o_],
                        c_c_re[o_], c_c_im[o_], c_d[o_], c_glu_w[o_], c_glu_b[o_])
            yd = mlstm_mix(dq, dk, dv, do, di + d_i_bias[o_], df + d_f_bias[o_],
                           d_conv_q[o_], d_conv_k[o_], d_norm[o_])
            y = jnp.concatenate([yc, yd], axis=-1).astype(x.dtype) @ w_out_odd[o_]
        x = x + y
        h = rms_norm(x, norm_ffn[l])
        x = x + swiglu(h, ffn_gate[l], ffn_up[l], ffn_down[l])
    return rms_norm(x, norm_final)
```

```python
import functools
import math

import jax
import jax.numpy as jnp
from jax import lax
from jax.experimental import pallas as pl
from jax.experimental.pallas import tpu as pltpu

F32 = jnp.float32
BF16 = jnp.bfloat16

RMS_EPS = 1e-6
GN_EPS = 64e-5
M_INIT = -1e30

D_MODEL = 1024
A_WIDTH, A_HEADS, A_KEY = 512, 4, 128
B_WIDTH, B_HEADS, B_HEAD = 512, 8, 64
B_DECAY_LORA, B_AAA_LORA, B_GATE_LORA = 32, 32, 96
B_TAIL = 256
EVEN_IN_PAD = 4 * A_WIDTH + 3 * B_WIDTH + B_TAIL
C_WIDTH, C_GROUP, C_GROUPS, C_STATE = 256, 16, 16, 64
D_HEADS, D_HEAD, D_HEAD_PAD = 4, 192, 256
D_WIDTH = D_HEADS * D_HEAD
D_WIDTH_PAD = D_HEADS * D_HEAD_PAD
D_CONV = 4
D_GATE_PAD = 128
ODD_IN_PAD = 4 * D_WIDTH_PAD + C_WIDTH + D_GATE_PAD
FFN_HIDDEN = 2816

HGRN_TILE = 128
HGRN_SUB = 16
RWKV_CHUNK = 64
S5_CHUNK = 16
MLSTM_CHUNK = 64

VMEM_LIMIT = 56 * 1024 * 1024


def _cparams(sem):
    return pltpu.CompilerParams(dimension_semantics=sem, vmem_limit_bytes=VMEM_LIMIT)


def _split2(x):
    hi = x.astype(BF16)
    lo = (x - hi.astype(F32)).astype(BF16)
    return hi, lo


def _split3(x):
    hi = x.astype(BF16)
    r1 = x - hi.astype(F32)
    mid = r1.astype(BF16)
    lo = (r1 - mid.astype(F32)).astype(BF16)
    return hi, mid, lo


def _dg(a, b, dims):
    return lax.dot_general(a, b, (dims, ((), ())), preferred_element_type=F32)


_NN = ((1,), (0,))
_NT = ((1,), (1,))
_TN = ((0,), (0,))


def _mm1(a, b, dims=_NN):
    return _dg(a.astype(BF16), b.astype(BF16), dims)


def _mm3(a, b, dims=_NN):
    ah, al = _split2(a)
    bh, bl = _split2(b)
    return _dg(ah, bh, dims) + (_dg(ah, bl, dims) + _dg(al, bh, dims))


def _mm_sel(sel_bf16, x, dims=_NN):
    h, m, l = _split3(x)
    return _dg(sel_bf16, h, dims) + (_dg(sel_bf16, m, dims) + _dg(sel_bf16, l, dims))


def _x_sel(x, sel_bf16, dims=_NN):
    h, m, l = _split3(x)
    return _dg(h, sel_bf16, dims) + (_dg(m, sel_bf16, dims) + _dg(l, sel_bf16, dims))


def _sigmoid(x):
    return 1.0 / (1.0 + jnp.exp(-x))


def _silu(x):
    return x * _sigmoid(x)


def _softplus(x):
    return jnp.maximum(x, 0.0) + jnp.log(1.0 + jnp.exp(-jnp.abs(x)))


def _gelu_tanh(x):
    c = math.sqrt(2.0 / math.pi)
    return 0.5 * x * (1.0 + jnp.tanh(c * (x + 0.044715 * (x * x * x))))


def _iota(shape, dim):
    return lax.broadcasted_iota(jnp.int32, shape, dim)


def _tril_mask(n, strict=False):
    r = _iota((n, n), 0)
    c = _iota((n, n), 1)
    return (r > c) if strict else (r >= c)


def _norm_matmul_kernel(x_ref, g_ref, w_ref, o_ref, h_ref):
    @pl.when(pl.program_id(1) == 0)
    def _():
        x = x_ref[...]
        ms = jnp.mean(x * x, axis=-1, keepdims=True)
        h_ref[...] = (x * lax.rsqrt(ms + RMS_EPS) * g_ref[...]).astype(BF16)

    o_ref[...] = jnp.dot(h_ref[...], w_ref[...], preferred_element_type=F32)


def _norm_matmul(x, g, w, *, tm, tn):
    m, d = x.shape
    n = w.shape[1]
    return pl.pallas_call(
        _norm_matmul_kernel,
        grid=(m // tm, n // tn),
        in_specs=[
            pl.BlockSpec((tm, d), lambda i, j: (i, 0)),
            pl.BlockSpec((1, d), lambda i, j: (0, 0)),
            pl.BlockSpec((d, tn), lambda i, j: (0, j)),
        ],
        out_specs=pl.BlockSpec((tm, tn), lambda i, j: (i, j)),
        out_shape=jax.ShapeDtypeStruct((m, n), F32),
        scratch_shapes=[pltpu.VMEM((tm, d), BF16)],
        compiler_params=_cparams(("parallel", "arbitrary")),
        name="norm_in_proj",
    )(x, g, w)


def _out_proj_kernel(a_ref, b_ref, wa_ref, wb_ref, x_ref, o_ref):
    y = jnp.dot(a_ref[...], wa_ref[...], preferred_element_type=F32)
    y = y + jnp.dot(b_ref[...], wb_ref[...], preferred_element_type=F32)
    o_ref[...] = x_ref[...] + y


def _out_proj(a, b, wa, wb, x, *, tm):
    m, d = x.shape
    ka, kb = a.shape[1], b.shape[1]
    return pl.pallas_call(
        _out_proj_kernel,
        grid=(m // tm,),
        in_specs=[
            pl.BlockSpec((tm, ka), lambda i: (i, 0)),
            pl.BlockSpec((tm, kb), lambda i: (i, 0)),
            pl.BlockSpec((ka, d), lambda i: (0, 0)),
            pl.BlockSpec((kb, d), lambda i: (0, 0)),
            pl.BlockSpec((tm, d), lambda i: (i, 0)),
        ],
        out_specs=pl.BlockSpec((tm, d), lambda i: (i, 0)),
        out_shape=jax.ShapeDtypeStruct((m, d), F32),
        compiler_params=_cparams(("parallel",)),
        name="out_proj_residual",
    )(a, b, wa, wb, x)


def _ffn_kernel(x_ref, g_ref, wg_ref, wu_ref, wd_ref, gf_ref, o_ref, h_ref, acc_ref, *, final_norm):
    j = pl.program_id(1)

    @pl.when(j == 0)
    def _():
        x = x_ref[...]
        ms = jnp.mean(x * x, axis=-1, keepdims=True)
        h_ref[...] = (x * lax.rsqrt(ms + RMS_EPS) * g_ref[...]).astype(BF16)
        acc_ref[...] = jnp.zeros_like(acc_ref)

    h = h_ref[...]
    gate = jnp.dot(h, wg_ref[...], preferred_element_type=F32)
    up = jnp.dot(h, wu_ref[...], preferred_element_type=F32)
    act = (_silu(gate) * up).astype(BF16)
    acc_ref[...] += jnp.dot(act, wd_ref[...], preferred_element_type=F32)

    @pl.when(j == pl.num_programs(1) - 1)
    def _():
        y = x_ref[...] + acc_ref[...]
        if final_norm:
            ms = jnp.mean(y * y, axis=-1, keepdims=True)
            y = y * lax.rsqrt(ms + RMS_EPS) * gf_ref[...]
        o_ref[...] = y


def _ffn(x, g, wg, wu, wd, gf, *, tm, th, final_norm):
    m, d = x.shape
    hid = wg.shape[1]
    return pl.pallas_call(
        functools.partial(_ffn_kernel, final_norm=final_norm),
        grid=(m // tm, hid // th),
        in_specs=[
            pl.BlockSpec((tm, d), lambda i, j: (i, 0)),
            pl.BlockSpec((1, d), lambda i, j: (0, 0)),
            pl.BlockSpec((d, th), lambda i, j: (0, j)),
            pl.BlockSpec((d, th), lambda i, j: (0, j)),
            pl.BlockSpec((th, d), lambda i, j: (j, 0)),
            pl.BlockSpec((1, d), lambda i, j: (0, 0)),
        ],
        out_specs=pl.BlockSpec((tm, d), lambda i, j: (i, 0)),
        out_shape=jax.ShapeDtypeStruct((m, d), F32),
        scratch_shapes=[pltpu.VMEM((tm, d), BF16), pltpu.VMEM((tm, d), F32)],
        compiler_params=_cparams(("parallel", "arbitrary")),
        name="swiglu_ffn",
    )(x, g, wg, wu, wd, gf)


def _hgrn2_kernel(q_ref, f_ref, i_ref, g_ref, lb_ref, gain_ref, o_ref, s_ref):
    T, SUB = HGRN_TILE, HGRN_SUB
    nb = T // SUB

    @pl.when(pl.program_id(1) == 0)
    def _():
        s_ref[...] = jnp.zeros_like(s_ref)

    r = _iota((T, T), 0)
    c = _iota((T, T), 1)
    tri = jnp.where((r >= c) & ((r // SUB) == (c // SUB)), 1.0, 0.0).astype(BF16)
    row3 = _iota((nb, SUB, A_KEY), 1)

    for h in range(A_HEADS):
        sl = slice(h * A_KEY, (h + 1) * A_KEY)
        q = _silu(q_ref[:, sl])
        lb = lb_ref[:, sl]
        f = lb + (1.0 - lb) * _sigmoid(f_ref[:, sl])
        lf = jnp.log(f)
        k = 1.0 - f
        v = i_ref[:, sl]
        b = _mm_sel(tri, lf)
        b3 = b.reshape(nb, SUB, A_KEY)
        q3 = q.reshape(nb, SUB, A_KEY)
        k3 = k.reshape(nb, SUB, A_KEY)
        v3 = v.reshape(nb, SUB, A_KEY)

        intra = jnp.zeros((nb, SUB, A_KEY), F32)
        for s in range(SUB):
            e = jnp.exp(jnp.minimum(b3 - b3[:, s:s + 1, :], 0.0))
            p = jnp.where(row3 >= s, q3 * e * k3[:, s:s + 1, :], 0.0)
            intra = intra + jnp.sum(p, axis=-1, keepdims=True) * v3[:, s:s + 1, :]

        bend = b3[:, SUB - 1:SUB, :]
        qt = q3 * jnp.exp(b3)
        kt = k3 * jnp.exp(bend - b3)
        dec = jnp.exp(bend)
        st = s_ref[h]
        outs = []
        for i in range(nb):
            outs.append(_mm1(qt[i], st, _NT))
            st = st * dec[i] + _mm1(v3[i], kt[i], _TN)
        s_ref[h] = st
        o = intra.reshape(T, A_KEY) + jnp.concatenate(outs, axis=0)
        o = o * lax.rsqrt(jnp.mean(o * o, axis=-1, keepdims=True) + RMS_EPS) * gain_ref[:, sl]
        o_ref[:, sl] = (o * _silu(g_ref[:, sl])).astype(o_ref.dtype)


def _hgrn2(u, lb, gain, *, batch, seq):
    T = HGRN_TILE
    nt = seq // T
    col = lambda j: pl.BlockSpec((T, A_WIDTH), lambda b, c, j=j: (b * nt + c, j))
    vec = pl.BlockSpec((1, A_WIDTH), lambda b, c: (0, 0))
    return pl.pallas_call(
        _hgrn2_kernel,
        grid=(batch, nt),
        in_specs=[col(0), col(1), col(2), col(3), vec, vec],
        out_specs=pl.BlockSpec((T, A_WIDTH), lambda b, c: (b * nt + c, 0)),
        out_shape=jax.ShapeDtypeStruct((batch * seq, A_WIDTH), BF16),
        scratch_shapes=[pltpu.VMEM((A_HEADS, A_KEY, A_KEY), F32)],
        compiler_params=_cparams(("parallel", "arbitrary")),
        name="hgrn2_mix",
    )(u, u, u, u, lb, gain)


def _rwkv7_kernel(r_ref, k_ref, v_ref, t_ref, mu_r, mu_k, mu_v, mu_t, w0_ref, w2_ref, a0_ref,
                  a2_ref, g2_ref, kk_ref, ka_ref, rk_ref, lnw_ref, lnb_ref, o_ref,
                  s_ref, cr_ref, ck_ref, cv_ref, ct_ref):
    L = RWKV_CHUNK
    N = B_HEAD

    @pl.when(pl.program_id(1) == 0)
    def _():
        s_ref[...] = jnp.zeros_like(s_ref)
        cr_ref[...] = jnp.zeros_like(cr_ref)
        ck_ref[...] = jnp.zeros_like(ck_ref)
        cv_ref[...] = jnp.zeros_like(cv_ref)
        ct_ref[...] = jnp.zeros_like(ct_ref)

    def shift_mix(x_ref, carry_ref, mu_ref):
        x = x_ref[...]
        row = _iota(x.shape, 0)
        prev = jnp.where(row == 0, carry_ref[7:8, :], pltpu.roll(x, 1, 0))
        carry_ref[...] = x[L - 8:L, :]
        return x + (prev - x) * mu_ref[...]

    r = shift_mix(r_ref, cr_ref, mu_r)
    k = shift_mix(k_ref, ck_ref, mu_k)
    v = shift_mix(v_ref, cv_ref, mu_v)
    tl = shift_mix(t_ref, ct_ref, mu_t)

    wl = _mm3(jnp.tanh(tl), w2_ref[...])
    al = _mm3(tl, a2_ref[...])
    gate = _mm1(_sigmoid(tl), g2_ref[...])
    w_log = -_softplus(-(w0_ref[...] + wl)) - 0.5
    lw = -jnp.exp(w_log)
    a_lr = _sigmoid(a0_ref[...] + al)

    hr = _iota((B_WIDTH, B_WIDTH), 0) // N
    hc = _iota((B_WIDTH, B_WIDTH), 1) // N
    ones_bd = jnp.where(hr == hc, 1.0, 0.0).astype(BF16)

    def seg_sum(x):
        return _x_sel(x, ones_bd)

    kk = k * kk_ref[...]
    kk = kk / jnp.maximum(jnp.sqrt(seg_sum(kk * kk)), 1e-12)
    k2 = k * (1.0 + (a_lr - 1.0) * ka_ref[...])
    aa = -kk
    bb = kk * a_lr

    tri = jnp.where(_tril_mask(L), 1.0, 0.0).astype(BF16)
    c = _mm_sel(tri, lw)
    c_last = c[L - 1:L, :]
    ec = jnp.exp(c)
    enc = jnp.exp(-c)
    rt = r * ec
    at = aa * jnp.exp(c - lw)
    bt = bb * enc
    kt = k2 * enc
    e_tail = jnp.exp(c_last - c)
    bl = bb * e_tail
    kl = k2 * e_tail
    g_last = jnp.exp(c_last)

    lower = _tril_mask(L)
    strict = _tril_mask(L, strict=True)
    eye = jnp.where(_iota((L, L), 0) == _iota((L, L), 1), 1.0, 0.0)

    ys = []
    for h in range(B_HEADS):
        sl = slice(h * N, (h + 1) * N)
        s0 = s_ref[h]
        rh, ah, bh, kh, vh = rt[:, sl], at[:, sl], bt[:, sl], kt[:, sl], v[:, sl]
        a_ab = jnp.where(strict, _mm3(ah, bh, _NT), 0.0)
        a_ak = jnp.where(strict, _mm1(ah, kh, _NT), 0.0)
        a_rb = jnp.where(lower, _mm1(rh, bh, _NT), 0.0)
        a_rk = jnp.where(lower, _mm1(rh, kh, _NT), 0.0)
        inv = eye + a_ab
        pw = a_ab
        for _ in range(int(math.log2(L)) - 1):
            pw = _mm3(pw, pw)
            inv = inv + _mm3(inv, pw)
        rhs = _mm1(ah, s0, _NT) + _mm1(a_ak, vh)
        u = _mm3(inv, rhs)
        ys.append(_mm1(rh, s0, _NT) + _mm1(a_rb, u) + _mm1(a_rk, vh))
        s_ref[h] = s0 * g_last[:, sl] + _mm1(u, bl[:, sl], _TN) + _mm1(vh, kl[:, sl], _TN)

    y = jnp.concatenate(ys, axis=-1)
    mean = seg_sum(y) * (1.0 / N)
    d = y - mean
    var = seg_sum(d * d) * (1.0 / N)
    y = d * lax.rsqrt(var + GN_EPS) * lnw_ref[...] + lnb_ref[...]
    y = y + seg_sum(r * k2 * rk_ref[...]) * v
    o_ref[...] = (y * gate).astype(o_ref.dtype)


def _rwkv7(u, p, *, batch, seq):
    L = RWKV_CHUNK
    nt = seq // L
    col = lambda j: pl.BlockSpec((L, B_WIDTH), lambda b, c, j=j: (b * nt + c, j))
    tail = pl.BlockSpec((L, B_TAIL), lambda b, c: (b * nt + c, (4 * A_WIDTH + 3 * B_WIDTH) // B_TAIL))
    full = lambda a: pl.BlockSpec(a.shape, lambda b, c: (0,) * a.ndim)
    params = [p["mu_r"], p["mu_k"], p["mu_v"], p["mu_t"], p["w0"], p["w2"], p["a0"], p["a2"],
              p["g2"], p["kk"], p["ka"], p["rk"], p["lnw"], p["lnb"]]
    return pl.pallas_call(
        _rwkv7_kernel,
        grid=(batch, nt),
        in_specs=[col(4), col(5), col(6), tail] + [full(a) for a in params],
        out_specs=pl.BlockSpec((L, B_WIDTH), lambda b, c: (b * nt + c, 0)),
        out_shape=jax.ShapeDtypeStruct((batch * seq, B_WIDTH), BF16),
        scratch_shapes=[pltpu.VMEM((B_HEADS, B_HEAD, B_HEAD), F32),
                        pltpu.VMEM((8, B_WIDTH), F32), pltpu.VMEM((8, B_WIDTH), F32),
                        pltpu.VMEM((8, B_WIDTH), F32), pltpu.VMEM((8, B_TAIL), F32)],
        compiler_params=_cparams(("parallel", "arbitrary")),
        name="rwkv7_mix",
    )(u, u, u, u, *params)


def _s5_kernel(u_ref, w_ref, wsw_ref, t_ref, m_ref, la_ref, lb_ref, lbs_ref, y_ref,
               z_ref, zs_ref, hp_ref, *, batch):
    n = u_ref.shape[1]
    u = u_ref[0].astype(BF16)
    z_ref[...] = jnp.dot(u, w_ref[0], preferred_element_type=F32)
    zs_ref[...] = jnp.dot(u, wsw_ref[0], preferred_element_type=F32)
    la = la_ref[0]
    lb = lb_ref[0]
    lbs = lbs_ref[0]

    def body(i, carry):
        h, hs = carry
        rows = pl.ds(pl.multiple_of(i * batch, batch), batch)
        hp_ref[rows, :] = h
        hn = la * h + lb * hs + z_ref[rows, :]
        hsn = la * hs + lbs * h + zs_ref[rows, :]
        return hn, hsn

    zero = jnp.zeros((batch, 2 * C_STATE), F32)
    lax.fori_loop(0, n // batch, body, (zero, zero))
    y = jnp.dot(u, t_ref[0], preferred_element_type=F32)
    y_ref[0] = y + jnp.dot(hp_ref[...].astype(BF16), m_ref[0], preferred_element_type=F32)


def _s5_scan(ug, prm, *, batch):
    g, n, w = ug.shape
    blk = lambda a: pl.BlockSpec((1,) + a.shape[1:], lambda i: (i,) + (0,) * (a.ndim - 1))
    ins = [ug, prm["w"], prm["wsw"], prm["t"], prm["m"], prm["la"], prm["lb"], prm["lbs"]]
    return pl.pallas_call(
        functools.partial(_s5_kernel, batch=batch),
        grid=(g,),
        in_specs=[blk(a) for a in ins],
        out_specs=pl.BlockSpec((1, n, w), lambda i: (i, 0, 0)),
        out_shape=jax.ShapeDtypeStruct((g, n, w), F32),
        scratch_shapes=[pltpu.VMEM((n, 2 * C_STATE), F32), pltpu.VMEM((n, 2 * C_STATE), F32),
                        pltpu.VMEM((n, 2 * C_STATE), F32)],
        compiler_params=_cparams(("parallel",)),
        name="s5_scan",
    )(*ins)


def _s5_glu_kernel(y_ref, u_ref, d_ref, gw_ref, gb_ref, o_ref):
    z = _gelu_tanh(y_ref[...] + d_ref[...] * u_ref[...])
    gate = jnp.dot(z.astype(BF16), gw_ref[...], preferred_element_type=F32) + gb_ref[...]
    o_ref[...] = (z * _sigmoid(gate)).astype(o_ref.dtype)


def _s5_glu(y, u, d, gw, gb, *, tm):
    m = y.shape[0]
    ucol = (4 * D_WIDTH_PAD) // C_WIDTH
    vec = pl.BlockSpec((1, C_WIDTH), lambda i: (0, 0))
    return pl.pallas_call(
        _s5_glu_kernel,
        grid=(m // tm,),
        in_specs=[pl.BlockSpec((tm, C_WIDTH), lambda i: (i, 0)),
                  pl.BlockSpec((tm, C_WIDTH), lambda i: (i, ucol)),
                  vec, pl.BlockSpec((C_WIDTH, C_WIDTH), lambda i: (0, 0)), vec],
        out_specs=pl.BlockSpec((tm, C_WIDTH), lambda i: (i, 0)),
        out_shape=jax.ShapeDtypeStruct((m, C_WIDTH), BF16),
        compiler_params=_cparams(("parallel",)),
        name="s5_glu",
    )(y, u, d, gw, gb)


def _s5_params(lam_re, lam_im, log_step, b_re, b_im, c_re, c_im):
    L = S5_CHUNK
    step = jnp.exp(log_step)[:, None]
    mag = jnp.exp(lam_re * step)
    lr, li = mag * jnp.cos(lam_im * step), mag * jnp.sin(lam_im * step)
    den = lam_re * lam_re + lam_im * lam_im
    nr, ni = lr - 1.0, li
    coef_re = (nr * lam_re + ni * lam_im) / den
    coef_im = (ni * lam_re - nr * lam_im) / den
    bb_re = coef_re[..., None] * b_re - coef_im[..., None] * b_im
    bb_im = coef_re[..., None] * b_im + coef_im[..., None] * b_re
    pr, pi = [jnp.ones_like(lr)], [jnp.zeros_like(lr)]
    for _ in range(L):
        pr.append(pr[-1] * lr - pi[-1] * li)
        pi.append(pr[-2] * li + pi[-1] * lr)
    pr, pi = jnp.stack(pr), jnp.stack(pi)
    hp = lax.Precision.HIGHEST
    zr = pr[:L, :, :, None] * bb_re[None] - pi[:L, :, :, None] * bb_im[None]
    zi = pr[:L, :, :, None] * bb_im[None] + pi[:L, :, :, None] * bb_re[None]
    kern = (jnp.einsum("gop,lgpi->lgoi", c_re, zr, precision=hp)
            - jnp.einsum("gop,lgpi->lgoi", c_im, zi, precision=hp))
    s_idx = jnp.arange(L)[:, None]
    t_idx = jnp.arange(L)[None, :]
    lag = jnp.clip(t_idx - s_idx, 0, L - 1)
    tt = jnp.where((t_idx >= s_idx)[:, :, None, None, None], kern[lag], 0.0)
    tmat = tt.transpose(2, 0, 4, 1, 3).reshape(C_GROUPS, L * C_GROUP, L * C_GROUP)
    wr = zr[::-1].transpose(1, 0, 3, 2).reshape(C_GROUPS, L * C_GROUP, C_STATE)
    wi = zi[::-1].transpose(1, 0, 3, 2).reshape(C_GROUPS, L * C_GROUP, C_STATE)
    w = jnp.concatenate([wr, wi], axis=-1)
    wsw = jnp.concatenate([wi, wr], axis=-1)
    qr, qi = pr[1:], pi[1:]
    m_re = c_re[None] * qr[:, :, None, :] - c_im[None] * qi[:, :, None, :]
    m_im = -(c_re[None] * qi[:, :, None, :] + c_im[None] * qr[:, :, None, :])
    mm = jnp.concatenate([m_re, m_im], axis=-1)
    mmat = mm.transpose(1, 3, 0, 2).reshape(C_GROUPS, 2 * C_STATE, L * C_GROUP)
    la = jnp.concatenate([pr[L], pr[L]], axis=-1)[:, None, :]
    lb = jnp.concatenate([-pi[L], pi[L]], axis=-1)[:, None, :]
    return {"w": w.astype(BF16), "wsw": wsw.astype(BF16), "t": tmat.astype(BF16),
            "m": mmat.astype(BF16), "la": la, "lb": lb, "lbs": -lb}


def _mlstm_kernel(q_ref, k_ref, v_ref, o_ref, gt_ref, cq_ref, ck_ref, gb_ref, gain_ref, y_ref,
                  c_ref, n_ref, m_ref, pq_ref, pk_ref):
    L = MLSTM_CHUNK
    P = D_HEAD_PAD

    @pl.when(pl.program_id(1) == 0)
    def _():
        c_ref[...] = jnp.zeros_like(c_ref)
        n_ref[...] = jnp.zeros_like(n_ref)
        m_ref[...] = jnp.full(m_ref.shape, M_INIT, F32)
        pq_ref[...] = jnp.zeros_like(pq_ref)
        pk_ref[...] = jnp.zeros_like(pk_ref)

    def conv(x_ref, prev_ref, w_ref):
        x = x_ref[...]
        xe = jnp.concatenate([prev_ref[...], x], axis=0)
        prev_ref[...] = x[L - 8:L, :]
        acc = xe[8:8 + L, :] * w_ref[D_CONV - 1:D_CONV, :]
        for j in range(D_CONV - 1):
            off = 8 - (D_CONV - 1) + j
            acc = acc + xe[off:off + L, :] * w_ref[j:j + 1, :]
        return _silu(acc)

    q = conv(q_ref, pq_ref, cq_ref)
    k = conv(k_ref, pk_ref, ck_ref) * (1.0 / math.sqrt(D_HEAD))
    v = v_ref[...]

    gates = gt_ref[...] + gb_ref[...]
    col = _iota(gates.shape, 1)
    is_f = (col >= D_HEADS) & (col < 2 * D_HEADS)
    gates = jnp.where(is_f, jnp.minimum(gates, 0.0) - jnp.log(1.0 + jnp.exp(-jnp.abs(gates))), gates)
    tri = jnp.where(_tril_mask(L), 1.0, 0.0).astype(BF16)
    bcum = _mm_sel(tri, gates)
    gates_t = gates.T
    bcum_t = bcum.T
    lower = _tril_mask(L)

    for h in range(D_HEADS):
        sl = slice(h * P, (h + 1) * P)
        qh, kh, vh = q[:, sl], k[:, sl], v[:, sl]
        b_col = bcum[:, D_HEADS + h:D_HEADS + h + 1]
        b_row = bcum_t[D_HEADS + h:D_HEADS + h + 1, :]
        i_col = gates[:, h:h + 1]
        i_row = gates_t[h:h + 1, :]
        m_st = m_ref[h]
        c_st = c_ref[h]
        n_st = n_ref[h]

        dmat = jnp.where(lower, b_col - b_row + i_row, -jnp.inf)
        e_inter = b_col + m_st
        m_t = jnp.maximum(e_inter, jnp.max(dmat, axis=-1, keepdims=True))
        w_intra = jnp.exp(dmat - m_t)
        s_inter = jnp.exp(e_inter - m_t)
        qk = _mm1(qh, kh, _NT) * w_intra
        num = s_inter * _mm1(qh, c_st) + _mm1(qk, vh)
        den = s_inter * jnp.sum(qh * n_st, axis=-1, keepdims=True) + jnp.sum(qk, axis=-1, keepdims=True)
        hh = num / jnp.maximum(jnp.abs(den), jnp.exp(-m_t))
        m_new = m_t[L - 1:L, :]
        b_last = b_col[L - 1:L, :]
        w_last = jnp.exp(b_last - b_col + i_col - m_new)
        d_prev = jnp.exp(b_last + m_st - m_new)
        kw = kh * w_last
        c_ref[h] = d_prev * c_st + _mm1(kw, vh, _TN)
        n_ref[h] = d_prev * n_st + jnp.sum(kw, axis=0, keepdims=True)
        m_ref[h] = m_new

        ms = jnp.sum(hh * hh, axis=-1, keepdims=True) * (1.0 / D_HEAD)
        out = hh * lax.rsqrt(ms + RMS_EPS) * gain_ref[:, sl] * _sigmoid(o_ref[:, sl])
        y_ref[:, sl] = out.astype(y_ref.dtype)


def _mlstm(u, conv_q, conv_k, gate_bias, gain, *, batch, seq):
    L = MLSTM_CHUNK
    nt = seq // L
    W = D_WIDTH_PAD
    col = lambda j: pl.BlockSpec((L, W), lambda b, c, j=j: (b * nt + c, j))
    gcol = pl.BlockSpec((L, D_GATE_PAD), lambda b, c: (b * nt + c, (4 * W + C_WIDTH) // D_GATE_PAD))
    full = lambda a: pl.BlockSpec(a.shape, lambda b, c: (0,) * a.ndim)
    return pl.pallas_call(
        _mlstm_kernel,
        grid=(batch, nt),
        in_specs=[col(0), col(1), col(2), col(3), gcol, full(conv_q), full(conv_k),
                  full(gate_bias), full(gain)],
        out_specs=pl.BlockSpec((L, W), lambda b, c: (b * nt + c, 0)),
        out_shape=jax.ShapeDtypeStruct((batch * seq, W), BF16),
        scratch_shapes=[pltpu.VMEM((D_HEADS, D_HEAD_PAD, D_HEAD_PAD), F32),
                        pltpu.VMEM((D_HEADS, 1, D_HEAD_PAD), F32),
                        pltpu.VMEM((D_HEADS, 1, 1), F32),
                        pltpu.VMEM((8, W), F32), pltpu.VMEM((8, W), F32)],
        compiler_params=_cparams(("parallel", "arbitrary")),
        name="mlstm_mix",
    )(u, u, u, u, u, conv_q, conv_k, gate_bias, gain)


def _pad_cols(a, n):
    return jnp.pad(a, [(0, 0)] * (a.ndim - 1) + [(0, n - a.shape[-1])])


def _pad_heads(a, axis):
    shp = a.shape
    a = a.reshape(shp[:axis] + (D_HEADS, D_HEAD) + shp[axis + 1:])
    pad = [(0, 0)] * a.ndim
    pad[axis + 1] = (0, D_HEAD_PAD - D_HEAD)
    a = jnp.pad(a, pad)
    return a.reshape(shp[:axis] + (D_WIDTH_PAD,) + shp[axis + 1:])


def _row(v):
    return v.reshape(1, -1).astype(F32)


def kernel(x, norm_mix, norm_ffn, norm_final, w_in_even, w_out_even, lb_table, a_norm, b_mu, b_w0, b_w2, b_a0, b_a2, b_g2, b_kk, b_ka, b_rk, b_ln_w, b_ln_b, w_in_odd, w_out_odd, c_lam_re, c_lam_im, c_log_step, c_b_re, c_b_im, c_c_re, c_c_im, c_d, c_glu_w, c_glu_b, d_conv_q, d_conv_k, d_i_bias, d_f_bias, d_norm, ffn_gate, ffn_up, ffn_down):
    batch, seq, d = x.shape
    m = batch * seq
    xf = x.reshape(m, d).astype(F32)
    lower_bounds = jnp.cumsum(jax.nn.softmax(lb_table.astype(F32), axis=0), axis=0)

    w_in = _pad_cols(w_in_even[0], EVEN_IN_PAD).astype(BF16)
    u = _norm_matmul(xf, _row(norm_mix[0]), w_in, tm=512, tn=768)
    ya = _hgrn2(u, _row(lower_bounds[0]), _row(a_norm[0]), batch=batch, seq=seq)

    mu = b_mu[0].astype(F32)
    lora_rows = lambda w, off: jnp.pad(w.astype(F32), ((off, B_TAIL - off - w.shape[0]), (0, 0)))
    rw = {
        "mu_r": _row(mu[:B_WIDTH]), "mu_k": _row(mu[B_WIDTH:2 * B_WIDTH]),
        "mu_v": _row(mu[2 * B_WIDTH:3 * B_WIDTH]), "mu_t": _row(_pad_cols(mu[3 * B_WIDTH:], B_TAIL)),
        "w0": _row(b_w0[0]), "w2": lora_rows(b_w2[0], 0),
        "a0": _row(b_a0[0]), "a2": lora_rows(b_a2[0], B_DECAY_LORA),
        "g2": lora_rows(b_g2[0], B_DECAY_LORA + B_AAA_LORA).astype(BF16),
        "kk": _row(b_kk[0]), "ka": _row(b_ka[0]), "rk": _row(b_rk[0]),
        "lnw": _row(b_ln_w[0]), "lnb": _row(b_ln_b[0]),
    }
    yb = _rwkv7(u, rw, batch=batch, seq=seq)
    wo = w_out_even[0].astype(BF16)
    xf = _out_proj(ya, yb, wo[:A_WIDTH], wo[A_WIDTH:], xf, tm=512)
    xf = _ffn(xf, _row(norm_ffn[0]), ffn_gate[0].astype(BF16), ffn_up[0].astype(BF16),
              ffn_down[0].astype(BF16), _row(norm_final), tm=512, th=1408, final_norm=False)

    wi = w_in_odd[0].astype(F32)
    seg = lambda j: _pad_heads(wi[:, C_WIDTH + j * D_WIDTH:C_WIDTH + (j + 1) * D_WIDTH], 1)
    w_in = jnp.concatenate(
        [seg(0), seg(1), seg(2), seg(3), wi[:, :C_WIDTH],
         _pad_cols(wi[:, C_WIDTH + 4 * D_WIDTH:], D_GATE_PAD)], axis=1).astype(BF16)
    u = _norm_matmul(xf, _row(norm_mix[1]), w_in, tm=512, tn=640)

    nch = seq // S5_CHUNK
    uc = u[:, 4 * D_WIDTH_PAD:4 * D_WIDTH_PAD + C_WIDTH]
    ug = uc.reshape(batch, nch, S5_CHUNK, C_GROUPS, C_GROUP).transpose(3, 1, 0, 2, 4)
    ug = ug.reshape(C_GROUPS, nch * batch, S5_CHUNK * C_GROUP)
    s5p = _s5_params(c_lam_re[0].astype(F32), c_lam_im[0].astype(F32), c_log_step[0].astype(F32),
                     c_b_re[0].astype(F32), c_b_im[0].astype(F32), c_c_re[0].astype(F32),
                     c_c_im[0].astype(F32))
    yg = _s5_scan(ug, s5p, batch=batch)
    ys5 = yg.reshape(C_GROUPS, nch, batch, S5_CHUNK, C_GROUP).transpose(2, 1, 3, 0, 4).reshape(m, C_WIDTH)
    yc = _s5_glu(ys5, u, _row(c_d[0]), c_glu_w[0].astype(BF16), _row(c_glu_b[0]), tm=1024)

    gate_bias = _row(_pad_cols(jnp.concatenate([d_i_bias[0], d_f_bias[0]]).astype(F32), D_GATE_PAD))
    yd = _mlstm(u, _pad_heads(d_conv_q[0].astype(F32), 1), _pad_heads(d_conv_k[0].astype(F32), 1),
                gate_bias, _row(_pad_heads(d_norm[0].astype(F32), 0)), batch=batch, seq=seq)
    wo = w_out_odd[0].astype(F32)
    xf = _out_proj(yc, yd, wo[:C_WIDTH].astype(BF16), _pad_heads(wo[C_WIDTH:], 0).astype(BF16), xf, tm=512)
    xf = _ffn(xf, _row(norm_ffn[1]), ffn_gate[1].astype(BF16), ffn_up[1].astype(BF16),
              ffn_down[1].astype(BF16), _row(norm_final), tm=512, th=1408, final_norm=True)
    return xf.reshape(batch, seq, d).astype(x.dtype)
```

```python
import functools
import math

import jax
import jax.numpy as jnp
from jax import lax
from jax.experimental import pallas as pl
from jax.experimental.pallas import tpu as pltpu

F32 = jnp.float32
BF16 = jnp.bfloat16

RMS_EPS = 1e-6
GN_EPS = 64e-5
M_INIT = -1e30

D_MODEL = 1024
A_WIDTH, A_HEADS, A_KEY = 512, 4, 128
B_WIDTH, B_HEADS, B_HEAD = 512, 8, 64
B_DECAY_LORA, B_AAA_LORA, B_GATE_LORA = 32, 32, 96
B_TAIL = 256
EVEN_IN_PAD = 4 * A_WIDTH + 3 * B_WIDTH + B_TAIL
C_WIDTH, C_GROUP, C_GROUPS, C_STATE = 256, 16, 16, 64
D_HEADS, D_HEAD, D_HEAD_PAD = 4, 192, 256
D_WIDTH = D_HEADS * D_HEAD
D_WIDTH_PAD = D_HEADS * D_HEAD_PAD
D_CONV = 4
D_GATE_PAD = 128
ODD_IN_PAD = 4 * D_WIDTH_PAD + C_WIDTH + D_GATE_PAD
FFN_HIDDEN = 2816

HGRN_TILE = 128
HGRN_SUB = 16
RWKV_CHUNK = 64
S5_CHUNK = 16
MLSTM_CHUNK = 64

VMEM_LIMIT = 56 * 1024 * 1024


def _cparams(sem):
    return pltpu.CompilerParams(dimension_semantics=sem, vmem_limit_bytes=VMEM_LIMIT)


def _split2(x):
    hi = x.astype(BF16)
    lo = (x - hi.astype(F32)).astype(BF16)
    return hi, lo


def _split3(x):
    hi = x.astype(BF16)
    r1 = x - hi.astype(F32)
    mid = r1.astype(BF16)
    lo = (r1 - mid.astype(F32)).astype(BF16)
    return hi, mid, lo


def _dg(a, b, dims):
    return lax.dot_general(a, b, (dims, ((), ())), preferred_element_type=F32)


_NN = ((1,), (0,))
_NT = ((1,), (1,))
_TN = ((0,), (0,))


def _mm1(a, b, dims=_NN):
    return _dg(a.astype(BF16), b.astype(BF16), dims)


def _mm3(a, b, dims=_NN):
    ah, al = _split2(a)
    bh, bl = _split2(b)
    return _dg(ah, bh, dims) + (_dg(ah, bl, dims) + _dg(al, bh, dims))


def _mm2(a, b, dims=_NN):
    ah, al = _split2(a)
    bh = b.astype(BF16)
    return _dg(ah, bh, dims) + _dg(al, bh, dims)


_mmi = _mm1


def _mm_sel(sel_bf16, x, dims=_NN):
    h, m, l = _split3(x)
    return _dg(sel_bf16, h, dims) + (_dg(sel_bf16, m, dims) + _dg(sel_bf16, l, dims))


def _x_sel(x, sel_bf16):
    n = x.shape[0]
    r = _dg(jnp.concatenate(_split3(x), axis=0), sel_bf16, _NN)
    return r[:n] + (r[n:2 * n] + r[2 * n:])


def _sigmoid(x):
    return 1.0 / (1.0 + jnp.exp(-x))


def _silu(x):
    return x * _sigmoid(x)


def _softplus(x):
    return jnp.maximum(x, 0.0) + jnp.log(1.0 + jnp.exp(-jnp.abs(x)))


def _gelu_tanh(x):
    c = math.sqrt(2.0 / math.pi)
    return 0.5 * x * (1.0 + jnp.tanh(c * (x + 0.044715 * (x * x * x))))


def _iota(shape, dim):
    return lax.broadcasted_iota(jnp.int32, shape, dim)


def _tril_mask(n, strict=False):
    r = _iota((n, n), 0)
    c = _iota((n, n), 1)
    return (r > c) if strict else (r >= c)


def _norm_matmul_kernel(x_ref, g_ref, w_ref, o_ref, h_ref):
    @pl.when(pl.program_id(1) == 0)
    def _():
        x = x_ref[...]
        ms = jnp.mean(x * x, axis=-1, keepdims=True)
        h_ref[...] = (x * lax.rsqrt(ms + RMS_EPS) * g_ref[...]).astype(BF16)

    o_ref[...] = jnp.dot(h_ref[...], w_ref[...], preferred_element_type=F32).astype(o_ref.dtype)


def _norm_matmul(x, g, w, *, tm, tn):
    m, d = x.shape
    n = w.shape[1]
    return pl.pallas_call(
        _norm_matmul_kernel,
        grid=(m // tm, n // tn),
        in_specs=[
            pl.BlockSpec((tm, d), lambda i, j: (i, 0)),
            pl.BlockSpec((1, d), lambda i, j: (0, 0)),
            pl.BlockSpec((d, tn), lambda i, j: (0, j)),
        ],
        out_specs=pl.BlockSpec((tm, tn), lambda i, j: (i, j)),
        out_shape=jax.ShapeDtypeStruct((m, n), BF16),
        scratch_shapes=[pltpu.VMEM((tm, d), BF16)],
        compiler_params=_cparams(("parallel", "arbitrary")),
        name="norm_in_proj",
    )(x, g, w)


def _out_proj_kernel(a_ref, b_ref, wa_ref, wb_ref, x_ref, o_ref):
    y = jnp.dot(a_ref[...], wa_ref[...], preferred_element_type=F32)
    y = y + jnp.dot(b_ref[...], wb_ref[...], preferred_element_type=F32)
    o_ref[...] = x_ref[...] + y


def _out_proj(a, b, wa, wb, x, *, tm):
    m, d = x.shape
    ka, kb = a.shape[1], b.shape[1]
    return pl.pallas_call(
        _out_proj_kernel,
        grid=(m // tm,),
        in_specs=[
            pl.BlockSpec((tm, ka), lambda i: (i, 0)),
            pl.BlockSpec((tm, kb), lambda i: (i, 0)),
            pl.BlockSpec((ka, d), lambda i: (0, 0)),
            pl.BlockSpec((kb, d), lambda i: (0, 0)),
            pl.BlockSpec((tm, d), lambda i: (i, 0)),
        ],
        out_specs=pl.BlockSpec((tm, d), lambda i: (i, 0)),
        out_shape=jax.ShapeDtypeStruct((m, d), F32),
        compiler_params=_cparams(("parallel",)),
        name="out_proj_residual",
    )(a, b, wa, wb, x)


def _ffn_kernel(x_ref, g_ref, wg_ref, wu_ref, wd_ref, gf_ref, o_ref, h_ref, acc_ref, *, final_norm):
    j = pl.program_id(1)

    @pl.when(j == 0)
    def _():
        x = x_ref[...]
        ms = jnp.mean(x * x, axis=-1, keepdims=True)
        h_ref[...] = (x * lax.rsqrt(ms + RMS_EPS) * g_ref[...]).astype(BF16)
        acc_ref[...] = jnp.zeros_like(acc_ref)

    h = h_ref[...]
    gate = jnp.dot(h, wg_ref[...], preferred_element_type=F32)
    up = jnp.dot(h, wu_ref[...], preferred_element_type=F32)
    act = (_silu(gate) * up).astype(BF16)
    acc_ref[...] += jnp.dot(act, wd_ref[...], preferred_element_type=F32)

    @pl.when(j == pl.num_programs(1) - 1)
    def _():
        y = x_ref[...] + acc_ref[...]
        if final_norm:
            ms = jnp.mean(y * y, axis=-1, keepdims=True)
            y = y * lax.rsqrt(ms + RMS_EPS) * gf_ref[...]
        o_ref[...] = y


def _ffn(x, g, wg, wu, wd, gf, *, tm, th, final_norm):
    m, d = x.shape
    hid = wg.shape[1]
    return pl.pallas_call(
        functools.partial(_ffn_kernel, final_norm=final_norm),
        grid=(m // tm, hid // th),
        in_specs=[
            pl.BlockSpec((tm, d), lambda i, j: (i, 0)),
            pl.BlockSpec((1, d), lambda i, j: (0, 0)),
            pl.BlockSpec((d, th), lambda i, j: (0, j)),
            pl.BlockSpec((d, th), lambda i, j: (0, j)),
            pl.BlockSpec((th, d), lambda i, j: (j, 0)),
            pl.BlockSpec((1, d), lambda i, j: (0, 0)),
        ],
        out_specs=pl.BlockSpec((tm, d), lambda i, j: (i, 0)),
        out_shape=jax.ShapeDtypeStruct((m, d), F32),
        scratch_shapes=[pltpu.VMEM((tm, d), BF16), pltpu.VMEM((tm, d), F32)],
        compiler_params=_cparams(("parallel", "arbitrary")),
        name="swiglu_ffn",
    )(x, g, wg, wu, wd, gf)


def _hgrn2_kernel(q_ref, f_ref, i_ref, g_ref, lb_ref, gain_ref, o_ref, s_ref):
    T, SUB = HGRN_TILE, HGRN_SUB
    nb = T // SUB

    @pl.when(pl.program_id(1) == 0)
    def _():
        s_ref[...] = jnp.zeros_like(s_ref)

    r = _iota((T, T), 0)
    c = _iota((T, T), 1)
    tri = jnp.where((r >= c) & ((r // SUB) == (c // SUB)), 1.0, 0.0).astype(BF16)
    row3 = _iota((nb, SUB, A_KEY), 1)

    for h in range(A_HEADS):
        sl = slice(h * A_KEY, (h + 1) * A_KEY)
        q = _silu(q_ref[:, sl].astype(F32))
        lb = lb_ref[:, sl]
        f = lb + (1.0 - lb) * _sigmoid(f_ref[:, sl].astype(F32))
        lf = jnp.log(f)
        k = 1.0 - f
        v = i_ref[:, sl].astype(F32)
        b = _mm_sel(tri, lf)
        b3 = b.reshape(nb, SUB, A_KEY)
        q3 = q.reshape(nb, SUB, A_KEY)
        k3 = k.reshape(nb, SUB, A_KEY)
        v3 = v.reshape(nb, SUB, A_KEY)

        intra = jnp.zeros((nb, SUB, A_KEY), F32)
        for s in range(SUB):
            e = jnp.exp(jnp.minimum(b3 - b3[:, s:s + 1, :], 0.0))
            p = jnp.where(row3 >= s, q3 * e * k3[:, s:s + 1, :], 0.0)
            intra = intra + jnp.sum(p, axis=-1, keepdims=True) * v3[:, s:s + 1, :]

        bend = b3[:, SUB - 1:SUB, :]
        qt = q3 * jnp.exp(b3)
        kt = k3 * jnp.exp(bend - b3)
        dec = jnp.exp(bend)
        st = s_ref[h]
        outs = []
        for i in range(nb):
            outs.append(_mm1(qt[i], st, _NT))
            st = st * dec[i] + _mm1(v3[i], kt[i], _TN)
        s_ref[h] = st
        o = intra.reshape(T, A_KEY) + jnp.concatenate(outs, axis=0)
        o = o * lax.rsqrt(jnp.mean(o * o, axis=-1, keepdims=True) + RMS_EPS) * gain_ref[:, sl]
        o_ref[:, sl] = (o * _silu(g_ref[:, sl].astype(F32))).astype(o_ref.dtype)


def _hgrn2(u, lb, gain, *, batch, seq):
    T = HGRN_TILE
    nt = seq // T
    col = lambda j: pl.BlockSpec((T, A_WIDTH), lambda b, c, j=j: (b * nt + c, j))
    vec = pl.BlockSpec((1, A_WIDTH), lambda b, c: (0, 0))
    return pl.pallas_call(
        _hgrn2_kernel,
        grid=(batch, nt),
        in_specs=[col(0), col(1), col(2), col(3), vec, vec],
        out_specs=pl.BlockSpec((T, A_WIDTH), lambda b, c: (b * nt + c, 0)),
        out_shape=jax.ShapeDtypeStruct((batch * seq, A_WIDTH), BF16),
        scratch_shapes=[pltpu.VMEM((A_HEADS, A_KEY, A_KEY), F32)],
        compiler_params=_cparams(("parallel", "arbitrary")),
        name="hgrn2_mix",
    )(u, u, u, u, lb, gain)


def _rwkv7_kernel(r_ref, k_ref, v_ref, t_ref, mu_r, mu_k, mu_v, mu_t, w0_ref, w2_ref, a0_ref,
                  a2_ref, g2_ref, kk_ref, ka_ref, rk_ref, lnw_ref, lnb_ref, o_ref,
                  s_ref, cr_ref, ck_ref, cv_ref, ct_ref):
    L = RWKV_CHUNK
    N = B_HEAD

    @pl.when(pl.program_id(1) == 0)
    def _():
        s_ref[...] = jnp.zeros_like(s_ref)
        cr_ref[...] = jnp.zeros_like(cr_ref)
        ck_ref[...] = jnp.zeros_like(ck_ref)
        cv_ref[...] = jnp.zeros_like(cv_ref)
        ct_ref[...] = jnp.zeros_like(ct_ref)

    def shift_mix(x_ref, carry_ref, mu_ref):
        x = x_ref[...].astype(F32)
        row = _iota(x.shape, 0)
        prev = jnp.where(row == 0, carry_ref[7:8, :], pltpu.roll(x, 1, 0))
        carry_ref[...] = x[L - 8:L, :]
        return x + (prev - x) * mu_ref[...]

    r = shift_mix(r_ref, cr_ref, mu_r)
    k = shift_mix(k_ref, ck_ref, mu_k)
    v = shift_mix(v_ref, cv_ref, mu_v)
    tl = shift_mix(t_ref, ct_ref, mu_t)

    wl = _mm3(jnp.tanh(tl), w2_ref[...])
    al = _mm3(tl, a2_ref[...])
    gate = _mm1(_sigmoid(tl), g2_ref[...])
    w_log = -_softplus(-(w0_ref[...] + wl)) - 0.5
    lw = -jnp.exp(w_log)
    a_lr = _sigmoid(a0_ref[...] + al)

    hr = _iota((B_WIDTH, B_WIDTH), 0) // N
    hc = _iota((B_WIDTH, B_WIDTH), 1) // N
    ones_bd = jnp.where(hr == hc, 1.0, 0.0).astype(BF16)

    def seg_sum(x):
        return _x_sel(x, ones_bd)

    kk = k * kk_ref[...]
    kk = kk / jnp.maximum(jnp.sqrt(seg_sum(kk * kk)), 1e-12)
    k2 = k * (1.0 + (a_lr - 1.0) * ka_ref[...])
    aa = -kk
    bb = kk * a_lr

    tri = jnp.where(_tril_mask(L), 1.0, 0.0).astype(BF16)
    c = _mm_sel(tri, lw)
    c_last = c[L - 1:L, :]
    ec = jnp.exp(c)
    enc = jnp.exp(-c)
    rt = r * ec
    at = aa * jnp.exp(c - lw)
    bt = bb * enc
    kt = k2 * enc
    e_tail = jnp.exp(c_last - c)
    bl = bb * e_tail
    kl = k2 * e_tail
    g_last = jnp.exp(c_last)

    lower = _tril_mask(L)
    strict = _tril_mask(L, strict=True)

    heads = range(B_HEADS)
    sls = [slice(h * N, (h + 1) * N) for h in heads]
    s0 = [s_ref[h] for h in heads]
    rh = [rt[:, sl] for sl in sls]
    ah = [at[:, sl] for sl in sls]
    bh = [bt[:, sl] for sl in sls]
    kh = [kt[:, sl] for sl in sls]
    vh = [v[:, sl] for sl in sls]
    a_ab = [jnp.where(strict, _mmi(ah[h], bh[h], _NT), 0.0) for h in heads]
    a_ak = [jnp.where(strict, _mm1(ah[h], kh[h], _NT), 0.0) for h in heads]
    a_rb = [jnp.where(lower, _mm1(rh[h], bh[h], _NT), 0.0) for h in heads]
    a_rk = [jnp.where(lower, _mm1(rh[h], kh[h], _NT), 0.0) for h in heads]
    rhs = [_mm1(ah[h], s0[h], _NT) + _mm1(a_ak[h], vh[h]) for h in heads]
    y0 = [_mm1(rh[h], s0[h], _NT) + _mm1(a_rk[h], vh[h]) for h in heads]
    pw = a_ab
    u = [rhs[h] + _mmi(pw[h], rhs[h]) for h in heads]
    for _ in range(int(math.log2(L)) - 1):
        pw = [_mmi(pw[h], pw[h]) for h in heads]
        u = [u[h] + _mmi(pw[h], u[h]) for h in heads]
    ys = [y0[h] + _mm1(a_rb[h], u[h]) for h in heads]
    for h in heads:
        s_ref[h] = (s0[h] * g_last[:, sls[h]] + _mm1(u[h], bl[:, sls[h]], _TN)
                    + _mm1(vh[h], kl[:, sls[h]], _TN))

    y = jnp.concatenate(ys, axis=-1)
    mean = seg_sum(y) * (1.0 / N)
    d = y - mean
    var = seg_sum(d * d) * (1.0 / N)
    y = d * lax.rsqrt(var + GN_EPS) * lnw_ref[...] + lnb_ref[...]
    y = y + seg_sum(r * k2 * rk_ref[...]) * v
    o_ref[...] = (y * gate).astype(o_ref.dtype)


def _rwkv7(u, p, *, batch, seq):
    L = RWKV_CHUNK
    nt = seq // L
    col = lambda j: pl.BlockSpec((L, B_WIDTH), lambda b, c, j=j: (b * nt + c, j))
    tail = pl.BlockSpec((L, B_TAIL), lambda b, c: (b * nt + c, (4 * A_WIDTH + 3 * B_WIDTH) // B_TAIL))
    full = lambda a: pl.BlockSpec(a.shape, lambda b, c: (0,) * a.ndim)
    params = [p["mu_r"], p["mu_k"], p["mu_v"], p["mu_t"], p["w0"], p["w2"], p["a0"], p["a2"],
              p["g2"], p["kk"], p["ka"], p["rk"], p["lnw"], p["lnb"]]
    return pl.pallas_call(
        _rwkv7_kernel,
        grid=(batch, nt),
        in_specs=[col(4), col(5), col(6), tail] + [full(a) for a in params],
        out_specs=pl.BlockSpec((L, B_WIDTH), lambda b, c: (b * nt + c, 0)),
        out_shape=jax.ShapeDtypeStruct((batch * seq, B_WIDTH), BF16),
        scratch_shapes=[pltpu.VMEM((B_HEADS, B_HEAD, B_HEAD), F32),
                        pltpu.VMEM((8, B_WIDTH), F32), pltpu.VMEM((8, B_WIDTH), F32),
                        pltpu.VMEM((8, B_WIDTH), F32), pltpu.VMEM((8, B_TAIL), F32)],
        compiler_params=_cparams(("parallel", "arbitrary")),
        name="rwkv7_mix",
    )(u, u, u, u, *params)


def _s5_kernel(u_ref, w_ref, wsw_ref, t_ref, m_ref, la_ref, lb_ref, lbs_ref, y_ref,
               z_ref, zs_ref, hp_ref, *, batch):
    n = u_ref.shape[1]
    u = u_ref[0].astype(BF16)
    z_ref[...] = jnp.dot(u, w_ref[0], preferred_element_type=F32)
    zs_ref[...] = jnp.dot(u, wsw_ref[0], preferred_element_type=F32)
    la = la_ref[0]
    lb = lb_ref[0]
    lbs = lbs_ref[0]

    def body(i, carry):
        h, hs = carry
        rows = pl.ds(pl.multiple_of(i * batch, batch), batch)
        hp_ref[rows, :] = h
        hn = la * h + lb * hs + z_ref[rows, :]
        hsn = la * hs + lbs * h + zs_ref[rows, :]
        return hn, hsn

    zero = jnp.zeros((batch, 2 * C_STATE), F32)
    lax.fori_loop(0, n // batch, body, (zero, zero))
    y = jnp.dot(u, t_ref[0], preferred_element_type=F32)
    y_ref[0] = y + jnp.dot(hp_ref[...].astype(BF16), m_ref[0], preferred_element_type=F32)


def _s5_scan(ug, prm, *, batch):
    g, n, w = ug.shape
    blk = lambda a: pl.BlockSpec((1,) + a.shape[1:], lambda i: (i,) + (0,) * (a.ndim - 1))
    ins = [ug, prm["w"], prm["wsw"], prm["t"], prm["m"], prm["la"], prm["lb"], prm["lbs"]]
    return pl.pallas_call(
        functools.partial(_s5_kernel, batch=batch),
        grid=(g,),
        in_specs=[blk(a) for a in ins],
        out_specs=pl.BlockSpec((1, n, w), lambda i: (i, 0, 0)),
        out_shape=jax.ShapeDtypeStruct((g, n, w), F32),
        scratch_shapes=[pltpu.VMEM((n, 2 * C_STATE), F32), pltpu.VMEM((n, 2 * C_STATE), F32),
                        pltpu.VMEM((n, 2 * C_STATE), F32)],
        compiler_params=_cparams(("parallel",)),
        name="s5_scan",
    )(*ins)


def _s5_glu_kernel(y_ref, u_ref, d_ref, gw_ref, gb_ref, o_ref):
    z = _gelu_tanh(y_ref[...] + d_ref[...] * u_ref[...].astype(F32))
    gate = jnp.dot(z.astype(BF16), gw_ref[...], preferred_element_type=F32) + gb_ref[...]
    o_ref[...] = (z * _sigmoid(gate)).astype(o_ref.dtype)


def _s5_glu(y, u, d, gw, gb, *, tm):
    m = y.shape[0]
    ucol = (4 * D_WIDTH_PAD) // C_WIDTH
    vec = pl.BlockSpec((1, C_WIDTH), lambda i: (0, 0))
    return pl.pallas_call(
        _s5_glu_kernel,
        grid=(m // tm,),
        in_specs=[pl.BlockSpec((tm, C_WIDTH), lambda i: (i, 0)),
                  pl.BlockSpec((tm, C_WIDTH), lambda i: (i, ucol)),
                  vec, pl.BlockSpec((C_WIDTH, C_WIDTH), lambda i: (0, 0)), vec],
        out_specs=pl.BlockSpec((tm, C_WIDTH), lambda i: (i, 0)),
        out_shape=jax.ShapeDtypeStruct((m, C_WIDTH), BF16),
        compiler_params=_cparams(("parallel",)),
        name="s5_glu",
    )(y, u, d, gw, gb)


def _s5_params(lam_re, lam_im, log_step, b_re, b_im, c_re, c_im):
    L = S5_CHUNK
    step = jnp.exp(log_step)[:, None]
    mag = jnp.exp(lam_re * step)
    lr, li = mag * jnp.cos(lam_im * step), mag * jnp.sin(lam_im * step)
    den = lam_re * lam_re + lam_im * lam_im
    nr, ni = lr - 1.0, li
    coef_re = (nr * lam_re + ni * lam_im) / den
    coef_im = (ni * lam_re - nr * lam_im) / den
    bb_re = coef_re[..., None] * b_re - coef_im[..., None] * b_im
    bb_im = coef_re[..., None] * b_im + coef_im[..., None] * b_re
    pr, pi = [jnp.ones_like(lr)], [jnp.zeros_like(lr)]
    for _ in range(L):
        pr.append(pr[-1] * lr - pi[-1] * li)
        pi.append(pr[-2] * li + pi[-1] * lr)
    pr, pi = jnp.stack(pr), jnp.stack(pi)
    hp = lax.Precision.HIGHEST
    zr = pr[:L, :, :, None] * bb_re[None] - pi[:L, :, :, None] * bb_im[None]
    zi = pr[:L, :, :, None] * bb_im[None] + pi[:L, :, :, None] * bb_re[None]
    kern = (jnp.einsum("gop,lgpi->lgoi", c_re, zr, precision=hp)
            - jnp.einsum("gop,lgpi->lgoi", c_im, zi, precision=hp))
    s_idx = jnp.arange(L)[:, None]
    t_idx = jnp.arange(L)[None, :]
    lag = jnp.clip(t_idx - s_idx, 0, L - 1)
    tt = jnp.where((t_idx >= s_idx)[:, :, None, None, None], kern[lag], 0.0)
    tmat = tt.transpose(2, 0, 4, 1, 3).reshape(C_GROUPS, L * C_GROUP, L * C_GROUP)
    wr = zr[::-1].transpose(1, 0, 3, 2).reshape(C_GROUPS, L * C_GROUP, C_STATE)
    wi = zi[::-1].transpose(1, 0, 3, 2).reshape(C_GROUPS, L * C_GROUP, C_STATE)
    w = jnp.concatenate([wr, wi], axis=-1)
    wsw = jnp.concatenate([wi, wr], axis=-1)
    qr, qi = pr[1:], pi[1:]
    m_re = c_re[None] * qr[:, :, None, :] - c_im[None] * qi[:, :, None, :]
    m_im = -(c_re[None] * qi[:, :, None, :] + c_im[None] * qr[:, :, None, :])
    mm = jnp.concatenate([m_re, m_im], axis=-1)
    mmat = mm.transpose(1, 3, 0, 2).reshape(C_GROUPS, 2 * C_STATE, L * C_GROUP)
    la = jnp.concatenate([pr[L], pr[L]], axis=-1)[:, None, :]
    lb = jnp.concatenate([-pi[L], pi[L]], axis=-1)[:, None, :]
    return {"w": w.astype(BF16), "wsw": wsw.astype(BF16), "t": tmat.astype(BF16),
            "m": mmat.astype(BF16), "la": la, "lb": lb, "lbs": -lb}


def _mlstm_kernel(q_ref, k_ref, v_ref, o_ref, gt_ref, cq_ref, ck_ref, gb_ref, gain_ref, y_ref,
                  c_ref, n_ref, m_ref, pq_ref, pk_ref):
    L = MLSTM_CHUNK
    P = D_HEAD_PAD

    @pl.when(pl.program_id(1) == 0)
    def _():
        c_ref[...] = jnp.zeros_like(c_ref)
        n_ref[...] = jnp.zeros_like(n_ref)
        m_ref[...] = jnp.full(m_ref.shape, M_INIT, F32)
        pq_ref[...] = jnp.zeros_like(pq_ref)
        pk_ref[...] = jnp.zeros_like(pk_ref)

    def conv(x_ref, prev_ref, w_ref):
        x = x_ref[...].astype(F32)
        xe = jnp.concatenate([prev_ref[...], x], axis=0)
        prev_ref[...] = x[L - 8:L, :]
        acc = xe[8:8 + L, :] * w_ref[D_CONV - 1:D_CONV, :]
        for j in range(D_CONV - 1):
            off = 8 - (D_CONV - 1) + j
            acc = acc + xe[off:off + L, :] * w_ref[j:j + 1, :]
        return _silu(acc)

    q = conv(q_ref, pq_ref, cq_ref)
    k = conv(k_ref, pk_ref, ck_ref) * (1.0 / math.sqrt(D_HEAD))
    v = v_ref[...].astype(F32)

    gates = gt_ref[...].astype(F32) + gb_ref[...]
    col = _iota(gates.shape, 1)
    is_f = (col >= D_HEADS) & (col < 2 * D_HEADS)
    gates = jnp.where(is_f, jnp.minimum(gates, 0.0) - jnp.log(1.0 + jnp.exp(-jnp.abs(gates))), gates)
    tri = jnp.where(_tril_mask(L), 1.0, 0.0).astype(BF16)
    bcum = _mm_sel(tri, gates)
    gates_t = gates.T
    bcum_t = bcum.T
    lower = _tril_mask(L)

    for h in range(D_HEADS):
        sl = slice(h * P, (h + 1) * P)
        qh, kh, vh = q[:, sl], k[:, sl], v[:, sl]
        b_col = bcum[:, D_HEADS + h:D_HEADS + h + 1]
        b_row = bcum_t[D_HEADS + h:D_HEADS + h + 1, :]
        i_col = gates[:, h:h + 1]
        i_row = gates_t[h:h + 1, :]
        m_st = m_ref[h]
        c_st = c_ref[h]
        n_st = n_ref[h]

        dmat = jnp.where(lower, b_col - b_row + i_row, -jnp.inf)
        e_inter = b_col + m_st
        m_t = jnp.maximum(e_inter, jnp.max(dmat, axis=-1, keepdims=True))
        w_intra = jnp.exp(dmat - m_t)
        s_inter = jnp.exp(e_inter - m_t)
        qk = _mm1(qh, kh, _NT) * w_intra
        num = s_inter * _mm1(qh, c_st) + _mm1(qk, vh)
        den = s_inter * jnp.sum(qh * n_st, axis=-1, keepdims=True) + jnp.sum(qk, axis=-1, keepdims=True)
        hh = num / jnp.maximum(jnp.abs(den), jnp.exp(-m_t))
        m_new = m_t[L - 1:L, :]
        b_last = b_col[L - 1:L, :]
        w_last = jnp.exp(b_last - b_col + i_col - m_new)
        d_prev = jnp.exp(b_last + m_st - m_new)
        kw = kh * w_last
        c_ref[h] = d_prev * c_st + _mm1(kw, vh, _TN)
        n_ref[h] = d_prev * n_st + jnp.sum(kw, axis=0, keepdims=True)
        m_ref[h] = m_new

        ms = jnp.sum(hh * hh, axis=-1, keepdims=True) * (1.0 / D_HEAD)
        out = hh * lax.rsqrt(ms + RMS_EPS) * gain_ref[:, sl] * _sigmoid(o_ref[:, sl].astype(F32))
        y_ref[:, sl] = out.astype(y_ref.dtype)


def _mlstm(u, conv_q, conv_k, gate_bias, gain, *, batch, seq):
    L = MLSTM_CHUNK
    nt = seq // L
    W = D_WIDTH_PAD
    col = lambda j: pl.BlockSpec((L, W), lambda b, c, j=j: (b * nt + c, j))
    gcol = pl.BlockSpec((L, D_GATE_PAD), lambda b, c: (b * nt + c, (4 * W + C_WIDTH) // D_GATE_PAD))
    full = lambda a: pl.BlockSpec(a.shape, lambda b, c: (0,) * a.ndim)
    return pl.pallas_call(
        _mlstm_kernel,
        grid=(batch, nt),
        in_specs=[col(0), col(1), col(2), col(3), gcol, full(conv_q), full(conv_k),
                  full(gate_bias), full(gain)],
        out_specs=pl.BlockSpec((L, W), lambda b, c: (b * nt + c, 0)),
        out_shape=jax.ShapeDtypeStruct((batch * seq, W), BF16),
        scratch_shapes=[pltpu.VMEM((D_HEADS, D_HEAD_PAD, D_HEAD_PAD), F32),
                        pltpu.VMEM((D_HEADS, 1, D_HEAD_PAD), F32),
                        pltpu.VMEM((D_HEADS, 1, 1), F32),
                        pltpu.VMEM((8, W), F32), pltpu.VMEM((8, W), F32)],
        compiler_params=_cparams(("parallel", "arbitrary")),
        name="mlstm_mix",
    )(u, u, u, u, u, conv_q, conv_k, gate_bias, gain)


def _pad_cols(a, n):
    return jnp.pad(a, [(0, 0)] * (a.ndim - 1) + [(0, n - a.shape[-1])])


def _pad_heads(a, axis):
    shp = a.shape
    a = a.reshape(shp[:axis] + (D_HEADS, D_HEAD) + shp[axis + 1:])
    pad = [(0, 0)] * a.ndim
    pad[axis + 1] = (0, D_HEAD_PAD - D_HEAD)
    a = jnp.pad(a, pad)
    return a.reshape(shp[:axis] + (D_WIDTH_PAD,) + shp[axis + 1:])


def _row(v):
    return v.reshape(1, -1).astype(F32)


def kernel(x, norm_mix, norm_ffn, norm_final, w_in_even, w_out_even, lb_table, a_norm, b_mu, b_w0, b_w2, b_a0, b_a2, b_g2, b_kk, b_ka, b_rk, b_ln_w, b_ln_b, w_in_odd, w_out_odd, c_lam_re, c_lam_im, c_log_step, c_b_re, c_b_im, c_c_re, c_c_im, c_d, c_glu_w, c_glu_b, d_conv_q, d_conv_k, d_i_bias, d_f_bias, d_norm, ffn_gate, ffn_up, ffn_down):
    batch, seq, d = x.shape
    m = batch * seq
    xf = x.reshape(m, d).astype(F32)
    lower_bounds = jnp.cumsum(jax.nn.softmax(lb_table.astype(F32), axis=0), axis=0)

    w_in = _pad_cols(w_in_even[0], EVEN_IN_PAD).astype(BF16)
    u = _norm_matmul(xf, _row(norm_mix[0]), w_in, tm=1024, tn=768)
    ya = _hgrn2(u, _row(lower_bounds[0]), _row(a_norm[0]), batch=batch, seq=seq)

    mu = b_mu[0].astype(F32)
    lora_rows = lambda w, off: jnp.pad(w.astype(F32), ((off, B_TAIL - off - w.shape[0]), (0, 0)))
    rw = {
        "mu_r": _row(mu[:B_WIDTH]), "mu_k": _row(mu[B_WIDTH:2 * B_WIDTH]),
        "mu_v": _row(mu[2 * B_WIDTH:3 * B_WIDTH]), "mu_t": _row(_pad_cols(mu[3 * B_WIDTH:], B_TAIL)),
        "w0": _row(b_w0[0]), "w2": lora_rows(b_w2[0], 0),
        "a0": _row(b_a0[0]), "a2": lora_rows(b_a2[0], B_DECAY_LORA),
        "g2": lora_rows(b_g2[0], B_DECAY_LORA + B_AAA_LORA).astype(BF16),
        "kk": _row(b_kk[0]), "ka": _row(b_ka[0]), "rk": _row(b_rk[0]),
        "lnw": _row(b_ln_w[0]), "lnb": _row(b_ln_b[0]),
    }
    yb = _rwkv7(u, rw, batch=batch, seq=seq)
    wo = w_out_even[0].astype(BF16)
    xf = _out_proj(ya, yb, wo[:A_WIDTH], wo[A_WIDTH:], xf, tm=512)
    xf = _ffn(xf, _row(norm_ffn[0]), ffn_gate[0].astype(BF16), ffn_up[0].astype(BF16),
              ffn_down[0].astype(BF16), _row(norm_final), tm=512, th=1408, final_norm=False)

    wi = w_in_odd[0].astype(F32)
    seg = lambda j: _pad_heads(wi[:, C_WIDTH + j * D_WIDTH:C_WIDTH + (j + 1) * D_WIDTH], 1)
    w_in = jnp.concatenate(
        [seg(0), seg(1), seg(2), seg(3), wi[:, :C_WIDTH],
         _pad_cols(wi[:, C_WIDTH + 4 * D_WIDTH:], D_GATE_PAD)], axis=1).astype(BF16)
    u = _norm_matmul(xf, _row(norm_mix[1]), w_in, tm=1024, tn=640)

    nch = seq // S5_CHUNK
    uc = u[:, 4 * D_WIDTH_PAD:4 * D_WIDTH_PAD + C_WIDTH]
    ug = uc.reshape(batch, nch, S5_CHUNK, C_GROUPS, C_GROUP).transpose(3, 1, 0, 2, 4)
    ug = ug.reshape(C_GROUPS, nch * batch, S5_CHUNK * C_GROUP)
    s5p = _s5_params(c_lam_re[0].astype(F32), c_lam_im[0].astype(F32), c_log_step[0].astype(F32),
                     c_b_re[0].astype(F32), c_b_im[0].astype(F32), c_c_re[0].astype(F32),
                     c_c_im[0].astype(F32))
    yg = _s5_scan(ug, s5p, batch=batch)
    ys5 = yg.reshape(C_GROUPS, nch, batch, S5_CHUNK, C_GROUP).transpose(2, 1, 3, 0, 4).reshape(m, C_WIDTH)
    yc = _s5_glu(ys5, u, _row(c_d[0]), c_glu_w[0].astype(BF16), _row(c_glu_b[0]), tm=1024)

    gate_bias = _row(_pad_cols(jnp.concatenate([d_i_bias[0], d_f_bias[0]]).astype(F32), D_GATE_PAD))
    yd = _mlstm(u, _pad_heads(d_conv_q[0].astype(F32), 1), _pad_heads(d_conv_k[0].astype(F32), 1),
                gate_bias, _row(_pad_heads(d_norm[0].astype(F32), 0)), batch=batch, seq=seq)
    wo = w_out_odd[0].astype(F32)
    xf = _out_proj(yc, yd, wo[:C_WIDTH].astype(BF16), _pad_heads(wo[C_WIDTH:], 0).astype(BF16), xf, tm=512)
    xf = _ffn(xf, _row(norm_ffn[1]), ffn_gate[1].astype(BF16), ffn_up[1].astype(BF16),
              ffn_down[1].astype(BF16), _row(norm_final), tm=512, th=1408, final_norm=True)
    return xf.reshape(batch, seq, d).astype(x.dtype)
```

```python
import functools
import math

import jax
import jax.numpy as jnp
from jax import lax
from jax.experimental import pallas as pl
from jax.experimental.pallas import tpu as pltpu

F32 = jnp.float32
BF16 = jnp.bfloat16

RMS_EPS = 1e-6
GN_EPS = 64e-5
M_INIT = -1e30

D_MODEL = 1024
A_WIDTH, A_HEADS, A_KEY = 512, 4, 128
B_WIDTH, B_HEADS, B_HEAD = 512, 8, 64
B_DECAY_LORA, B_AAA_LORA, B_GATE_LORA = 32, 32, 96
B_TAIL = 256
EVEN_IN_PAD = 4 * A_WIDTH + 3 * B_WIDTH + B_TAIL
C_WIDTH, C_GROUP, C_GROUPS, C_STATE = 256, 16, 16, 64
D_HEADS, D_HEAD, D_HEAD_PAD = 4, 192, 256
D_WIDTH = D_HEADS * D_HEAD
D_WIDTH_PAD = D_HEADS * D_HEAD_PAD
D_CONV = 4
D_GATE_PAD = 128
ODD_IN_PAD = 4 * D_WIDTH_PAD + C_WIDTH + D_GATE_PAD
FFN_HIDDEN = 2816

HGRN_TILE = 128
HGRN_SUB = 16
RWKV_CHUNK = 64
RWKV_TILE = 256
S5_CHUNK = 16
MLSTM_CHUNK = 64
MLSTM_TILE = 128

VMEM_LIMIT = 56 * 1024 * 1024


def _cparams(sem):
    return pltpu.CompilerParams(dimension_semantics=sem, vmem_limit_bytes=VMEM_LIMIT)


def _split2(x):
    hi = x.astype(BF16)
    lo = (x - hi.astype(F32)).astype(BF16)
    return hi, lo


def _split3(x):
    hi = x.astype(BF16)
    r1 = x - hi.astype(F32)
    mid = r1.astype(BF16)
    lo = (r1 - mid.astype(F32)).astype(BF16)
    return hi, mid, lo


def _dg(a, b, dims):
    return lax.dot_general(a, b, (dims, ((), ())), preferred_element_type=F32)


_NN = ((1,), (0,))
_NT = ((1,), (1,))
_TN = ((0,), (0,))


def _mm1(a, b, dims=_NN):
    return _dg(a.astype(BF16), b.astype(BF16), dims)


def _mm3(a, b, dims=_NN):
    ah, al = _split2(a)
    bh, bl = _split2(b)
    return _dg(ah, bh, dims) + (_dg(ah, bl, dims) + _dg(al, bh, dims))


def _mm2(a, b, dims=_NN):
    ah, al = _split2(a)
    bh = b.astype(BF16)
    return _dg(ah, bh, dims) + _dg(al, bh, dims)


_mmi = _mm1


def _mm_sel(sel_bf16, x, dims=_NN):
    h, m, l = _split3(x)
    return _dg(sel_bf16, h, dims) + (_dg(sel_bf16, m, dims) + _dg(sel_bf16, l, dims))


def _x_sel(x, sel_bf16):
    n = x.shape[0]
    r = _dg(jnp.concatenate(_split3(x), axis=0), sel_bf16, _NN)
    return r[:n] + (r[n:2 * n] + r[2 * n:])


def _sigmoid(x):
    return 1.0 / (1.0 + jnp.exp(-x))


def _silu(x):
    return x * _sigmoid(x)


def _softplus(x):
    return jnp.maximum(x, 0.0) + jnp.log(1.0 + jnp.exp(-jnp.abs(x)))


def _gelu_tanh(x):
    c = math.sqrt(2.0 / math.pi)
    return 0.5 * x * (1.0 + jnp.tanh(c * (x + 0.044715 * (x * x * x))))


def _iota(shape, dim):
    return lax.broadcasted_iota(jnp.int32, shape, dim)


def _tril_mask(n, strict=False):
    r = _iota((n, n), 0)
    c = _iota((n, n), 1)
    return (r > c) if strict else (r >= c)


def _norm_matmul_kernel(x_ref, g_ref, w_ref, o_ref, h_ref):
    @pl.when(pl.program_id(1) == 0)
    def _():
        x = x_ref[...]
        ms = jnp.mean(x * x, axis=-1, keepdims=True)
        h_ref[...] = (x * lax.rsqrt(ms + RMS_EPS) * g_ref[...]).astype(BF16)

    o_ref[...] = jnp.dot(h_ref[...], w_ref[...], preferred_element_type=F32).astype(o_ref.dtype)


def _norm_matmul(x, g, w, *, tm, tn):
    m, d = x.shape
    n = w.shape[1]
    return pl.pallas_call(
        _norm_matmul_kernel,
        grid=(m // tm, n // tn),
        in_specs=[
            pl.BlockSpec((tm, d), lambda i, j: (i, 0)),
            pl.BlockSpec((1, d), lambda i, j: (0, 0)),
            pl.BlockSpec((d, tn), lambda i, j: (0, j)),
        ],
        out_specs=pl.BlockSpec((tm, tn), lambda i, j: (i, j)),
        out_shape=jax.ShapeDtypeStruct((m, n), BF16),
        scratch_shapes=[pltpu.VMEM((tm, d), BF16)],
        compiler_params=_cparams(("parallel", "arbitrary")),
        name="norm_in_proj",
    )(x, g, w)


def _out_proj_kernel(a_ref, b_ref, wa_ref, wb_ref, x_ref, o_ref):
    y = jnp.dot(a_ref[...], wa_ref[...], preferred_element_type=F32)
    y = y + jnp.dot(b_ref[...], wb_ref[...], preferred_element_type=F32)
    o_ref[...] = x_ref[...] + y


def _out_proj(a, b, wa, wb, x, *, tm):
    m, d = x.shape
    ka, kb = a.shape[1], b.shape[1]
    return pl.pallas_call(
        _out_proj_kernel,
        grid=(m // tm,),
        in_specs=[
            pl.BlockSpec((tm, ka), lambda i: (i, 0)),
            pl.BlockSpec((tm, kb), lambda i: (i, 0)),
            pl.BlockSpec((ka, d), lambda i: (0, 0)),
            pl.BlockSpec((kb, d), lambda i: (0, 0)),
            pl.BlockSpec((tm, d), lambda i: (i, 0)),
        ],
        out_specs=pl.BlockSpec((tm, d), lambda i: (i, 0)),
        out_shape=jax.ShapeDtypeStruct((m, d), F32),
        compiler_params=_cparams(("parallel",)),
        name="out_proj_residual",
    )(a, b, wa, wb, x)


def _ffn_kernel(x_ref, g_ref, wg_ref, wu_ref, wd_ref, gf_ref, o_ref, h_ref, acc_ref, *, final_norm):
    j = pl.program_id(1)

    @pl.when(j == 0)
    def _():
        x = x_ref[...]
        ms = jnp.mean(x * x, axis=-1, keepdims=True)
        h_ref[...] = (x * lax.rsqrt(ms + RMS_EPS) * g_ref[...]).astype(BF16)
        acc_ref[...] = jnp.zeros_like(acc_ref)

    h = h_ref[...]
    gate = jnp.dot(h, wg_ref[...], preferred_element_type=F32)
    up = jnp.dot(h, wu_ref[...], preferred_element_type=F32)
    act = (_silu(gate) * up).astype(BF16)
    acc_ref[...] += jnp.dot(act, wd_ref[...], preferred_element_type=F32)

    @pl.when(j == pl.num_programs(1) - 1)
    def _():
        y = x_ref[...] + acc_ref[...]
        if final_norm:
            ms = jnp.mean(y * y, axis=-1, keepdims=True)
            y = y * lax.rsqrt(ms + RMS_EPS) * gf_ref[...]
        o_ref[...] = y


def _ffn(x, g, wg, wu, wd, gf, *, tm, th, final_norm):
    m, d = x.shape
    hid = wg.shape[1]
    return pl.pallas_call(
        functools.partial(_ffn_kernel, final_norm=final_norm),
        grid=(m // tm, hid // th),
        in_specs=[
            pl.BlockSpec((tm, d), lambda i, j: (i, 0)),
            pl.BlockSpec((1, d), lambda i, j: (0, 0)),
            pl.BlockSpec((d, th), lambda i, j: (0, j)),
            pl.BlockSpec((d, th), lambda i, j: (0, j)),
            pl.BlockSpec((th, d), lambda i, j: (j, 0)),
            pl.BlockSpec((1, d), lambda i, j: (0, 0)),
        ],
        out_specs=pl.BlockSpec((tm, d), lambda i, j: (i, 0)),
        out_shape=jax.ShapeDtypeStruct((m, d), F32),
        scratch_shapes=[pltpu.VMEM((tm, d), BF16), pltpu.VMEM((tm, d), F32)],
        compiler_params=_cparams(("parallel", "arbitrary")),
        name="swiglu_ffn",
    )(x, g, wg, wu, wd, gf)


def _hgrn2_kernel(q_ref, f_ref, i_ref, g_ref, lb_ref, gain_ref, o_ref, s_ref):
    T, SUB = HGRN_TILE, HGRN_SUB
    nb = T // SUB

    @pl.when(pl.program_id(1) == 0)
    def _():
        s_ref[...] = jnp.zeros_like(s_ref)

    r = _iota((T, T), 0)
    c = _iota((T, T), 1)
    tri = jnp.where((r >= c) & ((r // SUB) == (c // SUB)), 1.0, 0.0).astype(BF16)
    HALF = SUB // 2
    rowh = _iota((nb, HALF, A_KEY), 1)
    heads = range(A_HEADS)
    sls = [slice(h * A_KEY, (h + 1) * A_KEY) for h in heads]
    intra, qts, kts, v3s, decs = [], [], [], [], []

    for h in heads:
        sl = sls[h]
        q = _silu(q_ref[:, sl].astype(F32))
        lb = lb_ref[:, sl]
        f = lb + (1.0 - lb) * _sigmoid(f_ref[:, sl].astype(F32))
        lf = jnp.log(f)
        k = 1.0 - f
        v = i_ref[:, sl].astype(F32)
        b = _mm_sel(tri, lf)
        b3 = b.reshape(nb, SUB, A_KEY)
        q3 = q.reshape(nb, SUB, A_KEY)
        k3 = k.reshape(nb, SUB, A_KEY)
        v3 = v.reshape(nb, SUB, A_KEY)
        bk3 = b3 - jnp.log(k3)
        b_up, b_lo = b3[:, :HALF], b3[:, HALF:]
        q_up, q_lo = q3[:, :HALF], q3[:, HALF:]
        acc_up = jnp.zeros((nb, HALF, A_KEY), F32)
        acc_lo = jnp.zeros((nb, HALF, A_KEY), F32)
        for s in range(HALF):
            bs = bk3[:, s:s + 1, :]
            vs = v3[:, s:s + 1, :]
            e_up = jnp.exp(jnp.where(rowh >= s, b_up - bs, -jnp.inf))
            acc_up = acc_up + jnp.sum(q_up * e_up, axis=-1, keepdims=True) * vs
            acc_lo = acc_lo + jnp.sum(q_lo * jnp.exp(b_lo - bs), axis=-1, keepdims=True) * vs
        for s in range(HALF):
            bs = bk3[:, HALF + s:HALF + s + 1, :]
            e_lo = jnp.exp(jnp.where(rowh >= s, b_lo - bs, -jnp.inf))
            acc_lo = acc_lo + jnp.sum(q_lo * e_lo, axis=-1, keepdims=True) * v3[:, HALF + s:HALF + s + 1, :]
        intra.append(jnp.concatenate([acc_up, acc_lo], axis=1).reshape(T, A_KEY))

        bend = b3[:, SUB - 1:SUB, :]
        qts.append((q3 * jnp.exp(b3)).astype(BF16))
        kts.append((k3 * jnp.exp(bend - b3)).astype(BF16))
        v3s.append(v3.astype(BF16))
        decs.append(jnp.exp(bend))

    inc = [[_dg(v3s[h][i], kts[h][i], _TN) for i in range(nb)] for h in heads]
    for h in heads:
        st = s_ref[h]
        outs = []
        for i in range(nb):
            outs.append(_dg(qts[h][i], st.astype(BF16), _NT))
            st = st * decs[h][i] + inc[h][i]
        s_ref[h] = st
        o = intra[h] + jnp.concatenate(outs, axis=0)
        o = o * lax.rsqrt(jnp.mean(o * o, axis=-1, keepdims=True) + RMS_EPS) * gain_ref[:, sls[h]]
        o_ref[:, sls[h]] = (o * _silu(g_ref[:, sls[h]].astype(F32))).astype(o_ref.dtype)


def _hgrn2(u, lb, gain, *, batch, seq):
    T = HGRN_TILE
    nt = seq // T
    col = lambda j: pl.BlockSpec((T, A_WIDTH), lambda b, c, j=j: (b * nt + c, j))
    vec = pl.BlockSpec((1, A_WIDTH), lambda b, c: (0, 0))
    return pl.pallas_call(
        _hgrn2_kernel,
        grid=(batch, nt),
        in_specs=[col(0), col(1), col(2), col(3), vec, vec],
        out_specs=pl.BlockSpec((T, A_WIDTH), lambda b, c: (b * nt + c, 0)),
        out_shape=jax.ShapeDtypeStruct((batch * seq, A_WIDTH), BF16),
        scratch_shapes=[pltpu.VMEM((A_HEADS, A_KEY, A_KEY), F32)],
        compiler_params=_cparams(("parallel", "arbitrary")),
        name="hgrn2_mix",
    )(u, u, u, u, lb, gain)


def _rwkv7_kernel(r_ref, k_ref, v_ref, t_ref, mu_r, mu_k, mu_v, mu_t, w0_ref, w2_ref, a0_ref,
                  a2_ref, g2_ref, kk_ref, ka_ref, rk_ref, lnw_ref, lnb_ref, o_ref,
                  s_ref, cr_ref, ck_ref, cv_ref, ct_ref):
    L = RWKV_CHUNK
    T = RWKV_TILE
    N = B_HEAD
    nc = T // L

    @pl.when(pl.program_id(1) == 0)
    def _():
        s_ref[...] = jnp.zeros_like(s_ref)
        cr_ref[...] = jnp.zeros_like(cr_ref)
        ck_ref[...] = jnp.zeros_like(ck_ref)
        cv_ref[...] = jnp.zeros_like(cv_ref)
        ct_ref[...] = jnp.zeros_like(ct_ref)

    def shift_mix(x_ref, carry_ref, mu_ref):
        x = x_ref[...].astype(F32)
        row = _iota(x.shape, 0)
        prev = jnp.where(row == 0, carry_ref[7:8, :], pltpu.roll(x, 1, 0))
        carry_ref[...] = x[T - 8:T, :]
        return x + (prev - x) * mu_ref[...]

    r = shift_mix(r_ref, cr_ref, mu_r)
    k = shift_mix(k_ref, ck_ref, mu_k)
    v = shift_mix(v_ref, cv_ref, mu_v)
    tl = shift_mix(t_ref, ct_ref, mu_t)

    wl = _mm3(jnp.tanh(tl), w2_ref[...])
    al = _mm3(tl, a2_ref[...])
    gate = _mm1(_sigmoid(tl), g2_ref[...])
    w_log = -_softplus(-(w0_ref[...] + wl)) - 0.5
    lw = -jnp.exp(w_log)
    a_lr = _sigmoid(a0_ref[...] + al)

    ind = jnp.where(_iota((B_WIDTH, 128), 0) // N == _iota((B_WIDTH, 128), 1), 1.0, 0.0).astype(BF16)
    ind_t = jnp.where(_iota((128, B_WIDTH), 1) // N == _iota((128, B_WIDTH), 0), 1.0, 0.0).astype(BF16)

    def seg_sum(x):
        return _x_sel(_x_sel(x, ind), ind_t)

    kk = k * kk_ref[...]
    kk = kk / jnp.maximum(jnp.sqrt(seg_sum(kk * kk)), 1e-12)
    k2 = k * (1.0 + (a_lr - 1.0) * ka_ref[...])
    bonus = seg_sum(r * k2 * rk_ref[...])
    aa = -kk
    bb = kk * a_lr

    rr = _iota((T, T), 0)
    cc = _iota((T, T), 1)
    tri = jnp.where((rr >= cc) & (rr // L == cc // L), 1.0, 0.0).astype(BF16)
    c = _mm_sel(tri, lw)
    enc = jnp.exp(-c)
    rt = (r * jnp.exp(c)).astype(BF16)
    at = (aa * jnp.exp(c - lw)).astype(BF16)
    bt = (bb * enc).astype(BF16)
    kt = (k2 * enc).astype(BF16)
    vb = v.astype(BF16)

    r2 = _iota((2 * L, L), 0)
    c2 = _iota((2 * L, L), 1)
    both = ((r2 < L) & (r2 > c2)) | (r2 - L >= c2)

    heads = range(B_HEADS)
    cols = lambda h: slice(h * N, (h + 1) * N)
    solved = {}
    state = {"s": [s_ref[h] for h in heads], "y": []}

    def solve_stages(ci):
        rs = slice(ci * L, (ci + 1) * L)
        ar = [jnp.concatenate([at[rs, cols(h)], rt[rs, cols(h)]], axis=0) for h in heads]
        vh = [vb[rs, cols(h)] for h in heads]
        m_b = [jnp.where(both, _dg(ar[h], bt[rs, cols(h)], _NT), 0.0) for h in heads]
        yield
        m_k = [jnp.where(both, _dg(ar[h], kt[rs, cols(h)], _NT), 0.0).astype(BF16) for h in heads]
        yield
        mkv = [_dg(m_k[h], vh[h], _NN) for h in heads]
        yield
        pw = [m_b[h][:L] for h in heads]
        x = [jnp.concatenate([ar[h][:L].astype(F32), mkv[h][:L]], axis=-1) for h in heads]
        x = [x[h] + _mm1(pw[h], x[h]) for h in heads]
        yield
        for _ in range(int(math.log2(L)) - 1):
            pw = [_mm1(pw[h], pw[h]) for h in heads]
            yield
            x = [x[h] + _mm1(pw[h], x[h]) for h in heads]
            yield
        c_last = c[(ci + 1) * L - 1:(ci + 1) * L, :]
        e_tail = jnp.exp(c_last - c[rs, :])
        solved[ci] = dict(
            wr=[jnp.concatenate([x[h][:, :N].astype(BF16), ar[h][L:]], axis=0) for h in heads],
            u0=[x[h][:, N:] for h in heads],
            a_rb=[m_b[h][L:].astype(BF16) for h in heads],
            y0=[mkv[h][L:] for h in heads], vh=vh, g_last=jnp.exp(c_last),
            bkl=jnp.concatenate([(bb[rs, :] * e_tail).astype(BF16),
                                 (k2[rs, :] * e_tail).astype(BF16)], axis=0))
        yield

    def state_stages(ci):
        q = solved.pop(ci)
        st = state["s"]
        ws = [_dg(q["wr"][h], st[h].astype(BF16), _NT) for h in heads]
        yield
        ub = [(ws[h][:L] + q["u0"][h]).astype(BF16) for h in heads]
        state["y"].append(jnp.concatenate(
            [ws[h][L:] + _dg(q["a_rb"][h], ub[h], _NN) + q["y0"][h] for h in heads], axis=-1))
        state["s"] = [st[h] * q["g_last"][:, cols(h)]
                      + _dg(jnp.concatenate([ub[h], q["vh"][h]], axis=0), q["bkl"][:, cols(h)], _TN)
                      for h in heads]
        yield

    for _ in zip(*[solve_stages(ci) for ci in range(nc)]):
        pass
    for ci in range(nc):
        for _ in state_stages(ci):
            pass
    for h in heads:
        s_ref[h] = state["s"][h]
    ys = state["y"]

    y = jnp.concatenate(ys, axis=0)
    mean = seg_sum(y) * (1.0 / N)
    d = y - mean
    var = seg_sum(d * d) * (1.0 / N)
    y = d * lax.rsqrt(var + GN_EPS) * lnw_ref[...] + lnb_ref[...]
    y = y + bonus * v
    o_ref[...] = (y * gate).astype(o_ref.dtype)


def _rwkv7(u, p, *, batch, seq):
    T = RWKV_TILE
    nt = seq // T
    col = lambda j: pl.BlockSpec((T, B_WIDTH), lambda b, c, j=j: (b * nt + c, j))
    tail = pl.BlockSpec((T, B_TAIL), lambda b, c: (b * nt + c, (4 * A_WIDTH + 3 * B_WIDTH) // B_TAIL))
    full = lambda a: pl.BlockSpec(a.shape, lambda b, c: (0,) * a.ndim)
    params = [p["mu_r"], p["mu_k"], p["mu_v"], p["mu_t"], p["w0"], p["w2"], p["a0"], p["a2"],
              p["g2"], p["kk"], p["ka"], p["rk"], p["lnw"], p["lnb"]]
    return pl.pallas_call(
        _rwkv7_kernel,
        grid=(batch, nt),
        in_specs=[col(4), col(5), col(6), tail] + [full(a) for a in params],
        out_specs=pl.BlockSpec((T, B_WIDTH), lambda b, c: (b * nt + c, 0)),
        out_shape=jax.ShapeDtypeStruct((batch * seq, B_WIDTH), BF16),
        scratch_shapes=[pltpu.VMEM((B_HEADS, B_HEAD, B_HEAD), F32),
                        pltpu.VMEM((8, B_WIDTH), F32), pltpu.VMEM((8, B_WIDTH), F32),
                        pltpu.VMEM((8, B_WIDTH), F32), pltpu.VMEM((8, B_TAIL), F32)],
        compiler_params=_cparams(("parallel", "arbitrary")),
        name="rwkv7_mix",
    )(u, u, u, u, *params)


def _s5_inc_kernel(u_ref, w_ref, z_ref):
    z_ref[...] = jnp.dot(u_ref[...], w_ref[...], preferred_element_type=F32)


def _s5_increments(u2, w, *, tm, tn):
    n, kdim = u2.shape
    ncol = w.shape[1]
    return pl.pallas_call(
        _s5_inc_kernel,
        grid=(n // tm, ncol // tn),
        in_specs=[pl.BlockSpec((tm, kdim), lambda i, j: (i, 0)),
                  pl.BlockSpec((kdim, tn), lambda i, j: (0, j))],
        out_specs=pl.BlockSpec((tm, tn), lambda i, j: (i, j)),
        out_shape=jax.ShapeDtypeStruct((n, ncol), F32),
        compiler_params=_cparams(("parallel", "parallel")),
        name="s5_increments",
    )(u2, w)


def _s5_carry_kernel(z_ref, lr_ref, li_ref, hp_ref, h_ref):
    n = z_ref.shape[0]
    half = z_ref.shape[1] // 2
    lr = lr_ref[...]
    li = li_ref[...]

    def body(i, carry):
        hr, hi = carry
        rows = pl.ds(pl.multiple_of(i * 8, 8), 8)
        zz = z_ref[rows, :]
        outs = []
        for j in range(8):
            outs.append(jnp.concatenate([hr, hi], axis=-1))
            hr, hi = (lr * hr - li * hi + zz[j:j + 1, :half],
                      lr * hi + li * hr + zz[j:j + 1, half:])
        h_ref[rows, :] = jnp.concatenate(outs, axis=0)
        return hr, hi

    zero = jnp.zeros((1, half), F32)
    lax.fori_loop(0, n // 8, body, (zero, zero))
    hp_ref[...] = h_ref[...].astype(hp_ref.dtype)


def _s5_carry(z, lr, li, *, batch):
    n, w = z.shape
    rows = n // batch
    vec = pl.BlockSpec((1, w // 2), lambda b: (0, 0))
    return pl.pallas_call(
        _s5_carry_kernel,
        grid=(batch,),
        in_specs=[pl.BlockSpec((rows, w), lambda b: (b, 0)), vec, vec],
        out_specs=pl.BlockSpec((rows, w), lambda b: (b, 0)),
        out_shape=jax.ShapeDtypeStruct((n, w), BF16),
        scratch_shapes=[pltpu.VMEM((rows, w), F32)],
        compiler_params=_cparams(("parallel",)),
        name="s5_carry",
    )(z, lr, li)


def _s5_out_kernel(u_ref, uj_ref, hp_ref, t_ref, m_ref, d_ref, gw_ref, gb_ref, o_ref):
    y = jnp.dot(u_ref[...], t_ref[...], preferred_element_type=F32)
    y = y + jnp.dot(hp_ref[...], m_ref[...], preferred_element_type=F32)
    for s in range(o_ref.shape[1] // C_WIDTH):
        cs = slice(s * C_WIDTH, (s + 1) * C_WIDTH)
        z = _gelu_tanh(y[:, cs] + d_ref[...] * uj_ref[:, cs].astype(F32))
        gate = jnp.dot(z.astype(BF16), gw_ref[...], preferred_element_type=F32) + gb_ref[...]
        o_ref[:, cs] = (z * _sigmoid(gate)).astype(o_ref.dtype)


def _s5_output(u2, hp, t, mmat, d, gw, gb, *, tm, tn):
    n, kdim = u2.shape
    kst = hp.shape[1]
    vec = pl.BlockSpec((1, C_WIDTH), lambda i, j: (0, 0))
    return pl.pallas_call(
        _s5_out_kernel,
        grid=(n // tm, kdim // tn),
        in_specs=[pl.BlockSpec((tm, kdim), lambda i, j: (i, 0)),
                  pl.BlockSpec((tm, tn), lambda i, j: (i, j)),
                  pl.BlockSpec((tm, kst), lambda i, j: (i, 0)),
                  pl.BlockSpec((kdim, tn), lambda i, j: (0, j)),
                  pl.BlockSpec((kst, tn), lambda i, j: (0, j)),
                  vec, pl.BlockSpec((C_WIDTH, C_WIDTH), lambda i, j: (0, 0)), vec],
        out_specs=pl.BlockSpec((tm, tn), lambda i, j: (i, j)),
        out_shape=jax.ShapeDtypeStruct((n, kdim), BF16),
        compiler_params=_cparams(("parallel", "parallel")),
        name="s5_output",
    )(u2, u2, hp, t, mmat, d, gw, gb)


def _s5_params(lam_re, lam_im, log_step, b_re, b_im, c_re, c_im):
    L = S5_CHUNK
    step = jnp.exp(log_step)[:, None]
    mag = jnp.exp(lam_re * step)
    lr, li = mag * jnp.cos(lam_im * step), mag * jnp.sin(lam_im * step)
    den = lam_re * lam_re + lam_im * lam_im
    nr, ni = lr - 1.0, li
    coef_re = (nr * lam_re + ni * lam_im) / den
    coef_im = (ni * lam_re - nr * lam_im) / den
    bb_re = coef_re[..., None] * b_re - coef_im[..., None] * b_im
    bb_im = coef_re[..., None] * b_im + coef_im[..., None] * b_re
    pr, pi = [jnp.ones_like(lr)], [jnp.zeros_like(lr)]
    for _ in range(L):
        pr.append(pr[-1] * lr - pi[-1] * li)
        pi.append(pr[-2] * li + pi[-1] * lr)
    pr, pi = jnp.stack(pr), jnp.stack(pi)
    hp = lax.Precision.HIGHEST
    zr = pr[:L, :, :, None] * bb_re[None] - pi[:L, :, :, None] * bb_im[None]
    zi = pr[:L, :, :, None] * bb_im[None] + pi[:L, :, :, None] * bb_re[None]
    kern = (jnp.einsum("gop,lgpi->lgoi", c_re, zr, precision=hp)
            - jnp.einsum("gop,lgpi->lgoi", c_im, zi, precision=hp))
    s_idx = jnp.arange(L)[:, None]
    t_idx = jnp.arange(L)[None, :]
    lag = jnp.clip(t_idx - s_idx, 0, L - 1)
    tt = jnp.where((t_idx >= s_idx)[:, :, None, None, None], kern[lag], 0.0)
    eye = jnp.eye(C_GROUPS, dtype=F32)
    nin = L * C_WIDTH
    nst = 2 * C_GROUPS * C_STATE
    tmat = jnp.einsum("stgoi,gh->sgitho", tt, eye).reshape(nin, nin)
    wz = jnp.stack([zr[::-1], zi[::-1]])
    w = jnp.einsum("rsgpi,gh->sgirhp", wz, eye).reshape(nin, nst)
    qr, qi = pr[1:], pi[1:]
    m_re = c_re[None] * qr[:, :, None, :] - c_im[None] * qi[:, :, None, :]
    m_im = -(c_re[None] * qi[:, :, None, :] + c_im[None] * qr[:, :, None, :])
    mmat = jnp.einsum("rtgop,gh->rgptho", jnp.stack([m_re, m_im]), eye).reshape(nst, nin)
    return {"w": w.astype(BF16), "t": tmat.astype(BF16), "m": mmat.astype(BF16),
            "lr": pr[L].reshape(1, -1), "li": pi[L].reshape(1, -1)}


def _mlstm_kernel(q_ref, k_ref, v_ref, o_ref, gt_ref, cq_ref, ck_ref, gb_ref, gain_ref, y_ref,
                  c_ref, n_ref, m_ref, pq_ref, pk_ref):
    L = MLSTM_CHUNK
    P = D_HEAD_PAD

    @pl.when(pl.program_id(1) == 0)
    def _():
        c_ref[...] = jnp.zeros_like(c_ref)
        n_ref[...] = jnp.zeros_like(n_ref)
        m_ref[...] = jnp.full(m_ref.shape, M_INIT, F32)
        pq_ref[...] = jnp.zeros_like(pq_ref)
        pk_ref[...] = jnp.zeros_like(pk_ref)

    T = MLSTM_TILE
    nc = T // L
    PREV = pq_ref.shape[0]
    tap_row = _iota((D_CONV * T, T), 0)
    src = tap_row % T + tap_row // T - (D_CONV - 1)
    sel_x = jnp.where(_iota((D_CONV * T, T), 1) == src, 1.0, 0.0).astype(BF16)
    tap_row_p = _iota((D_CONV * T, PREV), 0)
    src_p = tap_row_p % T + tap_row_p // T - (D_CONV - 1) + PREV
    sel_p = jnp.where(_iota((D_CONV * T, PREV), 1) == src_p, 1.0, 0.0).astype(BF16)

    def conv(x_ref, prev_ref, w_ref):
        x = x_ref[...]
        taps = _dg(sel_x, x, _NN) + _dg(sel_p, prev_ref[...], _NN)
        prev_ref[...] = x[T - PREV:T, :]
        acc = taps[:T, :] * w_ref[0:1, :]
        for j in range(1, D_CONV):
            acc = acc + taps[j * T:(j + 1) * T, :] * w_ref[j:j + 1, :]
        return _silu(acc)

    q = conv(q_ref, pq_ref, cq_ref)
    k = conv(k_ref, pk_ref, ck_ref) * (1.0 / math.sqrt(D_HEAD))
    vb = v_ref[...]

    gates = gt_ref[...].astype(F32) + gb_ref[...]
    col = _iota(gates.shape, 1)
    is_f = (col >= D_HEADS) & (col < 2 * D_HEADS)
    gates = jnp.where(is_f, jnp.minimum(gates, 0.0) - jnp.log(1.0 + jnp.exp(-jnp.abs(gates))), gates)
    rr = _iota((T, T), 0)
    cc = _iota((T, T), 1)
    tri = jnp.where((rr >= cc) & (rr // L == cc // L), 1.0, 0.0).astype(BF16)
    bcum = _mm_sel(tri, gates)
    gates_t = gates.T
    bcum_t = bcum.T
    lower = _tril_mask(L)

    heads = range(D_HEADS)
    sls = [slice(h * P, (h + 1) * P) for h in heads]
    qb, kb = q.astype(BF16), k.astype(BF16)
    rows = [slice(ci * L, (ci + 1) * L) for ci in range(nc)]
    qk0 = [[_dg(qb[rs, sls[h]], kb[rs, sls[h]], _NT) for h in heads] for rs in rows]
    c_st = [c_ref[h] for h in heads]
    n_st = [n_ref[h] for h in heads]
    m_st = [m_ref[h] for h in heads]

    for ci, rs in enumerate(rows):
        qc = [_dg(qb[rs, sls[h]], c_st[h].astype(BF16), _NN) for h in heads]
        gate = []
        for h in heads:
            b_col = bcum[rs, D_HEADS + h:D_HEADS + h + 1]
            b_row = bcum_t[D_HEADS + h:D_HEADS + h + 1, rs]
            i_col = gates[rs, h:h + 1]
            i_row = gates_t[h:h + 1, rs]
            dmat = jnp.where(lower, b_col - b_row + i_row, -jnp.inf)
            e_inter = b_col + m_st[h]
            m_t = jnp.maximum(e_inter, jnp.max(dmat, axis=-1, keepdims=True))
            m_new = m_t[L - 1:L, :]
            b_last = b_col[L - 1:L, :]
            gate.append(dict(
                m_t=m_t, m_new=m_new, w_intra=jnp.exp(dmat - m_t), s_inter=jnp.exp(e_inter - m_t),
                w_last=jnp.exp(b_last - b_col + i_col - m_new),
                d_prev=jnp.exp(b_last + m_st[h] - m_new)))

        qk = [qk0[ci][h] * gate[h]["w_intra"] for h in heads]
        kw = [k[rs, sls[h]] * gate[h]["w_last"] for h in heads]
        qkv = [_dg(qk[h].astype(BF16), vb[rs, sls[h]], _NN) for h in heads]
        cu = [_dg(kw[h].astype(BF16), vb[rs, sls[h]], _TN) for h in heads]

        for h in heads:
            g, sl = gate[h], sls[h]
            num = g["s_inter"] * qc[h] + qkv[h]
            den = (g["s_inter"] * jnp.sum(q[rs, sl] * n_st[h], axis=-1, keepdims=True)
                   + jnp.sum(qk[h], axis=-1, keepdims=True))
            hh = num / jnp.maximum(jnp.abs(den), jnp.exp(-g["m_t"]))
            c_st[h] = g["d_prev"] * c_st[h] + cu[h]
            n_st[h] = g["d_prev"] * n_st[h] + jnp.sum(kw[h], axis=0, keepdims=True)
            m_st[h] = g["m_new"]

            ms = jnp.sum(hh * hh, axis=-1, keepdims=True) * (1.0 / D_HEAD)
            out = hh * lax.rsqrt(ms + RMS_EPS) * gain_ref[:, sl] * _sigmoid(o_ref[rs, sl].astype(F32))
            y_ref[rs, sl] = out.astype(y_ref.dtype)

    for h in heads:
        c_ref[h] = c_st[h]
        n_ref[h] = n_st[h]
        m_ref[h] = m_st[h]


def _mlstm(u, conv_q, conv_k, gate_bias, gain, *, batch, seq):
    L = MLSTM_TILE
    nt = seq // L
    W = D_WIDTH_PAD
    col = lambda j: pl.BlockSpec((L, W), lambda b, c, j=j: (b * nt + c, j))
    gcol = pl.BlockSpec((L, D_GATE_PAD), lambda b, c: (b * nt + c, (4 * W + C_WIDTH) // D_GATE_PAD))
    full = lambda a: pl.BlockSpec(a.shape, lambda b, c: (0,) * a.ndim)
    return pl.pallas_call(
        _mlstm_kernel,
        grid=(batch, nt),
        in_specs=[col(0), col(1), col(2), col(3), gcol, full(conv_q), full(conv_k),
                  full(gate_bias), full(gain)],
        out_specs=pl.BlockSpec((L, W), lambda b, c: (b * nt + c, 0)),
        out_shape=jax.ShapeDtypeStruct((batch * seq, W), BF16),
        scratch_shapes=[pltpu.VMEM((D_HEADS, D_HEAD_PAD, D_HEAD_PAD), F32),
                        pltpu.VMEM((D_HEADS, 1, D_HEAD_PAD), F32),
                        pltpu.VMEM((D_HEADS, 1, 1), F32),
                        pltpu.VMEM((16, W), BF16), pltpu.VMEM((16, W), BF16)],
        compiler_params=_cparams(("parallel", "arbitrary")),
        name="mlstm_mix",
    )(u, u, u, u, u, conv_q, conv_k, gate_bias, gain)


def _pad_cols(a, n):
    return jnp.pad(a, [(0, 0)] * (a.ndim - 1) + [(0, n - a.shape[-1])])


def _pad_heads(a, axis):
    shp = a.shape
    a = a.reshape(shp[:axis] + (D_HEADS, D_HEAD) + shp[axis + 1:])
    pad = [(0, 0)] * a.ndim
    pad[axis + 1] = (0, D_HEAD_PAD - D_HEAD)
    a = jnp.pad(a, pad)
    return a.reshape(shp[:axis] + (D_WIDTH_PAD,) + shp[axis + 1:])


def _row(v):
    return v.reshape(1, -1).astype(F32)


def kernel(x, norm_mix, norm_ffn, norm_final, w_in_even, w_out_even, lb_table, a_norm, b_mu, b_w0, b_w2, b_a0, b_a2, b_g2, b_kk, b_ka, b_rk, b_ln_w, b_ln_b, w_in_odd, w_out_odd, c_lam_re, c_lam_im, c_log_step, c_b_re, c_b_im, c_c_re, c_c_im, c_d, c_glu_w, c_glu_b, d_conv_q, d_conv_k, d_i_bias, d_f_bias, d_norm, ffn_gate, ffn_up, ffn_down):
    batch, seq, d = x.shape
    m = batch * seq
    xf = x.reshape(m, d).astype(F32)
    lower_bounds = jnp.cumsum(jax.nn.softmax(lb_table.astype(F32), axis=0), axis=0)

    w_in = _pad_cols(w_in_even[0], EVEN_IN_PAD).astype(BF16)
    u = _norm_matmul(xf, _row(norm_mix[0]), w_in, tm=1024, tn=768)
    ya = _hgrn2(u, _row(lower_bounds[0]), _row(a_norm[0]), batch=batch, seq=seq)

    mu = b_mu[0].astype(F32)
    lora_rows = lambda w, off: jnp.pad(w.astype(F32), ((off, B_TAIL - off - w.shape[0]), (0, 0)))
    rw = {
        "mu_r": _row(mu[:B_WIDTH]), "mu_k": _row(mu[B_WIDTH:2 * B_WIDTH]),
        "mu_v": _row(mu[2 * B_WIDTH:3 * B_WIDTH]), "mu_t": _row(_pad_cols(mu[3 * B_WIDTH:], B_TAIL)),
        "w0": _row(b_w0[0]), "w2": lora_rows(b_w2[0], 0),
        "a0": _row(b_a0[0]), "a2": lora_rows(b_a2[0], B_DECAY_LORA),
        "g2": lora_rows(b_g2[0], B_DECAY_LORA + B_AAA_LORA).astype(BF16),
        "kk": _row(b_kk[0]), "ka": _row(b_ka[0]), "rk": _row(b_rk[0]),
        "lnw": _row(b_ln_w[0]), "lnb": _row(b_ln_b[0]),
    }
    yb = _rwkv7(u, rw, batch=batch, seq=seq)
    wo = w_out_even[0].astype(BF16)
    xf = _out_proj(ya, yb, wo[:A_WIDTH], wo[A_WIDTH:], xf, tm=512)
    xf = _ffn(xf, _row(norm_ffn[0]), ffn_gate[0].astype(BF16), ffn_up[0].astype(BF16),
              ffn_down[0].astype(BF16), _row(norm_final), tm=512, th=1408, final_norm=False)

    wi = w_in_odd[0].astype(F32)
    seg = lambda j: _pad_heads(wi[:, C_WIDTH + j * D_WIDTH:C_WIDTH + (j + 1) * D_WIDTH], 1)
    w_in = jnp.concatenate(
        [seg(0), seg(1), seg(2), seg(3), wi[:, :C_WIDTH],
         _pad_cols(wi[:, C_WIDTH + 4 * D_WIDTH:], D_GATE_PAD)], axis=1).astype(BF16)
    u = _norm_matmul(xf, _row(norm_mix[1]), w_in, tm=1024, tn=640)

    u2 = u[:, 4 * D_WIDTH_PAD:4 * D_WIDTH_PAD + C_WIDTH].reshape(m // S5_CHUNK, S5_CHUNK * C_WIDTH)
    s5p = _s5_params(c_lam_re[0].astype(F32), c_lam_im[0].astype(F32), c_log_step[0].astype(F32),
                     c_b_re[0].astype(F32), c_b_im[0].astype(F32), c_c_re[0].astype(F32),
                     c_c_im[0].astype(F32))
    tr = min(1024, m // S5_CHUNK)
    z = _s5_increments(u2, s5p["w"], tm=tr, tn=512)
    hp = _s5_carry(z, s5p["lr"], s5p["li"], batch=batch)
    yc = _s5_output(u2, hp, s5p["t"], s5p["m"], _row(c_d[0]), c_glu_w[0].astype(BF16),
                    _row(c_glu_b[0]), tm=tr, tn=512).reshape(m, C_WIDTH)

    gate_bias = _row(_pad_cols(jnp.concatenate([d_i_bias[0], d_f_bias[0]]).astype(F32), D_GATE_PAD))
    yd = _mlstm(u, _pad_heads(d_conv_q[0].astype(F32), 1), _pad_heads(d_conv_k[0].astype(F32), 1),
                gate_bias, _row(_pad_heads(d_norm[0].astype(F32), 0)), batch=batch, seq=seq)
    wo = w_out_odd[0].astype(F32)
    xf = _out_proj(yc, yd, wo[:C_WIDTH].astype(BF16), _pad_heads(wo[C_WIDTH:], 0).astype(BF16), xf, tm=512)
    xf = _ffn(xf, _row(norm_ffn[1]), ffn_gate[1].astype(BF16), ffn_up[1].astype(BF16),
              ffn_down[1].astype(BF16), _row(norm_final), tm=512, th=1408, final_norm=True)
    return xf.reshape(batch, seq, d).astype(x.dtype)
```

```python
import functools
import math

import jax
import jax.numpy as jnp
from jax import lax
from jax.experimental import pallas as pl
from jax.experimental.pallas import tpu as pltpu

F32 = jnp.float32
BF16 = jnp.bfloat16

RMS_EPS = 1e-6
GN_EPS = 64e-5
M_INIT = -1e30

D_MODEL = 1024
A_WIDTH, A_HEADS, A_KEY = 512, 4, 128
B_WIDTH, B_HEADS, B_HEAD = 512, 8, 64
B_DECAY_LORA, B_AAA_LORA, B_GATE_LORA = 32, 32, 96
B_TAIL = 256
EVEN_IN_PAD = 4 * A_WIDTH + 3 * B_WIDTH + B_TAIL
C_WIDTH, C_GROUP, C_GROUPS, C_STATE = 256, 16, 16, 64
D_HEADS, D_HEAD, D_HEAD_PAD = 4, 192, 256
D_WIDTH = D_HEADS * D_HEAD
D_WIDTH_PAD = D_HEADS * D_HEAD_PAD
D_CONV = 4
D_GATE_PAD = 128
ODD_IN_PAD = 4 * D_WIDTH_PAD + C_WIDTH + D_GATE_PAD
FFN_HIDDEN = 2816

HGRN_TILE = 128
HGRN_SUB = 16
RWKV_CHUNK = 64
RWKV_TILE = 256
S5_CHUNK = 16
MLSTM_CHUNK = 64
MLSTM_TILE = 256

VMEM_LIMIT = 56 * 1024 * 1024


def _cparams(sem):
    return pltpu.CompilerParams(dimension_semantics=sem, vmem_limit_bytes=VMEM_LIMIT)


def _split2(x):
    hi = x.astype(BF16)
    lo = (x - hi.astype(F32)).astype(BF16)
    return hi, lo


def _split3(x):
    hi = x.astype(BF16)
    r1 = x - hi.astype(F32)
    mid = r1.astype(BF16)
    lo = (r1 - mid.astype(F32)).astype(BF16)
    return hi, mid, lo


def _dg(a, b, dims):
    return lax.dot_general(a, b, (dims, ((), ())), preferred_element_type=F32)


_NN = ((1,), (0,))
_NT = ((1,), (1,))
_TN = ((0,), (0,))


def _mm1(a, b, dims=_NN):
    return _dg(a.astype(BF16), b.astype(BF16), dims)


def _mm3(a, b, dims=_NN):
    ah, al = _split2(a)
    bh, bl = _split2(b)
    return _dg(ah, bh, dims) + (_dg(ah, bl, dims) + _dg(al, bh, dims))


def _mm2(a, b, dims=_NN):
    ah, al = _split2(a)
    bh = b.astype(BF16)
    return _dg(ah, bh, dims) + _dg(al, bh, dims)


_mmi = _mm1


def _mm_sel(sel_bf16, x, dims=_NN):
    h, m, l = _split3(x)
    return _dg(sel_bf16, h, dims) + (_dg(sel_bf16, m, dims) + _dg(sel_bf16, l, dims))


def _x_sel(x, sel_bf16):
    n = x.shape[0]
    r = _dg(jnp.concatenate(_split3(x), axis=0), sel_bf16, _NN)
    return r[:n] + (r[n:2 * n] + r[2 * n:])


def _sigmoid(x):
    return 1.0 / (1.0 + jnp.exp(-x))


def _silu(x):
    return x * _sigmoid(x)


def _softplus(x):
    return jnp.maximum(x, 0.0) + jnp.log(1.0 + jnp.exp(-jnp.abs(x)))


def _gelu_tanh(x):
    c = math.sqrt(2.0 / math.pi)
    return 0.5 * x * (1.0 + jnp.tanh(c * (x + 0.044715 * (x * x * x))))


def _iota(shape, dim):
    return lax.broadcasted_iota(jnp.int32, shape, dim)


def _tril_mask(n, strict=False):
    r = _iota((n, n), 0)
    c = _iota((n, n), 1)
    return (r > c) if strict else (r >= c)


def _norm_matmul_kernel(x_ref, g_ref, w_ref, o_ref, *, tn):
    x = x_ref[...]
    ms = jnp.mean(x * x, axis=-1, keepdims=True)
    h = (x * lax.rsqrt(ms + RMS_EPS) * g_ref[...]).astype(BF16)
    for j in range(o_ref.shape[1] // tn):
        cs = slice(j * tn, (j + 1) * tn)
        o_ref[:, cs] = jnp.dot(h, w_ref[:, cs], preferred_element_type=F32).astype(o_ref.dtype)


def _norm_matmul(x, g, w, *, tm, tn):
    m, d = x.shape
    n = w.shape[1]
    return pl.pallas_call(
        functools.partial(_norm_matmul_kernel, tn=tn),
        grid=(m // tm,),
        in_specs=[
            pl.BlockSpec((tm, d), lambda i: (i, 0)),
            pl.BlockSpec((1, d), lambda i: (0, 0)),
            pl.BlockSpec((d, n), lambda i: (0, 0)),
        ],
        out_specs=pl.BlockSpec((tm, n), lambda i: (i, 0)),
        out_shape=jax.ShapeDtypeStruct((m, n), BF16),
        compiler_params=_cparams(("parallel",)),
        name="norm_in_proj",
    )(x, g, w)


def _ffn_kernel(a_ref, b_ref, wa_ref, wb_ref, x_ref, g_ref, wg_ref, wu_ref, wd_ref, gf_ref, o_ref,
                xs_ref, h_ref, acc_ref, *, final_norm):
    j = pl.program_id(1)

    @pl.when(j == 0)
    def _():
        x = x_ref[...] + jnp.dot(a_ref[...], wa_ref[...], preferred_element_type=F32)
        x = x + jnp.dot(b_ref[...], wb_ref[...], preferred_element_type=F32)
        xs_ref[...] = x
        ms = jnp.mean(x * x, axis=-1, keepdims=True)
        h_ref[...] = (x * lax.rsqrt(ms + RMS_EPS) * g_ref[...]).astype(BF16)
        acc_ref[...] = jnp.zeros_like(acc_ref)

    h = h_ref[...]
    gate = jnp.dot(h, wg_ref[...], preferred_element_type=F32)
    up = jnp.dot(h, wu_ref[...], preferred_element_type=F32)
    act = (_silu(gate) * up).astype(BF16)
    acc_ref[...] += jnp.dot(act, wd_ref[...], preferred_element_type=F32)

    @pl.when(j == pl.num_programs(1) - 1)
    def _():
        y = xs_ref[...] + acc_ref[...]
        if final_norm:
            ms = jnp.mean(y * y, axis=-1, keepdims=True)
            y = y * lax.rsqrt(ms + RMS_EPS) * gf_ref[...]
        o_ref[...] = y


def _out_proj_ffn(a, b, wa, wb, x, g, wg, wu, wd, gf, *, tm, th, final_norm):
    m, d = x.shape
    hid = wg.shape[1]
    ka, kb = a.shape[1], b.shape[1]
    return pl.pallas_call(
        functools.partial(_ffn_kernel, final_norm=final_norm),
        grid=(m // tm, hid // th),
        in_specs=[
            pl.BlockSpec((tm, ka), lambda i, j: (i, 0)),
            pl.BlockSpec((tm, kb), lambda i, j: (i, 0)),
            pl.BlockSpec((ka, d), lambda i, j: (0, 0)),
            pl.BlockSpec((kb, d), lambda i, j: (0, 0)),
            pl.BlockSpec((tm, d), lambda i, j: (i, 0)),
            pl.BlockSpec((1, d), lambda i, j: (0, 0)),
            pl.BlockSpec((d, th), lambda i, j: (0, j)),
            pl.BlockSpec((d, th), lambda i, j: (0, j)),
            pl.BlockSpec((th, d), lambda i, j: (j, 0)),
            pl.BlockSpec((1, d), lambda i, j: (0, 0)),
        ],
        out_specs=pl.BlockSpec((tm, d), lambda i, j: (i, 0)),
        out_shape=jax.ShapeDtypeStruct((m, d), F32),
        scratch_shapes=[pltpu.VMEM((tm, d), F32), pltpu.VMEM((tm, d), BF16), pltpu.VMEM((tm, d), F32)],
        compiler_params=_cparams(("parallel", "arbitrary")),
        name="out_proj_swiglu_ffn",
    )(a, b, wa, wb, x, g, wg, wu, wd, gf)


def _hgrn2_kernel(q_ref, f_ref, i_ref, g_ref, lb_ref, gain_ref, o_ref, s_ref):
    T, SUB = HGRN_TILE, HGRN_SUB
    nb = T // SUB

    @pl.when(pl.program_id(1) == 0)
    def _():
        s_ref[...] = jnp.zeros_like(s_ref)

    r = _iota((T, T), 0)
    c = _iota((T, T), 1)
    tri = jnp.where((r >= c) & ((r // SUB) == (c // SUB)), 1.0, 0.0).astype(BF16)
    HALF = SUB // 2
    rowh = _iota((nb, HALF, A_KEY), 1)
    heads = range(A_HEADS)
    sls = [slice(h * A_KEY, (h + 1) * A_KEY) for h in heads]
    intra, qts, kts, v3s, decs = [], [], [], [], []

    for h in heads:
        sl = sls[h]
        q = _silu(q_ref[:, sl].astype(F32))
        lb = lb_ref[:, sl]
        f = lb + (1.0 - lb) * _sigmoid(f_ref[:, sl].astype(F32))
        lf = jnp.log(f)
        k = 1.0 - f
        v = i_ref[:, sl].astype(F32)
        b = _mm_sel(tri, lf)
        b3 = b.reshape(nb, SUB, A_KEY)
        q3 = q.reshape(nb, SUB, A_KEY)
        k3 = k.reshape(nb, SUB, A_KEY)
        v3 = v.reshape(nb, SUB, A_KEY)
        bk3 = b3 - jnp.log(k3)
        b_up, b_lo = b3[:, :HALF], b3[:, HALF:]
        q_up, q_lo = q3[:, :HALF], q3[:, HALF:]
        acc_up = jnp.zeros((nb, HALF, A_KEY), F32)
        acc_lo = jnp.zeros((nb, HALF, A_KEY), F32)
        for s in range(HALF):
            bs = bk3[:, s:s + 1, :]
            vs = v3[:, s:s + 1, :]
            e_up = jnp.exp(jnp.where(rowh >= s, b_up - bs, -jnp.inf))
            acc_up = acc_up + jnp.sum(q_up * e_up, axis=-1, keepdims=True) * vs
            acc_lo = acc_lo + jnp.sum(q_lo * jnp.exp(b_lo - bs), axis=-1, keepdims=True) * vs
        for s in range(HALF):
            bs = bk3[:, HALF + s:HALF + s + 1, :]
            e_lo = jnp.exp(jnp.where(rowh >= s, b_lo - bs, -jnp.inf))
            acc_lo = acc_lo + jnp.sum(q_lo * e_lo, axis=-1, keepdims=True) * v3[:, HALF + s:HALF + s + 1, :]
        intra.append(jnp.concatenate([acc_up, acc_lo], axis=1).reshape(T, A_KEY))

        bend = b3[:, SUB - 1:SUB, :]
        qts.append((q3 * jnp.exp(b3)).astype(BF16))
        kts.append((k3 * jnp.exp(bend - b3)).astype(BF16))
        v3s.append(v3.astype(BF16))
        decs.append(jnp.exp(bend))

    inc = [[_dg(v3s[h][i], kts[h][i], _TN) for i in range(nb)] for h in heads]
    for h in heads:
        st = s_ref[h]
        outs = []
        for i in range(nb):
            outs.append(_dg(qts[h][i], st.astype(BF16), _NT))
            st = st * decs[h][i] + inc[h][i]
        s_ref[h] = st
        o = intra[h] + jnp.concatenate(outs, axis=0)
        o = o * lax.rsqrt(jnp.mean(o * o, axis=-1, keepdims=True) + RMS_EPS) * gain_ref[:, sls[h]]
        o_ref[:, sls[h]] = (o * _silu(g_ref[:, sls[h]].astype(F32))).astype(o_ref.dtype)


def _hgrn2(u, lb, gain, *, batch, seq):
    T = HGRN_TILE
    nt = seq // T
    col = lambda j: pl.BlockSpec((T, A_WIDTH), lambda b, c, j=j: (b * nt + c, j))
    vec = pl.BlockSpec((1, A_WIDTH), lambda b, c: (0, 0))
    return pl.pallas_call(
        _hgrn2_kernel,
        grid=(batch, nt),
        in_specs=[col(0), col(1), col(2), col(3), vec, vec],
        out_specs=pl.BlockSpec((T, A_WIDTH), lambda b, c: (b * nt + c, 0)),
        out_shape=jax.ShapeDtypeStruct((batch * seq, A_WIDTH), BF16),
        scratch_shapes=[pltpu.VMEM((A_HEADS, A_KEY, A_KEY), F32)],
        compiler_params=_cparams(("parallel", "arbitrary")),
        name="hgrn2_mix",
    )(u, u, u, u, lb, gain)


def _rwkv7_kernel(r_ref, k_ref, v_ref, t_ref, mu_r, mu_k, mu_v, mu_t, w0_ref, w2_ref, a0_ref,
                  a2_ref, g2_ref, kk_ref, ka_ref, rk_ref, lnw_ref, lnb_ref, o_ref,
                  s_ref, cr_ref, ck_ref, cv_ref, ct_ref):
    L = RWKV_CHUNK
    T = RWKV_TILE
    N = B_HEAD
    nc = T // L

    @pl.when(pl.program_id(1) == 0)
    def _():
        s_ref[...] = jnp.zeros_like(s_ref)
        cr_ref[...] = jnp.zeros_like(cr_ref)
        ck_ref[...] = jnp.zeros_like(ck_ref)
        cv_ref[...] = jnp.zeros_like(cv_ref)
        ct_ref[...] = jnp.zeros_like(ct_ref)

    def shift_mix(x_ref, carry_ref, mu_ref):
        x = x_ref[...].astype(F32)
        row = _iota(x.shape, 0)
        prev = jnp.where(row == 0, carry_ref[7:8, :], pltpu.roll(x, 1, 0))
        carry_ref[...] = x[T - 8:T, :]
        return x + (prev - x) * mu_ref[...]

    r = shift_mix(r_ref, cr_ref, mu_r)
    k = shift_mix(k_ref, ck_ref, mu_k)
    v = shift_mix(v_ref, cv_ref, mu_v)
    tl = shift_mix(t_ref, ct_ref, mu_t)

    wl = _mm1(jnp.tanh(tl), w2_ref[...])
    al = _mm1(tl, a2_ref[...])
    gate = _mm1(_sigmoid(tl), g2_ref[...])
    w_log = -_softplus(-(w0_ref[...] + wl)) - 0.5
    lw = -jnp.exp(w_log)
    a_lr = _sigmoid(a0_ref[...] + al)

    ones_bd = jnp.where(_iota((B_WIDTH, B_WIDTH), 0) // N == _iota((B_WIDTH, B_WIDTH), 1) // N,
                        1.0, 0.0).astype(BF16)
    kk = k * kk_ref[...]
    k2 = k * (1.0 + (a_lr - 1.0) * ka_ref[...])
    sums = _x_sel(jnp.concatenate([kk * kk, r * k2 * rk_ref[...]], axis=0), ones_bd)
    kk = kk / jnp.maximum(jnp.sqrt(sums[:T]), 1e-12)
    bonus = sums[T:]
    aa = -kk
    bb = kk * a_lr

    rr = _iota((T, T), 0)
    cc = _iota((T, T), 1)
    tri = jnp.where((rr >= cc) & (rr // L == cc // L), 1.0, 0.0).astype(BF16)
    c = _mm_sel(tri, lw)
    enc = jnp.exp(-c)
    rt = (r * jnp.exp(c)).astype(BF16)
    at = (aa * jnp.exp(c - lw)).astype(BF16)
    bt = (bb * enc).astype(BF16)
    kt = (k2 * enc).astype(BF16)
    vb = v.astype(BF16)

    r2 = _iota((2 * L, L), 0)
    c2 = _iota((2 * L, L), 1)
    both = ((r2 < L) & (r2 > c2)) | (r2 - L >= c2)

    heads = range(B_HEADS)
    cols = lambda h: slice(h * N, (h + 1) * N)
    solved = {}
    state = {"s": [s_ref[h] for h in heads], "y": []}

    def solve_stages(ci):
        rs = slice(ci * L, (ci + 1) * L)
        ar = [jnp.concatenate([at[rs, cols(h)], rt[rs, cols(h)]], axis=0) for h in heads]
        vh = [vb[rs, cols(h)] for h in heads]
        m_b = [jnp.where(both, _dg(ar[h], bt[rs, cols(h)], _NT), 0.0) for h in heads]
        yield
        m_k = [jnp.where(both, _dg(ar[h], kt[rs, cols(h)], _NT), 0.0).astype(BF16) for h in heads]
        yield
        mkv = [_dg(m_k[h], vh[h], _NN) for h in heads]
        yield
        pw = [m_b[h][:L] for h in heads]
        x = [jnp.concatenate([ar[h][:L].astype(F32), mkv[h][:L]], axis=-1) for h in heads]
        x = [x[h] + _mm1(pw[h], x[h]) for h in heads]
        yield
        for _ in range(int(math.log2(L)) - 1):
            pw = [_mm1(pw[h], pw[h]) for h in heads]
            yield
            x = [x[h] + _mm1(pw[h], x[h]) for h in heads]
            yield
        c_last = c[(ci + 1) * L - 1:(ci + 1) * L, :]
        e_tail = jnp.exp(c_last - c[rs, :])
        solved[ci] = dict(
            wr=[jnp.concatenate([x[h][:, :N].astype(BF16), ar[h][L:]], axis=0) for h in heads],
            u0=[x[h][:, N:] for h in heads],
            a_rb=[m_b[h][L:].astype(BF16) for h in heads],
            y0=[mkv[h][L:] for h in heads], vh=vh, g_last=jnp.exp(c_last),
            bkl=jnp.concatenate([(bb[rs, :] * e_tail).astype(BF16),
                                 (k2[rs, :] * e_tail).astype(BF16)], axis=0))
        yield

    def state_stages(ci):
        q = solved.pop(ci)
        st = state["s"]
        ws = [_dg(q["wr"][h], st[h].astype(BF16), _NT) for h in heads]
        yield
        ub = [(ws[h][:L] + q["u0"][h]).astype(BF16) for h in heads]
        state["y"].append(jnp.concatenate(
            [ws[h][L:] + _dg(q["a_rb"][h], ub[h], _NN) + q["y0"][h] for h in heads], axis=-1))
        state["s"] = [st[h] * q["g_last"][:, cols(h)]
                      + _dg(jnp.concatenate([ub[h], q["vh"][h]], axis=0), q["bkl"][:, cols(h)], _TN)
                      for h in heads]
        yield

    for _ in zip(*[solve_stages(ci) for ci in range(nc)]):
        pass
    for ci in range(nc):
        for _ in state_stages(ci):
            pass
    for h in heads:
        s_ref[h] = state["s"][h]
    ys = state["y"]

    y = jnp.concatenate(ys, axis=0)
    mom =_x_sel(jnp.concatenate([y, y * y], axis=0), ones_bd) * (1.0 / N)
    mean = mom[:T]
    var = jnp.maximum(mom[T:] - mean * mean, 0.0)
    y = (y - mean) * lax.rsqrt(var + GN_EPS) * lnw_ref[...] + lnb_ref[...]
    y = y + bonus * v
    o_ref[...] = (y * gate).astype(o_ref.dtype)


def _rwkv7(u, p, *, batch, seq):
    T = RWKV_TILE
    nt = seq // T
    col = lambda j: pl.BlockSpec((T, B_WIDTH), lambda b, c, j=j: (b * nt + c, j))
    tail = pl.BlockSpec((T, B_TAIL), lambda b, c: (b * nt + c, (4 * A_WIDTH + 3 * B_WIDTH) // B_TAIL))
    full = lambda a: pl.BlockSpec(a.shape, lambda b, c: (0,) * a.ndim)
    params = [p["mu_r"], p["mu_k"], p["mu_v"], p["mu_t"], p["w0"], p["w2"], p["a0"], p["a2"],
              p["g2"], p["kk"], p["ka"], p["rk"], p["lnw"], p["lnb"]]
    return pl.pallas_call(
        _rwkv7_kernel,
        grid=(batch, nt),
        in_specs=[col(4), col(5), col(6), tail] + [full(a) for a in params],
        out_specs=pl.BlockSpec((T, B_WIDTH), lambda b, c: (b * nt + c, 0)),
        out_shape=jax.ShapeDtypeStruct((batch * seq, B_WIDTH), BF16),
        scratch_shapes=[pltpu.VMEM((B_HEADS, B_HEAD, B_HEAD), F32),
                        pltpu.VMEM((8, B_WIDTH), F32), pltpu.VMEM((8, B_WIDTH), F32),
                        pltpu.VMEM((8, B_WIDTH), F32), pltpu.VMEM((8, B_TAIL), F32)],
        compiler_params=_cparams(("parallel", "arbitrary")),
        name="rwkv7_mix",
    )(u, u, u, u, *params)


def _s5_inc_kernel(u_ref, w_ref, z_ref):
    z_ref[...] = jnp.dot(u_ref[...], w_ref[...], preferred_element_type=F32)


def _s5_increments(u2, w, *, tm, tn):
    n, kdim = u2.shape
    ncol = w.shape[1]
    return pl.pallas_call(
        _s5_inc_kernel,
        grid=(n // tm, ncol // tn),
        in_specs=[pl.BlockSpec((tm, kdim), lambda i, j: (i, 0)),
                  pl.BlockSpec((kdim, tn), lambda i, j: (0, j))],
        out_specs=pl.BlockSpec((tm, tn), lambda i, j: (i, j)),
        out_shape=jax.ShapeDtypeStruct((n, ncol), F32),
        compiler_params=_cparams(("parallel", "parallel")),
        name="s5_increments",
    )(u2, w)


def _s5_carry_kernel(z_ref, lr_ref, li_ref, hp_ref, h_ref):
    n = z_ref.shape[0]
    half = z_ref.shape[1] // 2
    lr = lr_ref[...]
    li = li_ref[...]

    def body(i, carry):
        hr, hi = carry
        rows = pl.ds(pl.multiple_of(i * 8, 8), 8)
        zz = z_ref[rows, :]
        outs = []
        for j in range(8):
            outs.append(jnp.concatenate([hr, hi], axis=-1))
            hr, hi = (lr * hr - li * hi + zz[j:j + 1, :half],
                      lr * hi + li * hr + zz[j:j + 1, half:])
        h_ref[rows, :] = jnp.concatenate(outs, axis=0)
        return hr, hi

    zero = jnp.zeros((1, half), F32)
    lax.fori_loop(0, n // 8, body, (zero, zero))
    hp_ref[...] = h_ref[...].astype(hp_ref.dtype)


def _s5_carry(z, lr, li, *, batch):
    n, w = z.shape
    rows = n // batch
    vec = pl.BlockSpec((1, w // 2), lambda b: (0, 0))
    return pl.pallas_call(
        _s5_carry_kernel,
        grid=(batch,),
        in_specs=[pl.BlockSpec((rows, w), lambda b: (b, 0)), vec, vec],
        out_specs=pl.BlockSpec((rows, w), lambda b: (b, 0)),
        out_shape=jax.ShapeDtypeStruct((n, w), BF16),
        scratch_shapes=[pltpu.VMEM((rows, w), F32)],
        compiler_params=_cparams(("parallel",)),
        name="s5_carry",
    )(z, lr, li)


def _s5_out_kernel(u_ref, uj_ref, hp_ref, t_ref, m_ref, d_ref, gw_ref, gb_ref, o_ref):
    y = jnp.dot(u_ref[...], t_ref[...], preferred_element_type=F32)
    y = y + jnp.dot(hp_ref[...], m_ref[...], preferred_element_type=F32)
    for s in range(o_ref.shape[1] // C_WIDTH):
        cs = slice(s * C_WIDTH, (s + 1) * C_WIDTH)
        z = _gelu_tanh(y[:, cs] + d_ref[...] * uj_ref[:, cs].astype(F32))
        gate = jnp.dot(z.astype(BF16), gw_ref[...], preferred_element_type=F32) + gb_ref[...]
        o_ref[:, cs] = (z * _sigmoid(gate)).astype(o_ref.dtype)


def _s5_output(u2, hp, t, mmat, d, gw, gb, *, tm, tn):
    n, kdim = u2.shape
    kst = hp.shape[1]
    vec = pl.BlockSpec((1, C_WIDTH), lambda i, j: (0, 0))
    return pl.pallas_call(
        _s5_out_kernel,
        grid=(n // tm, kdim // tn),
        in_specs=[pl.BlockSpec((tm, kdim), lambda i, j: (i, 0)),
                  pl.BlockSpec((tm, tn), lambda i, j: (i, j)),
                  pl.BlockSpec((tm, kst), lambda i, j: (i, 0)),
                  pl.BlockSpec((kdim, tn), lambda i, j: (0, j)),
                  pl.BlockSpec((kst, tn), lambda i, j: (0, j)),
                  vec, pl.BlockSpec((C_WIDTH, C_WIDTH), lambda i, j: (0, 0)), vec],
        out_specs=pl.BlockSpec((tm, tn), lambda i, j: (i, j)),
        out_shape=jax.ShapeDtypeStruct((n, kdim), BF16),
        compiler_params=_cparams(("parallel", "parallel")),
        name="s5_output",
    )(u2, u2, hp, t, mmat, d, gw, gb)


def _s5_params(lam_re, lam_im, log_step, b_re, b_im, c_re, c_im):
    L = S5_CHUNK
    step = jnp.exp(log_step)[:, None]
    mag = jnp.exp(lam_re * step)
    lr, li = mag * jnp.cos(lam_im * step), mag * jnp.sin(lam_im * step)
    den = lam_re * lam_re + lam_im * lam_im
    nr, ni = lr - 1.0, li
    coef_re = (nr * lam_re + ni * lam_im) / den
    coef_im = (ni * lam_re - nr * lam_im) / den
    bb_re = coef_re[..., None] * b_re - coef_im[..., None] * b_im
    bb_im = coef_re[..., None] * b_im + coef_im[..., None] * b_re
    pr, pi = [jnp.ones_like(lr)], [jnp.zeros_like(lr)]
    for _ in range(L):
        pr.append(pr[-1] * lr - pi[-1] * li)
        pi.append(pr[-2] * li + pi[-1] * lr)
    pr, pi = jnp.stack(pr), jnp.stack(pi)
    hp = lax.Precision.HIGHEST
    zr = pr[:L, :, :, None] * bb_re[None] - pi[:L, :, :, None] * bb_im[None]
    zi = pr[:L, :, :, None] * bb_im[None] + pi[:L, :, :, None] * bb_re[None]
    kern = (jnp.einsum("gop,lgpi->lgoi", c_re, zr, precision=hp)
            - jnp.einsum("gop,lgpi->lgoi", c_im, zi, precision=hp))
    nin = L * C_WIDTH
    nst = 2 * C_GROUPS * C_STATE

    def spread(a, row_group, n_inner):
        rows, cols = a.shape
        wide = cols * C_GROUPS
        cw = jnp.arange(wide)
        src = (cw // (C_GROUPS * n_inner)) * n_inner + cw % n_inner
        rep = (jnp.arange(cols)[:, None] == src[None, :]).astype(BF16)
        out = jnp.dot(a.astype(BF16), rep, preferred_element_type=F32)
        keep = row_group[:, None] == ((cw // n_inner) % C_GROUPS)[None, :]
        return jnp.where(keep, out, 0.0).astype(BF16)

    ka = kern.transpose(0, 1, 3, 2).reshape(L * C_WIDTH, C_GROUP)
    bd = spread(ka, (jnp.arange(L * C_WIDTH) // C_GROUP) % C_GROUPS, C_GROUP).reshape(L, C_WIDTH, C_WIDTH)
    s_idx = jnp.arange(L)[:, None]
    t_idx = jnp.arange(L)[None, :]
    lag = jnp.clip(t_idx - s_idx, 0, L - 1)
    tmat = jnp.where((t_idx >= s_idx)[:, :, None, None], bd[lag], jnp.zeros((), BF16))
    tmat = tmat.transpose(0, 2, 1, 3).reshape(nin, nin)
    wz = jnp.stack([zr[::-1], zi[::-1]])
    wa = wz.transpose(1, 2, 4, 0, 3).reshape(nin, 2 * C_STATE)
    w = spread(wa, (jnp.arange(nin) // C_GROUP) % C_GROUPS, C_STATE)
    qr, qi = pr[1:], pi[1:]
    m_re = c_re[None] * qr[:, :, None, :] - c_im[None] * qi[:, :, None, :]
    m_im = -(c_re[None] * qi[:, :, None, :] + c_im[None] * qr[:, :, None, :])
    ma = jnp.stack([m_re, m_im]).transpose(0, 2, 4, 1, 3).reshape(nst, L * C_GROUP)
    mmat = spread(ma, (jnp.arange(nst) // C_STATE) % C_GROUPS, C_GROUP)
    return {"w": w, "t": tmat, "m": mmat,
            "lr": pr[L].reshape(1, -1), "li": pi[L].reshape(1, -1)}


def _mlstm_kernel(q_ref, k_ref, v_ref, o_ref, gt_ref, cq_ref, ck_ref, gb_ref, gain_ref, y_ref,
                  c_ref, n_ref, m_ref, pq_ref, pk_ref):
    L = MLSTM_CHUNK
    P = D_HEAD_PAD

    @pl.when(pl.program_id(1) == 0)
    def _():
        c_ref[...] = jnp.zeros_like(c_ref)
        n_ref[...] = jnp.zeros_like(n_ref)
        m_ref[...] = jnp.full(m_ref.shape, M_INIT, F32)
        pq_ref[...] = jnp.zeros_like(pq_ref)
        pk_ref[...] = jnp.zeros_like(pk_ref)

    T = MLSTM_TILE
    nc = T // L
    PREV = pq_ref.shape[0]
    tap_row = _iota((D_CONV * L, L), 0)
    src = tap_row % L + tap_row // L - (D_CONV - 1)
    sel_x = jnp.where(_iota((D_CONV * L, L), 1) == src, 1.0, 0.0).astype(BF16)
    tap_row_p = _iota((D_CONV * L, PREV), 0)
    src_p = tap_row_p % L + tap_row_p // L - (D_CONV - 1) + PREV
    sel_p = jnp.where(_iota((D_CONV * L, PREV), 1) == src_p, 1.0, 0.0).astype(BF16)

    def conv(x_ref, prev_ref, w_ref):
        x = x_ref[...]
        outs = []
        for ci in range(nc):
            prev = prev_ref[...] if ci == 0 else x[ci * L - PREV:ci * L, :]
            taps = _dg(sel_x, x[ci * L:(ci + 1) * L, :], _NN) + _dg(sel_p, prev, _NN)
            acc = taps[:L, :] * w_ref[0:1, :]
            for j in range(1, D_CONV):
                acc = acc + taps[j * L:(j + 1) * L, :] * w_ref[j:j + 1, :]
            outs.append(acc)
        prev_ref[...] = x[T - PREV:T, :]
        return _silu(jnp.concatenate(outs, axis=0))

    q = conv(q_ref, pq_ref, cq_ref)
    k = conv(k_ref, pk_ref, ck_ref) * (1.0 / math.sqrt(D_HEAD))
    vb = v_ref[...]

    gates = gt_ref[...].astype(F32) + gb_ref[...]
    col = _iota(gates.shape, 1)
    is_f = (col >= D_HEADS) & (col < 2 * D_HEADS)
    gates = jnp.where(is_f, jnp.minimum(gates, 0.0) - jnp.log(1.0 + jnp.exp(-jnp.abs(gates))), gates)
    rr = _iota((T, T), 0)
    cc = _iota((T, T), 1)
    tri = jnp.where((rr >= cc) & (rr // L == cc // L), 1.0, 0.0).astype(BF16)
    bcum = _mm_sel(tri, gates)
    gates_t = gates.T
    bcum_t = bcum.T
    lower = _tril_mask(L)

    heads = range(D_HEADS)
    sls = [slice(h * P, (h + 1) * P) for h in heads]
    qb, kb = q.astype(BF16), k.astype(BF16)
    rows = [slice(ci * L, (ci + 1) * L) for ci in range(nc)]
    qk0 = [[_dg(qb[rs, sls[h]], kb[rs, sls[h]], _NT) for h in heads] for rs in rows]
    c_st = [c_ref[h] for h in heads]
    n_st = [n_ref[h] for h in heads]
    m_st = [m_ref[h] for h in heads]

    for ci, rs in enumerate(rows):
        qc = [_dg(qb[rs, sls[h]], c_st[h].astype(BF16), _NN) for h in heads]
        gate = []
        for h in heads:
            b_col = bcum[rs, D_HEADS + h:D_HEADS + h + 1]
            b_row = bcum_t[D_HEADS + h:D_HEADS + h + 1, rs]
            i_col = gates[rs, h:h + 1]
            i_row = gates_t[h:h + 1, rs]
            dmat = jnp.where(lower, b_col - b_row + i_row, -jnp.inf)
            e_inter = b_col + m_st[h]
            m_t = jnp.maximum(e_inter, jnp.max(dmat, axis=-1, keepdims=True))
            m_new = m_t[L - 1:L, :]
            b_last = b_col[L - 1:L, :]
            gate.append(dict(
                m_t=m_t, m_new=m_new, w_intra=jnp.exp(dmat - m_t), s_inter=jnp.exp(e_inter - m_t),
                w_last=jnp.exp(b_last - b_col + i_col - m_new),
                d_prev=jnp.exp(b_last + m_st[h] - m_new)))

        qk = [qk0[ci][h] * gate[h]["w_intra"] for h in heads]
        kw = [k[rs, sls[h]] * gate[h]["w_last"] for h in heads]
        qkv = [_dg(qk[h].astype(BF16), vb[rs, sls[h]], _NN) for h in heads]
        cu = [_dg(kw[h].astype(BF16), vb[rs, sls[h]], _TN) for h in heads]

        for h in heads:
            g, sl = gate[h], sls[h]
            num = g["s_inter"] * qc[h] + qkv[h]
            den = (g["s_inter"] * jnp.sum(q[rs, sl] * n_st[h], axis=-1, keepdims=True)
                   + jnp.sum(qk[h], axis=-1, keepdims=True))
            hh = num / jnp.maximum(jnp.abs(den), jnp.exp(-g["m_t"]))
            c_st[h] = g["d_prev"] * c_st[h] + cu[h]
            n_st[h] = g["d_prev"] * n_st[h] + jnp.sum(kw[h], axis=0, keepdims=True)
            m_st[h] = g["m_new"]

            ms = jnp.sum(hh * hh, axis=-1, keepdims=True) * (1.0 / D_HEAD)
            out = hh * lax.rsqrt(ms + RMS_EPS) * gain_ref[:, sl] * _sigmoid(o_ref[rs, sl].astype(F32))
            y_ref[rs, sl] = out.astype(y_ref.dtype)

    for h in heads:
        c_ref[h] = c_st[h]
        n_ref[h] = n_st[h]
        m_ref[h] = m_st[h]


def _mlstm(u, conv_q, conv_k, gate_bias, gain, *, batch, seq):
    L = MLSTM_TILE
    nt = seq // L
    W = D_WIDTH_PAD
    col = lambda j: pl.BlockSpec((L, W), lambda b, c, j=j: (b * nt + c, j))
    gcol = pl.BlockSpec((L, D_GATE_PAD), lambda b, c: (b * nt + c, (4 * W + C_WIDTH) // D_GATE_PAD))
    full = lambda a: pl.BlockSpec(a.shape, lambda b, c: (0,) * a.ndim)
    return pl.pallas_call(
        _mlstm_kernel,
        grid=(batch, nt),
        in_specs=[col(0), col(1), col(2), col(3), gcol, full(conv_q), full(conv_k),
                  full(gate_bias), full(gain)],
        out_specs=pl.BlockSpec((L, W), lambda b, c: (b * nt + c, 0)),
        out_shape=jax.ShapeDtypeStruct((batch * seq, W), BF16),
        scratch_shapes=[pltpu.VMEM((D_HEADS, D_HEAD_PAD, D_HEAD_PAD), F32),
                        pltpu.VMEM((D_HEADS, 1, D_HEAD_PAD), F32),
                        pltpu.VMEM((D_HEADS, 1, 1), F32),
                        pltpu.VMEM((16, W), BF16), pltpu.VMEM((16, W), BF16)],
        compiler_params=_cparams(("parallel", "arbitrary")),
        name="mlstm_mix",
    )(u, u, u, u, u, conv_q, conv_k, gate_bias, gain)


def _pad_cols(a, n):
    return jnp.pad(a, [(0, 0)] * (a.ndim - 1) + [(0, n - a.shape[-1])])


def _pad_heads(a, axis):
    shp = a.shape
    a = a.reshape(shp[:axis] + (D_HEADS, D_HEAD) + shp[axis + 1:])
    pad = [(0, 0)] * a.ndim
    pad[axis + 1] = (0, D_HEAD_PAD - D_HEAD)
    a = jnp.pad(a, pad)
    return a.reshape(shp[:axis] + (D_WIDTH_PAD,) + shp[axis + 1:])


def _row(v):
    return v.reshape(1, -1).astype(F32)


def kernel(x, norm_mix, norm_ffn, norm_final, w_in_even, w_out_even, lb_table, a_norm, b_mu, b_w0, b_w2, b_a0, b_a2, b_g2, b_kk, b_ka, b_rk, b_ln_w, b_ln_b, w_in_odd, w_out_odd, c_lam_re, c_lam_im, c_log_step, c_b_re, c_b_im, c_c_re, c_c_im, c_d, c_glu_w, c_glu_b, d_conv_q, d_conv_k, d_i_bias, d_f_bias, d_norm, ffn_gate, ffn_up, ffn_down):
    batch, seq, d = x.shape
    m = batch * seq
    xf = x.reshape(m, d).astype(F32)
    lower_bounds = jnp.cumsum(jax.nn.softmax(lb_table.astype(F32), axis=0), axis=0)

    w_in = _pad_cols(w_in_even[0], EVEN_IN_PAD).astype(BF16)
    u = _norm_matmul(xf, _row(norm_mix[0]), w_in, tm=512, tn=768)
    ya = _hgrn2(u, _row(lower_bounds[0]), _row(a_norm[0]), batch=batch, seq=seq)

    mu = b_mu[0].astype(F32)
    lora_rows = lambda w, off: jnp.pad(w.astype(F32), ((off, B_TAIL - off - w.shape[0]), (0, 0)))
    rw = {
        "mu_r": _row(mu[:B_WIDTH]), "mu_k": _row(mu[B_WIDTH:2 * B_WIDTH]),
        "mu_v": _row(mu[2 * B_WIDTH:3 * B_WIDTH]), "mu_t": _row(_pad_cols(mu[3 * B_WIDTH:], B_TAIL)),
        "w0": _row(b_w0[0]), "w2": lora_rows(b_w2[0], 0),
        "a0": _row(b_a0[0]), "a2": lora_rows(b_a2[0], B_DECAY_LORA),
        "g2": lora_rows(b_g2[0], B_DECAY_LORA + B_AAA_LORA).astype(BF16),
        "kk": _row(b_kk[0]), "ka": _row(b_ka[0]), "rk": _row(b_rk[0]),
        "lnw": _row(b_ln_w[0]), "lnb": _row(b_ln_b[0]),
    }
    yb = _rwkv7(u, rw, batch=batch, seq=seq)
    wo = w_out_even[0].astype(BF16)
    xf = _out_proj_ffn(ya, yb, wo[:A_WIDTH], wo[A_WIDTH:], xf, _row(norm_ffn[0]),
                       ffn_gate[0].astype(BF16), ffn_up[0].astype(BF16), ffn_down[0].astype(BF16),
                       _row(norm_final), tm=512, th=1408, final_norm=False)

    wi = w_in_odd[0].astype(F32)
    seg = lambda j: _pad_heads(wi[:, C_WIDTH + j * D_WIDTH:C_WIDTH + (j + 1) * D_WIDTH], 1)
    w_in = jnp.concatenate(
        [seg(0), seg(1), seg(2), seg(3), wi[:, :C_WIDTH],
         _pad_cols(wi[:, C_WIDTH + 4 * D_WIDTH:], D_GATE_PAD)], axis=1).astype(BF16)
    u = _norm_matmul(xf, _row(norm_mix[1]), w_in, tm=512, tn=640)

    u2 = u[:, 4 * D_WIDTH_PAD:4 * D_WIDTH_PAD + C_WIDTH].reshape(m // S5_CHUNK, S5_CHUNK * C_WIDTH)
    s5p = _s5_params(c_lam_re[0].astype(F32), c_lam_im[0].astype(F32), c_log_step[0].astype(F32),
                     c_b_re[0].astype(F32), c_b_im[0].astype(F32), c_c_re[0].astype(F32),
                     c_c_im[0].astype(F32))
    tr = min(1024, m // S5_CHUNK)
    z = _s5_increments(u2, s5p["w"], tm=tr, tn=512)
    hp = _s5_carry(z, s5p["lr"], s5p["li"], batch=batch)
    yc = _s5_output(u2, hp, s5p["t"], s5p["m"], _row(c_d[0]), c_glu_w[0].astype(BF16),
                    _row(c_glu_b[0]), tm=tr, tn=512).reshape(m, C_WIDTH)

    gate_bias = _row(_pad_cols(jnp.concatenate([d_i_bias[0], d_f_bias[0]]).astype(F32), D_GATE_PAD))
    yd = _mlstm(u, _pad_heads(d_conv_q[0].astype(F32), 1), _pad_heads(d_conv_k[0].astype(F32), 1),
                gate_bias, _row(_pad_heads(d_norm[0].astype(F32), 0)), batch=batch, seq=seq)
    wo = w_out_odd[0].astype(F32)
    xf = _out_proj_ffn(yc, yd, wo[:C_WIDTH].astype(BF16), _pad_heads(wo[C_WIDTH:], 0).astype(BF16), xf,
                       _row(norm_ffn[1]), ffn_gate[1].astype(BF16), ffn_up[1].astype(BF16),
                       ffn_down[1].astype(BF16), _row(norm_final), tm=512, th=1408, final_norm=True)
    return xf.reshape(batch, seq, d).astype(x.dtype)
```

```python
import functools
import math

import jax
import jax.numpy as jnp
from jax import lax
from jax.experimental import pallas as pl
from jax.experimental.pallas import tpu as pltpu

F32 = jnp.float32
BF16 = jnp.bfloat16

RMS_EPS = 1e-6
GN_EPS = 64e-5
M_INIT = -1e30

D_MODEL = 1024
A_WIDTH, A_HEADS, A_KEY = 512, 4, 128
B_WIDTH, B_HEADS, B_HEAD = 512, 8, 64
B_DECAY_LORA, B_AAA_LORA, B_GATE_LORA = 32, 32, 96
B_TAIL = 256
EVEN_IN_PAD = 4 * A_WIDTH + 3 * B_WIDTH + B_TAIL
C_WIDTH, C_GROUP, C_GROUPS, C_STATE = 256, 16, 16, 64
D_HEADS, D_HEAD, D_HEAD_PAD = 4, 192, 256
D_WIDTH = D_HEADS * D_HEAD
D_WIDTH_PAD = D_HEADS * D_HEAD_PAD
D_CONV = 4
D_GATE_PAD = 128
ODD_IN_PAD = 4 * D_WIDTH_PAD + C_WIDTH + D_GATE_PAD
FFN_HIDDEN = 2816

HGRN_TILE = 128
HGRN_SUB = 16
RWKV_CHUNK = 64
RWKV_TILE = 256
S5_CHUNK = 16
MLSTM_CHUNK = 64
MLSTM_TILE = 256

VMEM_LIMIT = 56 * 1024 * 1024


def _cparams(sem):
    return pltpu.CompilerParams(dimension_semantics=sem, vmem_limit_bytes=VMEM_LIMIT)


def _split2(x):
    hi = x.astype(BF16)
    lo = (x - hi.astype(F32)).astype(BF16)
    return hi, lo


def _split3(x):
    hi = x.astype(BF16)
    r1 = x - hi.astype(F32)
    mid = r1.astype(BF16)
    lo = (r1 - mid.astype(F32)).astype(BF16)
    return hi, mid, lo


def _dg(a, b, dims):
    return lax.dot_general(a, b, (dims, ((), ())), preferred_element_type=F32)


_NN = ((1,), (0,))
_NT = ((1,), (1,))
_TN = ((0,), (0,))


def _mm1(a, b, dims=_NN):
    return _dg(a.astype(BF16), b.astype(BF16), dims)


def _mm3(a, b, dims=_NN):
    ah, al = _split2(a)
    bh, bl = _split2(b)
    return _dg(ah, bh, dims) + (_dg(ah, bl, dims) + _dg(al, bh, dims))


def _mm2(a, b, dims=_NN):
    ah, al = _split2(a)
    bh = b.astype(BF16)
    return _dg(ah, bh, dims) + _dg(al, bh, dims)


_mmi = _mm1


def _mm_sel(sel_bf16, x, dims=_NN):
    h, m, l = _split3(x)
    return _dg(sel_bf16, h, dims) + (_dg(sel_bf16, m, dims) + _dg(sel_bf16, l, dims))


def _x_sel(x, sel_bf16):
    n = x.shape[0]
    r = _dg(jnp.concatenate(_split3(x), axis=0), sel_bf16, _NN)
    return r[:n] + (r[n:2 * n] + r[2 * n:])


def _sigmoid(x):
    return 1.0 / (1.0 + jnp.exp(-x))


def _silu(x):
    return x * _sigmoid(x)


def _softplus(x):
    return jnp.maximum(x, 0.0) + jnp.log(1.0 + jnp.exp(-jnp.abs(x)))


def _gelu_tanh(x):
    c = math.sqrt(2.0 / math.pi)
    return 0.5 * x * (1.0 + jnp.tanh(c * (x + 0.044715 * (x * x * x))))


def _iota(shape, dim):
    return lax.broadcasted_iota(jnp.int32, shape, dim)


def _tril_mask(n, strict=False):
    r = _iota((n, n), 0)
    c = _iota((n, n), 1)
    return (r > c) if strict else (r >= c)


def _norm_matmul_kernel(x_ref, g_ref, w_ref, o_ref, *, tn):
    x = x_ref[...]
    ms = jnp.mean(x * x, axis=-1, keepdims=True)
    h = (x * lax.rsqrt(ms + RMS_EPS) * g_ref[...]).astype(BF16)
    for j in range(o_ref.shape[1] // tn):
        cs = slice(j * tn, (j + 1) * tn)
        o_ref[:, cs] = jnp.dot(h, w_ref[:, cs], preferred_element_type=F32).astype(o_ref.dtype)


def _norm_matmul(x, g, w, *, tm, tn):
    m, d = x.shape
    n = w.shape[1]
    return pl.pallas_call(
        functools.partial(_norm_matmul_kernel, tn=tn),
        grid=(m // tm,),
        in_specs=[
            pl.BlockSpec((tm, d), lambda i: (i, 0)),
            pl.BlockSpec((1, d), lambda i: (0, 0)),
            pl.BlockSpec((d, n), lambda i: (0, 0)),
        ],
        out_specs=pl.BlockSpec((tm, n), lambda i: (i, 0)),
        out_shape=jax.ShapeDtypeStruct((m, n), BF16),
        compiler_params=_cparams(("parallel",)),
        name="norm_in_proj",
    )(x, g, w)


def _ffn_kernel(a_ref, b_ref, wa_ref, wb_ref, x_ref, g_ref, wg_ref, wu_ref, wd_ref, gf_ref, o_ref,
                xs_ref, h_ref, acc_ref, *, final_norm):
    j = pl.program_id(1)

    @pl.when(j == 0)
    def _():
        x = x_ref[...] + jnp.dot(a_ref[...], wa_ref[...], preferred_element_type=F32)
        x = x + jnp.dot(b_ref[...], wb_ref[...], preferred_element_type=F32)
        xs_ref[...] = x
        ms = jnp.mean(x * x, axis=-1, keepdims=True)
        h_ref[...] = (x * lax.rsqrt(ms + RMS_EPS) * g_ref[...]).astype(BF16)
        acc_ref[...] = jnp.zeros_like(acc_ref)

    h = h_ref[...]
    gate = jnp.dot(h, wg_ref[...], preferred_element_type=F32)
    up = jnp.dot(h, wu_ref[...], preferred_element_type=F32)
    act = (_silu(gate) * up).astype(BF16)
    acc_ref[...] += jnp.dot(act, wd_ref[...], preferred_element_type=F32)

    @pl.when(j == pl.num_programs(1) - 1)
    def _():
        y = xs_ref[...] + acc_ref[...]
        if final_norm:
            ms = jnp.mean(y * y, axis=-1, keepdims=True)
            y = y * lax.rsqrt(ms + RMS_EPS) * gf_ref[...]
        o_ref[...] = y


def _out_proj_ffn(a, b, wa, wb, x, g, wg, wu, wd, gf, *, tm, th, final_norm):
    m, d = x.shape
    hid = wg.shape[1]
    ka, kb = a.shape[1], b.shape[1]
    return pl.pallas_call(
        functools.partial(_ffn_kernel, final_norm=final_norm),
        grid=(m // tm, hid // th),
        in_specs=[
            pl.BlockSpec((tm, ka), lambda i, j: (i, 0)),
            pl.BlockSpec((tm, kb), lambda i, j: (i, 0)),
            pl.BlockSpec((ka, d), lambda i, j: (0, 0)),
            pl.BlockSpec((kb, d), lambda i, j: (0, 0)),
            pl.BlockSpec((tm, d), lambda i, j: (i, 0)),
            pl.BlockSpec((1, d), lambda i, j: (0, 0)),
            pl.BlockSpec((d, th), lambda i, j: (0, j)),
            pl.BlockSpec((d, th), lambda i, j: (0, j)),
            pl.BlockSpec((th, d), lambda i, j: (j, 0)),
            pl.BlockSpec((1, d), lambda i, j: (0, 0)),
        ],
        out_specs=pl.BlockSpec((tm, d), lambda i, j: (i, 0)),
        out_shape=jax.ShapeDtypeStruct((m, d), F32),
        scratch_shapes=[pltpu.VMEM((tm, d), F32), pltpu.VMEM((tm, d), BF16), pltpu.VMEM((tm, d), F32)],
        compiler_params=_cparams(("parallel", "arbitrary")),
        name="out_proj_swiglu_ffn",
    )(a, b, wa, wb, x, g, wg, wu, wd, gf)


def _hgrn2_kernel(q_ref, f_ref, i_ref, g_ref, lb_ref, gain_ref, o_ref, s_ref):
    T, SUB = HGRN_TILE, HGRN_SUB
    nb = T // SUB

    @pl.when(pl.program_id(1) == 0)
    def _():
        s_ref[...] = jnp.zeros_like(s_ref)

    r = _iota((T, T), 0)
    c = _iota((T, T), 1)
    tri = jnp.where((r >= c) & ((r // SUB) == (c // SUB)), 1.0, 0.0).astype(BF16)
    HALF = SUB // 2
    rowh = _iota((nb, HALF, A_KEY), 1)
    heads = range(A_HEADS)
    sls = [slice(h * A_KEY, (h + 1) * A_KEY) for h in heads]
    intra, qts, kts, v3s, decs = [], [], [], [], []

    for h in heads:
        sl = sls[h]
        q = _silu(q_ref[:, sl].astype(F32))
        lb = lb_ref[:, sl]
        f = lb + (1.0 - lb) * _sigmoid(f_ref[:, sl].astype(F32))
        lf = jnp.log(f)
        k = 1.0 - f
        v = i_ref[:, sl].astype(F32)
        b = _mm_sel(tri, lf)
        b3 = b.reshape(nb, SUB, A_KEY)
        q3 = q.reshape(nb, SUB, A_KEY)
        k3 = k.reshape(nb, SUB, A_KEY)
        v3 = v.reshape(nb, SUB, A_KEY)
        log2e = 1.0 / math.log(2.0)
        b2 = b3 * log2e
        bk3 = b2 - jnp.log2(k3)
        b_up, b_lo = b2[:, :HALF], b2[:, HALF:]
        q_up, q_lo = q3[:, :HALF], q3[:, HALF:]
        acc_up = jnp.zeros((nb, HALF, A_KEY), F32)
        acc_lo = jnp.zeros((nb, HALF, A_KEY), F32)
        for s in range(HALF):
            bs = bk3[:, s:s + 1, :]
            vs = v3[:, s:s + 1, :]
            e_up = jnp.exp2(jnp.where(rowh >= s, b_up - bs, -jnp.inf))
            acc_up = acc_up + jnp.sum(q_up * e_up, axis=-1, keepdims=True) * vs
            acc_lo = acc_lo + jnp.sum(q_lo * jnp.exp2(b_lo - bs), axis=-1, keepdims=True) * vs
        for s in range(HALF):
            bs = bk3[:, HALF + s:HALF + s + 1, :]
            e_lo = jnp.exp2(jnp.where(rowh >= s, b_lo - bs, -jnp.inf))
            acc_lo = acc_lo + jnp.sum(q_lo * e_lo, axis=-1, keepdims=True) * v3[:, HALF + s:HALF + s + 1, :]
        intra.append(jnp.concatenate([acc_up, acc_lo], axis=1).reshape(T, A_KEY))

        bend = b3[:, SUB - 1:SUB, :]
        qts.append((q3 * jnp.exp(b3)).astype(BF16))
        kts.append((k3 * jnp.exp(bend - b3)).astype(BF16))
        v3s.append(v3.astype(BF16))
        decs.append(jnp.exp(bend))

    inc = [[_dg(v3s[h][i], kts[h][i], _TN) for i in range(nb)] for h in heads]
    for h in heads:
        st = s_ref[h]
        outs = []
        for i in range(nb):
            outs.append(_dg(qts[h][i], st.astype(BF16), _NT))
            st = st * decs[h][i] + inc[h][i]
        s_ref[h] = st
        o = intra[h] + jnp.concatenate(outs, axis=0)
        o = o * lax.rsqrt(jnp.mean(o * o, axis=-1, keepdims=True) + RMS_EPS) * gain_ref[:, sls[h]]
        o_ref[:, sls[h]] = (o * _silu(g_ref[:, sls[h]].astype(F32))).astype(o_ref.dtype)


def _hgrn2(u, lb, gain, *, batch, seq):
    T = HGRN_TILE
    nt = seq // T
    col = lambda j: pl.BlockSpec((T, A_WIDTH), lambda b, c, j=j: (b * nt + c, j))
    vec = pl.BlockSpec((1, A_WIDTH), lambda b, c: (0, 0))
    return pl.pallas_call(
        _hgrn2_kernel,
        grid=(batch, nt),
        in_specs=[col(0), col(1), col(2), col(3), vec, vec],
        out_specs=pl.BlockSpec((T, A_WIDTH), lambda b, c: (b * nt + c, 0)),
        out_shape=jax.ShapeDtypeStruct((batch * seq, A_WIDTH), BF16),
        scratch_shapes=[pltpu.VMEM((A_HEADS, A_KEY, A_KEY), F32)],
        compiler_params=_cparams(("parallel", "arbitrary")),
        name="hgrn2_mix",
    )(u, u, u, u, lb, gain)


def _rwkv7_kernel(r_ref, k_ref, v_ref, t_ref, mu_r, mu_k, mu_v, mu_t, w0_ref, w2_ref, a0_ref,
                  a2_ref, g2_ref, kk_ref, ka_ref, rk_ref, lnw_ref, lnb_ref, o_ref,
                  s_ref, cr_ref, ck_ref, cv_ref, ct_ref):
    L = RWKV_CHUNK
    T = RWKV_TILE
    N = B_HEAD
    nc = T // L

    @pl.when(pl.program_id(1) == 0)
    def _():
        s_ref[...] = jnp.zeros_like(s_ref)
        cr_ref[...] = jnp.zeros_like(cr_ref)
        ck_ref[...] = jnp.zeros_like(ck_ref)
        cv_ref[...] = jnp.zeros_like(cv_ref)
        ct_ref[...] = jnp.zeros_like(ct_ref)

    def shift_mix(x_ref, carry_ref, mu_ref):
        x = x_ref[...].astype(F32)
        row = _iota(x.shape, 0)
        prev = jnp.where(row == 0, carry_ref[7:8, :], pltpu.roll(x, 1, 0))
        carry_ref[...] = x[T - 8:T, :]
        return x + (prev - x) * mu_ref[...]

    r = shift_mix(r_ref, cr_ref, mu_r)
    k = shift_mix(k_ref, ck_ref, mu_k)
    v = shift_mix(v_ref, cv_ref, mu_v)
    tl = shift_mix(t_ref, ct_ref, mu_t)

    wl = _mm1(jnp.tanh(tl), w2_ref[...])
    al = _mm1(tl, a2_ref[...])
    gate = _mm1(_sigmoid(tl), g2_ref[...])
    w_log = -_softplus(-(w0_ref[...] + wl)) - 0.5
    lw = -jnp.exp(w_log)
    a_lr = _sigmoid(a0_ref[...] + al)

    ones_bd = jnp.where(_iota((B_WIDTH, B_WIDTH), 0) // N == _iota((B_WIDTH, B_WIDTH), 1) // N,
                        1.0, 0.0).astype(BF16)
    kk = k * kk_ref[...]
    k2 = k * (1.0 + (a_lr - 1.0) * ka_ref[...])
    sums = _x_sel(jnp.concatenate([kk * kk, r * k2 * rk_ref[...]], axis=0), ones_bd)
    kk = kk / jnp.maximum(jnp.sqrt(sums[:T]), 1e-12)
    bonus = sums[T:]
    aa = -kk
    bb = kk * a_lr

    rr = _iota((T, T), 0)
    cc = _iota((T, T), 1)
    tri = jnp.where((rr >= cc) & (rr // L == cc // L), 1.0, 0.0).astype(BF16)
    c = _mm_sel(tri, lw)
    enc = jnp.exp(-c)
    rt = (r * jnp.exp(c)).astype(BF16)
    at = (aa * jnp.exp(c - lw)).astype(BF16)
    bt = (bb * enc).astype(BF16)
    kt = (k2 * enc).astype(BF16)
    vb = v.astype(BF16)

    r2 = _iota((2 * L, L), 0)
    c2 = _iota((2 * L, L), 1)
    both = ((r2 < L) & (r2 > c2)) | (r2 - L >= c2)

    heads = range(B_HEADS)
    cols = lambda h: slice(h * N, (h + 1) * N)
    solved = {}
    state = {"s": [s_ref[h] for h in heads], "y": []}

    def solve_stages(ci):
        rs = slice(ci * L, (ci + 1) * L)
        ar = [jnp.concatenate([at[rs, cols(h)], rt[rs, cols(h)]], axis=0) for h in heads]
        vh = [vb[rs, cols(h)] for h in heads]
        m_b = [jnp.where(both, _dg(ar[h], bt[rs, cols(h)], _NT), 0.0) for h in heads]
        yield
        m_k = [jnp.where(both, _dg(ar[h], kt[rs, cols(h)], _NT), 0.0).astype(BF16) for h in heads]
        yield
        mkv = [_dg(m_k[h], vh[h], _NN) for h in heads]
        yield
        pw = [m_b[h][:L] for h in heads]
        x = [jnp.concatenate([ar[h][:L].astype(F32), mkv[h][:L]], axis=-1) for h in heads]
        x = [x[h] + _mm1(pw[h], x[h]) for h in heads]
        yield
        for _ in range(int(math.log2(L)) - 1):
            pw = [_mm1(pw[h], pw[h]) for h in heads]
            yield
            x = [x[h] + _mm1(pw[h], x[h]) for h in heads]
            yield
        c_last = c[(ci + 1) * L - 1:(ci + 1) * L, :]
        e_tail = jnp.exp(c_last - c[rs, :])
        solved[ci] = dict(
            wr=[jnp.concatenate([x[h][:, :N].astype(BF16), ar[h][L:]], axis=0) for h in heads],
            u0=[x[h][:, N:] for h in heads],
            a_rb=[m_b[h][L:].astype(BF16) for h in heads],
            y0=[mkv[h][L:] for h in heads], vh=vh, g_last=jnp.exp(c_last),
            bkl=jnp.concatenate([(bb[rs, :] * e_tail).astype(BF16),
                                 (k2[rs, :] * e_tail).astype(BF16)], axis=0))
        yield

    def state_stages(ci):
        q = solved.pop(ci)
        st = state["s"]
        ws = [_dg(q["wr"][h], st[h].astype(BF16), _NT) for h in heads]
        yield
        ub = [(ws[h][:L] + q["u0"][h]).astype(BF16) for h in heads]
        state["y"].append(jnp.concatenate(
            [ws[h][L:] + _dg(q["a_rb"][h], ub[h], _NN) + q["y0"][h] for h in heads], axis=-1))
        state["s"] = [st[h] * q["g_last"][:, cols(h)]
                      + _dg(jnp.concatenate([ub[h], q["vh"][h]], axis=0), q["bkl"][:, cols(h)], _TN)
                      for h in heads]
        yield

    for _ in zip(*[solve_stages(ci) for ci in range(nc)]):
        pass
    for ci in range(nc):
        for _ in state_stages(ci):
            pass
    for h in heads:
        s_ref[h] = state["s"][h]
    ys = state["y"]

    y = jnp.concatenate(ys, axis=0)
    mom =_x_sel(jnp.concatenate([y, y * y], axis=0), ones_bd) * (1.0 / N)
    mean = mom[:T]
    var = jnp.maximum(mom[T:] - mean * mean, 0.0)
    y = (y - mean) * lax.rsqrt(var + GN_EPS) * lnw_ref[...] + lnb_ref[...]
    y = y + bonus * v
    o_ref[...] = (y * gate).astype(o_ref.dtype)


def _rwkv7(u, p, *, batch, seq):
    T = RWKV_TILE
    nt = seq // T
    col = lambda j: pl.BlockSpec((T, B_WIDTH), lambda b, c, j=j: (b * nt + c, j))
    tail = pl.BlockSpec((T, B_TAIL), lambda b, c: (b * nt + c, (4 * A_WIDTH + 3 * B_WIDTH) // B_TAIL))
    full = lambda a: pl.BlockSpec(a.shape, lambda b, c: (0,) * a.ndim)
    params = [p["mu_r"], p["mu_k"], p["mu_v"], p["mu_t"], p["w0"], p["w2"], p["a0"], p["a2"],
              p["g2"], p["kk"], p["ka"], p["rk"], p["lnw"], p["lnb"]]
    return pl.pallas_call(
        _rwkv7_kernel,
        grid=(batch, nt),
        in_specs=[col(4), col(5), col(6), tail] + [full(a) for a in params],
        out_specs=pl.BlockSpec((T, B_WIDTH), lambda b, c: (b * nt + c, 0)),
        out_shape=jax.ShapeDtypeStruct((batch * seq, B_WIDTH), BF16),
        scratch_shapes=[pltpu.VMEM((B_HEADS, B_HEAD, B_HEAD), F32),
                        pltpu.VMEM((8, B_WIDTH), F32), pltpu.VMEM((8, B_WIDTH), F32),
                        pltpu.VMEM((8, B_WIDTH), F32), pltpu.VMEM((8, B_TAIL), F32)],
        compiler_params=_cparams(("parallel", "arbitrary")),
        name="rwkv7_mix",
    )(u, u, u, u, *params)


def _s5_inc_kernel(u_ref, w_ref, z_ref):
    z_ref[...] = jnp.dot(u_ref[...], w_ref[...], preferred_element_type=F32)


def _s5_increments(u2, w, *, tm, tn):
    n, kdim = u2.shape
    ncol = w.shape[1]
    return pl.pallas_call(
        _s5_inc_kernel,
        grid=(n // tm, ncol // tn),
        in_specs=[pl.BlockSpec((tm, kdim), lambda i, j: (i, 0)),
                  pl.BlockSpec((kdim, tn), lambda i, j: (0, j))],
        out_specs=pl.BlockSpec((tm, tn), lambda i, j: (i, j)),
        out_shape=jax.ShapeDtypeStruct((n, ncol), F32),
        compiler_params=_cparams(("parallel", "parallel")),
        name="s5_increments",
    )(u2, w)


def _s5_carry_kernel(z_ref, lr_ref, li_ref, hp_ref, h_ref):
    n = z_ref.shape[0]
    half = z_ref.shape[1] // 2
    lr = lr_ref[...]
    li = li_ref[...]

    def body(i, carry):
        hr, hi = carry
        rows = pl.ds(pl.multiple_of(i * 8, 8), 8)
        zz = z_ref[rows, :]
        outs = []
        for j in range(8):
            outs.append(jnp.concatenate([hr, hi], axis=-1))
            hr, hi = (lr * hr - li * hi + zz[j:j + 1, :half],
                      lr * hi + li * hr + zz[j:j + 1, half:])
        h_ref[rows, :] = jnp.concatenate(outs, axis=0)
        return hr, hi

    zero = jnp.zeros((1, half), F32)
    lax.fori_loop(0, n // 8, body, (zero, zero))
    hp_ref[...] = h_ref[...].astype(hp_ref.dtype)


def _s5_carry(z, lr, li, *, batch):
    n, w = z.shape
    rows = n // batch
    vec = pl.BlockSpec((1, w // 2), lambda b: (0, 0))
    return pl.pallas_call(
        _s5_carry_kernel,
        grid=(batch,),
        in_specs=[pl.BlockSpec((rows, w), lambda b: (b, 0)), vec, vec],
        out_specs=pl.BlockSpec((rows, w), lambda b: (b, 0)),
        out_shape=jax.ShapeDtypeStruct((n, w), BF16),
        scratch_shapes=[pltpu.VMEM((rows, w), F32)],
        compiler_params=_cparams(("parallel",)),
        name="s5_carry",
    )(z, lr, li)


def _s5_out_kernel(u_ref, uj_ref, hp_ref, bd_ref, m_ref, d_ref, gw_ref, gb_ref, o_ref, acc_ref):
    t = pl.program_id(1)
    acc_ref[...] = jnp.dot(hp_ref[...], m_ref[...], preferred_element_type=F32)
    for s in range(S5_CHUNK):
        @pl.when(s <= t)
        def _():
            acc_ref[...] += jnp.dot(u_ref[:, s * C_WIDTH:(s + 1) * C_WIDTH], bd_ref[t - s],
                                    preferred_element_type=F32)
    z = _gelu_tanh(acc_ref[...] + d_ref[...] * uj_ref[...].astype(F32))
    gate = jnp.dot(z.astype(BF16), gw_ref[...], preferred_element_type=F32) + gb_ref[...]
    o_ref[...] = (z * _sigmoid(gate)).astype(o_ref.dtype)


def _s5_output(u2, hp, bd, mmat, d, gw, gb, *, tm):
    n, kdim = u2.shape
    kst = hp.shape[1]
    vec = pl.BlockSpec((1, C_WIDTH), lambda i, t: (0, 0))
    return pl.pallas_call(
        _s5_out_kernel,
        grid=(n // tm, S5_CHUNK),
        in_specs=[pl.BlockSpec((tm, kdim), lambda i, t: (i, 0)),
                  pl.BlockSpec((tm, C_WIDTH), lambda i, t: (i, t)),
                  pl.BlockSpec((tm, kst), lambda i, t: (i, 0)),
                  pl.BlockSpec(bd.shape, lambda i, t: (0, 0, 0)),
                  pl.BlockSpec((kst, C_WIDTH), lambda i, t: (0, t)),
                  vec, pl.BlockSpec((C_WIDTH, C_WIDTH), lambda i, t: (0, 0)), vec],
        out_specs=pl.BlockSpec((tm, C_WIDTH), lambda i, t: (i, t)),
        out_shape=jax.ShapeDtypeStruct((n, kdim), BF16),
        scratch_shapes=[pltpu.VMEM((tm, C_WIDTH), F32)],
        compiler_params=_cparams(("parallel", "arbitrary")),
        name="s5_output",
    )(u2, u2, hp, bd, mmat, d, gw, gb)


def _s5_params(lam_re, lam_im, log_step, b_re, b_im, c_re, c_im):
    L = S5_CHUNK
    step = jnp.exp(log_step)[:, None]
    mag = jnp.exp(lam_re * step)
    lr, li = mag * jnp.cos(lam_im * step), mag * jnp.sin(lam_im * step)
    den = lam_re * lam_re + lam_im * lam_im
    nr, ni = lr - 1.0, li
    coef_re = (nr * lam_re + ni * lam_im) / den
    coef_im = (ni * lam_re - nr * lam_im) / den
    bb_re = coef_re[..., None] * b_re - coef_im[..., None] * b_im
    bb_im = coef_re[..., None] * b_im + coef_im[..., None] * b_re
    pr, pi = [jnp.ones_like(lr)], [jnp.zeros_like(lr)]
    for _ in range(L):
        pr.append(pr[-1] * lr - pi[-1] * li)
        pi.append(pr[-2] * li + pi[-1] * lr)
    pr, pi = jnp.stack(pr), jnp.stack(pi)
    hp = lax.Precision.HIGHEST
    zr = pr[:L, :, :, None] * bb_re[None] - pi[:L, :, :, None] * bb_im[None]
    zi = pr[:L, :, :, None] * bb_im[None] + pi[:L, :, :, None] * bb_re[None]
    kern = (jnp.einsum("gop,lgpi->lgoi", c_re, zr, precision=hp)
            - jnp.einsum("gop,lgpi->lgoi", c_im, zi, precision=hp))
    nin = L * C_WIDTH
    nst = 2 * C_GROUPS * C_STATE

    def spread(a, row_group, n_inner):
        rows, cols = a.shape
        wide = cols * C_GROUPS
        cw = jnp.arange(wide)
        src = (cw // (C_GROUPS * n_inner)) * n_inner + cw % n_inner
        rep = (jnp.arange(cols)[:, None] == src[None, :]).astype(BF16)
        out = jnp.dot(a.astype(BF16), rep, preferred_element_type=F32)
        keep = row_group[:, None] == ((cw // n_inner) % C_GROUPS)[None, :]
        return jnp.where(keep, out, 0.0).astype(BF16)

    ka = kern.transpose(0, 1, 3, 2).reshape(L * C_WIDTH, C_GROUP)
    bd = spread(ka, (jnp.arange(L * C_WIDTH) // C_GROUP) % C_GROUPS, C_GROUP).reshape(L, C_WIDTH, C_WIDTH)
    wz = jnp.stack([zr[::-1], zi[::-1]])
    wa = wz.transpose(1, 2, 4, 0, 3).reshape(nin, 2 * C_STATE)
    w = spread(wa, (jnp.arange(nin) // C_GROUP) % C_GROUPS, C_STATE)
    qr, qi = pr[1:], pi[1:]
    m_re = c_re[None] * qr[:, :, None, :] - c_im[None] * qi[:, :, None, :]
    m_im = -(c_re[None] * qi[:, :, None, :] + c_im[None] * qr[:, :, None, :])
    ma = jnp.stack([m_re, m_im]).transpose(0, 2, 4, 1, 3).reshape(nst, L * C_GROUP)
    mmat = spread(ma, (jnp.arange(nst) // C_STATE) % C_GROUPS, C_GROUP)
    return {"w": w, "bd": bd, "m": mmat,
            "lr": pr[L].reshape(1, -1), "li": pi[L].reshape(1, -1)}


def _mlstm_kernel(q_ref, k_ref, v_ref, o_ref, gt_ref, cq_ref, ck_ref, gb_ref, gain_ref, y_ref,
                  c_ref, n_ref, m_ref, pq_ref, pk_ref):
    L = MLSTM_CHUNK
    P = D_HEAD_PAD

    @pl.when(pl.program_id(1) == 0)
    def _():
        c_ref[...] = jnp.zeros_like(c_ref)
        n_ref[...] = jnp.zeros_like(n_ref)
        m_ref[...] = jnp.full(m_ref.shape, M_INIT, F32)
        pq_ref[...] = jnp.zeros_like(pq_ref)
        pk_ref[...] = jnp.zeros_like(pk_ref)

    T = MLSTM_TILE
    nc = T // L
    PREV = pq_ref.shape[0]
    LE = L + PREV
    out_row = _iota((L, D_CONV * LE), 0)
    kcol = _iota((L, D_CONV * LE), 1)
    sel = jnp.where(kcol % LE == out_row + kcol // LE - (D_CONV - 1) + PREV, 1.0, 0.0).astype(BF16)

    def conv(x_ref, prev_ref, w_ref):
        x = x_ref[...]
        outs = []
        for ci in range(nc):
            prev = prev_ref[...] if ci == 0 else x[ci * L - PREV:ci * L, :]
            xe = jnp.concatenate([prev, x[ci * L:(ci + 1) * L, :]], axis=0).astype(F32)
            stack = jnp.concatenate([(xe * w_ref[j:j + 1, :]).astype(BF16) for j in range(D_CONV)], axis=0)
            outs.append(_dg(sel, stack, _NN))
        prev_ref[...] = x[T - PREV:T, :]
        return _silu(jnp.concatenate(outs, axis=0))

    q = conv(q_ref, pq_ref, cq_ref)
    k = conv(k_ref, pk_ref, ck_ref) * (1.0 / math.sqrt(D_HEAD))
    vb = v_ref[...]

    gates = gt_ref[...].astype(F32) + gb_ref[...]
    col = _iota(gates.shape, 1)
    is_f = (col >= D_HEADS) & (col < 2 * D_HEADS)
    gates = jnp.where(is_f, jnp.minimum(gates, 0.0) - jnp.log(1.0 + jnp.exp(-jnp.abs(gates))), gates)
    rr = _iota((T, T), 0)
    cc = _iota((T, T), 1)
    tri = jnp.where((rr >= cc) & (rr // L == cc // L), 1.0, 0.0).astype(BF16)
    bcum = _mm_sel(tri, gates)
    gates_t = gates.T
    bcum_t = bcum.T
    lower = _tril_mask(L)

    heads = range(D_HEADS)
    sls = [slice(h * P, (h + 1) * P) for h in heads]
    qb, kb = q.astype(BF16), k.astype(BF16)
    rows = [slice(ci * L, (ci + 1) * L) for ci in range(nc)]
    qk0 = [[_dg(qb[rs, sls[h]], kb[rs, sls[h]], _NT) for h in heads] for rs in rows]
    c_st = [c_ref[h] for h in heads]
    n_st = [n_ref[h] for h in heads]
    m_st = [m_ref[h] for h in heads]

    probs = [(ci, h) for ci in range(nc) for h in heads]
    g = {}
    for ci, h in probs:
        rs = rows[ci]
        b_col = bcum[rs, D_HEADS + h:D_HEADS + h + 1]
        b_row = bcum_t[D_HEADS + h:D_HEADS + h + 1, rs]
        dmat = jnp.where(lower, b_col - b_row + gates_t[h:h + 1, rs], -jnp.inf)
        g[ci, h] = dict(b_col=b_col, i_col=gates[rs, h:h + 1], dmat=dmat,
                        dmax=jnp.max(dmat, axis=-1, keepdims=True), b_last=b_col[L - 1:L, :])
    for ci, h in probs:
        e = g[ci, h]
        e["e_inter"] = e["b_col"] + m_st[h]
        e["m_t"] = jnp.maximum(e["e_inter"], e["dmax"])
        e["m_new"] = e["m_t"][L - 1:L, :]
        e["d_prev"] = jnp.exp(e["b_last"] + m_st[h] - e["m_new"])
        m_st[h] = e["m_new"]
    for p in probs:
        e = g[p]
        e["w_intra"] = jnp.exp(e["dmat"] - e["m_t"])
        e["s_inter"] = jnp.exp(e["e_inter"] - e["m_t"])
        e["w_last"] = jnp.exp(e["b_last"] - e["b_col"] + e["i_col"] - e["m_new"])
    qk = {(ci, h): qk0[ci][h] * g[ci, h]["w_intra"] for ci, h in probs}
    kw = {(ci, h): k[rows[ci], sls[h]] * g[ci, h]["w_last"] for ci, h in probs}
    qkv = {(ci, h): _dg(qk[ci, h].astype(BF16), vb[rows[ci], sls[h]], _NN) for ci, h in probs}
    cu = {(ci, h): _dg(kw[ci, h].astype(BF16), vb[rows[ci], sls[h]], _TN) for ci, h in probs}
    c_in, n_in = {}, {}
    for ci, h in probs:
        c_in[ci, h], n_in[ci, h] = c_st[h], n_st[h]
        c_st[h] = g[ci, h]["d_prev"] * c_st[h] + cu[ci, h]
        n_st[h] = g[ci, h]["d_prev"] * n_st[h] + jnp.sum(kw[ci, h], axis=0, keepdims=True)
    qc = {(ci, h): _dg(qb[rows[ci], sls[h]], c_in[ci, h].astype(BF16), _NN) for ci, h in probs}
    for ci, h in probs:
        e, rs, sl = g[ci, h], rows[ci], sls[h]
        num = e["s_inter"] * qc[ci, h] + qkv[ci, h]
        den = (e["s_inter"] * jnp.sum(q[rs, sl] * n_in[ci, h], axis=-1, keepdims=True)
               + jnp.sum(qk[ci, h], axis=-1, keepdims=True))
        hh = num / jnp.maximum(jnp.abs(den), jnp.exp(-e["m_t"]))
        ms = jnp.sum(hh * hh, axis=-1, keepdims=True) * (1.0 / D_HEAD)
        out = hh * lax.rsqrt(ms + RMS_EPS) * gain_ref[:, sl] * _sigmoid(o_ref[rs, sl].astype(F32))
        y_ref[rs, sl] = out.astype(y_ref.dtype)

    for h in heads:
        c_ref[h] = c_st[h]
        n_ref[h] = n_st[h]
        m_ref[h] = m_st[h]


def _mlstm(u, conv_q, conv_k, gate_bias, gain, *, batch, seq):
    L = MLSTM_TILE
    nt = seq // L
    W = D_WIDTH_PAD
    col = lambda j: pl.BlockSpec((L, W), lambda b, c, j=j: (b * nt + c, j))
    gcol = pl.BlockSpec((L, D_GATE_PAD), lambda b, c: (b * nt + c, (4 * W + C_WIDTH) // D_GATE_PAD))
    full = lambda a: pl.BlockSpec(a.shape, lambda b, c: (0,) * a.ndim)
    return pl.pallas_call(
        _mlstm_kernel,
        grid=(batch, nt),
        in_specs=[col(0), col(1), col(2), col(3), gcol, full(conv_q), full(conv_k),
                  full(gate_bias), full(gain)],
        out_specs=pl.BlockSpec((L, W), lambda b, c: (b * nt + c, 0)),
        out_shape=jax.ShapeDtypeStruct((batch * seq, W), BF16),
        scratch_shapes=[pltpu.VMEM((D_HEADS, D_HEAD_PAD, D_HEAD_PAD), F32),
                        pltpu.VMEM((D_HEADS, 1, D_HEAD_PAD), F32),
                        pltpu.VMEM((D_HEADS, 1, 1), F32),
                        pltpu.VMEM((16, W), BF16), pltpu.VMEM((16, W), BF16)],
        compiler_params=_cparams(("parallel", "arbitrary")),
        name="mlstm_mix",
    )(u, u, u, u, u, conv_q, conv_k, gate_bias, gain)


def _pad_cols(a, n):
    return jnp.pad(a, [(0, 0)] * (a.ndim - 1) + [(0, n - a.shape[-1])])


def _pad_heads(a, axis):
    zshape = a.shape[:axis] + (D_HEAD_PAD - D_HEAD,) + a.shape[axis + 1:]
    pieces = []
    for h in range(D_HEADS):
        pieces += [lax.slice_in_dim(a, h * D_HEAD, (h + 1) * D_HEAD, axis=axis), jnp.zeros(zshape, a.dtype)]
    return jnp.concatenate(pieces, axis=axis)


def _row(v):
    return v.reshape(1, -1).astype(F32)


def kernel(x, norm_mix, norm_ffn, norm_final, w_in_even, w_out_even, lb_table, a_norm, b_mu, b_w0, b_w2, b_a0, b_a2, b_g2, b_kk, b_ka, b_rk, b_ln_w, b_ln_b, w_in_odd, w_out_odd, c_lam_re, c_lam_im, c_log_step, c_b_re, c_b_im, c_c_re, c_c_im, c_d, c_glu_w, c_glu_b, d_conv_q, d_conv_k, d_i_bias, d_f_bias, d_norm, ffn_gate, ffn_up, ffn_down):
    batch, seq, d = x.shape
    m = batch * seq
    xf = x.reshape(m, d).astype(F32)
    lower_bounds = jnp.cumsum(jax.nn.softmax(lb_table.astype(F32), axis=0), axis=0)

    w_in = _pad_cols(w_in_even[0], EVEN_IN_PAD).astype(BF16)
    u = _norm_matmul(xf, _row(norm_mix[0]), w_in, tm=512, tn=768)
    ya = _hgrn2(u, _row(lower_bounds[0]), _row(a_norm[0]), batch=batch, seq=seq)

    mu = b_mu[0].astype(F32)
    lora_rows = lambda w, off: jnp.pad(w.astype(F32), ((off, B_TAIL - off - w.shape[0]), (0, 0)))
    rw = {
        "mu_r": _row(mu[:B_WIDTH]), "mu_k": _row(mu[B_WIDTH:2 * B_WIDTH]),
        "mu_v": _row(mu[2 * B_WIDTH:3 * B_WIDTH]), "mu_t": _row(_pad_cols(mu[3 * B_WIDTH:], B_TAIL)),
        "w0": _row(b_w0[0]), "w2": lora_rows(b_w2[0], 0),
        "a0": _row(b_a0[0]), "a2": lora_rows(b_a2[0], B_DECAY_LORA),
        "g2": lora_rows(b_g2[0], B_DECAY_LORA + B_AAA_LORA).astype(BF16),
        "kk": _row(b_kk[0]), "ka": _row(b_ka[0]), "rk": _row(b_rk[0]),
        "lnw": _row(b_ln_w[0]), "lnb": _row(b_ln_b[0]),
    }
    yb = _rwkv7(u, rw, batch=batch, seq=seq)
    wo = w_out_even[0].astype(BF16)
    xf = _out_proj_ffn(ya, yb, wo[:A_WIDTH], wo[A_WIDTH:], xf, _row(norm_ffn[0]),
                       ffn_gate[0].astype(BF16), ffn_up[0].astype(BF16), ffn_down[0].astype(BF16),
                       _row(norm_final), tm=512, th=1408, final_norm=False)

    wi = w_in_odd[0].astype(BF16)
    seg = lambda j: _pad_heads(wi[:, C_WIDTH + j * D_WIDTH:C_WIDTH + (j + 1) * D_WIDTH], 1)
    w_in = jnp.concatenate(
        [seg(0), seg(1), seg(2), seg(3), wi[:, :C_WIDTH],
         _pad_cols(wi[:, C_WIDTH + 4 * D_WIDTH:], D_GATE_PAD)], axis=1)
    u = _norm_matmul(xf, _row(norm_mix[1]), w_in, tm=512, tn=640)

    u2 = u[:, 4 * D_WIDTH_PAD:4 * D_WIDTH_PAD + C_WIDTH].reshape(m // S5_CHUNK, S5_CHUNK * C_WIDTH)
    s5p = _s5_params(c_lam_re[0].astype(F32), c_lam_im[0].astype(F32), c_log_step[0].astype(F32),
                     c_b_re[0].astype(F32), c_b_im[0].astype(F32), c_c_re[0].astype(F32),
                     c_c_im[0].astype(F32))
    tr = min(1024, m // S5_CHUNK)
    z = _s5_increments(u2, s5p["w"], tm=tr, tn=512)
    hp = _s5_carry(z, s5p["lr"], s5p["li"], batch=batch)
    yc = _s5_output(u2, hp, s5p["bd"], s5p["m"], _row(c_d[0]), c_glu_w[0].astype(BF16),
                    _row(c_glu_b[0]), tm=tr).reshape(m, C_WIDTH)

    gate_bias = _row(_pad_cols(jnp.concatenate([d_i_bias[0], d_f_bias[0]]).astype(F32), D_GATE_PAD))
    yd = _mlstm(u, _pad_heads(d_conv_q[0].astype(F32), 1), _pad_heads(d_conv_k[0].astype(F32), 1),
                gate_bias, _row(_pad_heads(d_norm[0].astype(F32), 0)), batch=batch, seq=seq)
    wo = w_out_odd[0].astype(BF16)
    xf = _out_proj_ffn(yc, yd, wo[:C_WIDTH], _pad_heads(wo[C_WIDTH:], 0), xf,
                       _row(norm_ffn[1]), ffn_gate[1].astype(BF16), ffn_up[1].astype(BF16),
                       ffn_down[1].astype(BF16), _row(norm_final), tm=512, th=1408, final_norm=True)
    return xf.reshape(batch, seq, d).astype(x.dtype)
```

```python
import functools
import math

import jax
import jax.numpy as jnp
from jax import lax
from jax.experimental import pallas as pl
from jax.experimental.pallas import tpu as pltpu

F32 = jnp.float32
BF16 = jnp.bfloat16

RMS_EPS = 1e-6
GN_EPS = 64e-5
M_INIT = -1e30

D_MODEL = 1024
A_WIDTH, A_HEADS, A_KEY = 512, 4, 128
B_WIDTH, B_HEADS, B_HEAD = 512, 8, 64
B_DECAY_LORA, B_AAA_LORA, B_GATE_LORA = 32, 32, 96
B_TAIL = 256
EVEN_IN_PAD = 4 * A_WIDTH + 3 * B_WIDTH + B_TAIL
C_WIDTH, C_GROUP, C_GROUPS, C_STATE = 256, 16, 16, 64
D_HEADS, D_HEAD, D_HEAD_PAD = 4, 192, 256
D_WIDTH = D_HEADS * D_HEAD
D_WIDTH_PAD = D_HEADS * D_HEAD_PAD
D_CONV = 4
D_GATE_PAD = 128
ODD_IN_PAD = 4 * D_WIDTH_PAD + C_WIDTH + D_GATE_PAD
FFN_HIDDEN = 2816

HGRN_TILE = 128
HGRN_SUB = 16
RWKV_CHUNK = 64
RWKV_TILE = 256
S5_CHUNK = 16
MLSTM_CHUNK = 64
MLSTM_TILE = 256

VMEM_LIMIT = 56 * 1024 * 1024


def _cparams(sem):
    return pltpu.CompilerParams(dimension_semantics=sem, vmem_limit_bytes=VMEM_LIMIT)


def _split2(x):
    hi = x.astype(BF16)
    lo = (x - hi.astype(F32)).astype(BF16)
    return hi, lo


def _split3(x):
    hi = x.astype(BF16)
    r1 = x - hi.astype(F32)
    mid = r1.astype(BF16)
    lo = (r1 - mid.astype(F32)).astype(BF16)
    return hi, mid, lo


def _dg(a, b, dims):
    return lax.dot_general(a, b, (dims, ((), ())), preferred_element_type=F32)


_NN = ((1,), (0,))
_NT = ((1,), (1,))
_TN = ((0,), (0,))


def _mm1(a, b, dims=_NN):
    return _dg(a.astype(BF16), b.astype(BF16), dims)


def _mm3(a, b, dims=_NN):
    ah, al = _split2(a)
    bh, bl = _split2(b)
    return _dg(ah, bh, dims) + (_dg(ah, bl, dims) + _dg(al, bh, dims))


def _mm2(a, b, dims=_NN):
    ah, al = _split2(a)
    bh = b.astype(BF16)
    return _dg(ah, bh, dims) + _dg(al, bh, dims)


_mmi = _mm1


def _mm_sel(sel_bf16, x, dims=_NN):
    h, m, l = _split3(x)
    return _dg(sel_bf16, h, dims) + (_dg(sel_bf16, m, dims) + _dg(sel_bf16, l, dims))


def _x_sel(x, sel_bf16):
    n = x.shape[0]
    r = _dg(jnp.concatenate(_split3(x), axis=0), sel_bf16, _NN)
    return r[:n] + (r[n:2 * n] + r[2 * n:])


def _sigmoid(x):
    return 1.0 / (1.0 + jnp.exp(-x))


def _silu(x):
    return x * _sigmoid(x)


def _softplus(x):
    return jnp.maximum(x, 0.0) + jnp.log(1.0 + jnp.exp(-jnp.abs(x)))


def _gelu_tanh(x):
    c = math.sqrt(2.0 / math.pi)
    return 0.5 * x * (1.0 + jnp.tanh(c * (x + 0.044715 * (x * x * x))))


def _iota(shape, dim):
    return lax.broadcasted_iota(jnp.int32, shape, dim)


def _tril_mask(n, strict=False):
    r = _iota((n, n), 0)
    c = _iota((n, n), 1)
    return (r > c) if strict else (r >= c)


MXU_WIDTH = 256


def _col_chunks(n, target):
    bounds = list(range(0, n, target)) + [n]
    return list(zip(bounds[:-1], bounds[1:]))


def _resident(shape):
    return pl.BlockSpec(shape, lambda i: (0,) * len(shape))


def _norm_matmul_kernel(x_ref, g_ref, w_ref, o_ref, *, chunks):
    x = x_ref[...]
    ms = jnp.mean(x * x, axis=-1, keepdims=True)
    h = (x * lax.rsqrt(ms + RMS_EPS) * g_ref[...]).astype(BF16)
    for lo, hi in chunks:
        o_ref[:, lo:hi] = jnp.dot(h, w_ref[:, lo:hi], preferred_element_type=F32).astype(o_ref.dtype)


def _norm_matmul(x, g, w, *, tm, tn):
    m, d = x.shape
    n = w.shape[1]
    return pl.pallas_call(
        functools.partial(_norm_matmul_kernel, chunks=_col_chunks(n, tn)),
        grid=(m // tm,),
        in_specs=[pl.BlockSpec((tm, d), lambda i: (i, 0)), _resident((1, d)), _resident((d, n))],
        out_specs=pl.BlockSpec((tm, n), lambda i: (i, 0)),
        out_shape=jax.ShapeDtypeStruct((m, n), BF16),
        compiler_params=_cparams(("parallel",)),
        name="norm_in_proj",
    )(x, g, w)


def _ffn_kernel(a_ref, b_ref, wa_ref, wb_ref, x_ref, g_ref, wg_ref, wu_ref, wd_ref, gf_ref, o_ref,
                *, chunks, final_norm):
    x = x_ref[...] + jnp.dot(a_ref[...], wa_ref[...], preferred_element_type=F32)
    x = x + jnp.dot(b_ref[...], wb_ref[...], preferred_element_type=F32)
    ms = jnp.mean(x * x, axis=-1, keepdims=True)
    h = (x * lax.rsqrt(ms + RMS_EPS) * g_ref[...]).astype(BF16)
    y = x
    for lo, hi in chunks:
        gate = jnp.dot(h, wg_ref[:, lo:hi], preferred_element_type=F32)
        up = jnp.dot(h, wu_ref[:, lo:hi], preferred_element_type=F32)
        act = (_silu(gate) * up).astype(BF16)
        y = y + jnp.dot(act, wd_ref[lo:hi, :], preferred_element_type=F32)
    if final_norm:
        ms = jnp.mean(y * y, axis=-1, keepdims=True)
        y = y * lax.rsqrt(ms + RMS_EPS) * gf_ref[...]
    o_ref[...] = y


def _out_proj_ffn(a, b, wa, wb, x, g, wg, wu, wd, gf, *, tm, th, final_norm):
    m, d = x.shape
    hid = wg.shape[1]
    ka, kb = a.shape[1], b.shape[1]
    return pl.pallas_call(
        functools.partial(_ffn_kernel, chunks=_col_chunks(hid, th), final_norm=final_norm),
        grid=(m // tm,),
        in_specs=[
            pl.BlockSpec((tm, ka), lambda i: (i, 0)),
            pl.BlockSpec((tm, kb), lambda i: (i, 0)),
            _resident((ka, d)), _resident((kb, d)),
            pl.BlockSpec((tm, d), lambda i: (i, 0)),
            _resident((1, d)), _resident((d, hid)), _resident((d, hid)), _resident((hid, d)),
            _resident((1, d)),
        ],
        out_specs=pl.BlockSpec((tm, d), lambda i: (i, 0)),
        out_shape=jax.ShapeDtypeStruct((m, d), F32),
        compiler_params=_cparams(("parallel",)),
        name="out_proj_swiglu_ffn",
    )(a, b, wa, wb, x, g, wg, wu, wd, gf)


def _hgrn2_kernel(q_ref, f_ref, i_ref, g_ref, lb_ref, gain_ref, o_ref, s_ref):
    T, SUB = HGRN_TILE, HGRN_SUB
    nb = T // SUB

    @pl.when(pl.program_id(1) == 0)
    def _():
        s_ref[...] = jnp.zeros_like(s_ref)

    r = _iota((T, T), 0)
    c = _iota((T, T), 1)
    tri = jnp.where((r >= c) & ((r // SUB) == (c // SUB)), 1.0, 0.0).astype(BF16)
    HALF = SUB // 2
    rowh = _iota((nb, HALF, A_KEY), 1)
    heads = range(A_HEADS)
    sls = [slice(h * A_KEY, (h + 1) * A_KEY) for h in heads]
    intra, qts, kts, v3s, decs = [], [], [], [], []

    for h in heads:
        sl = sls[h]
        q = _silu(q_ref[:, sl].astype(F32))
        lb = lb_ref[:, sl]
        f = lb + (1.0 - lb) * _sigmoid(f_ref[:, sl].astype(F32))
        lf = jnp.log(f)
        k = 1.0 - f
        v = i_ref[:, sl].astype(F32)
        b = _mm_sel(tri, lf)
        b3 = b.reshape(nb, SUB, A_KEY)
        q3 = q.reshape(nb, SUB, A_KEY)
        k3 = k.reshape(nb, SUB, A_KEY)
        v3 = v.reshape(nb, SUB, A_KEY)
        log2e = 1.0 / math.log(2.0)
        b2 = b3 * log2e
        bk3 = b2 - jnp.log2(k3)
        b_up, b_lo = b2[:, :HALF], b2[:, HALF:]
        q_up, q_lo = q3[:, :HALF], q3[:, HALF:]
        acc_up = jnp.zeros((nb, HALF, A_KEY), F32)
        acc_lo = jnp.zeros((nb, HALF, A_KEY), F32)
        for s in range(HALF):
            bs = bk3[:, s:s + 1, :]
            vs = v3[:, s:s + 1, :]
            e_up = jnp.exp2(jnp.where(rowh >= s, b_up - bs, -jnp.inf))
            acc_up = acc_up + jnp.sum(q_up * e_up, axis=-1, keepdims=True) * vs
            acc_lo = acc_lo + jnp.sum(q_lo * jnp.exp2(b_lo - bs), axis=-1, keepdims=True) * vs
        for s in range(HALF):
            bs = bk3[:, HALF + s:HALF + s + 1, :]
            e_lo = jnp.exp2(jnp.where(rowh >= s, b_lo - bs, -jnp.inf))
            acc_lo = acc_lo + jnp.sum(q_lo * e_lo, axis=-1, keepdims=True) * v3[:, HALF + s:HALF + s + 1, :]
        intra.append(jnp.concatenate([acc_up, acc_lo], axis=1).reshape(T, A_KEY))

        bend = b3[:, SUB - 1:SUB, :]
        qts.append((q3 * jnp.exp(b3)).astype(BF16))
        kts.append((k3 * jnp.exp(bend - b3)).astype(BF16))
        v3s.append(v3.astype(BF16))
        decs.append(jnp.exp(bend))

    inc = [[_dg(v3s[h][i], kts[h][i], _TN) for i in range(nb)] for h in heads]
    for h in heads:
        st = s_ref[h]
        outs = []
        for i in range(nb):
            outs.append(_dg(qts[h][i], st.astype(BF16), _NT))
            st = st * decs[h][i] + inc[h][i]
        s_ref[h] = st
        o = intra[h] + jnp.concatenate(outs, axis=0)
        o = o * lax.rsqrt(jnp.mean(o * o, axis=-1, keepdims=True) + RMS_EPS) * gain_ref[:, sls[h]]
        o_ref[:, sls[h]] = (o * _silu(g_ref[:, sls[h]].astype(F32))).astype(o_ref.dtype)


def _hgrn2(u, lb, gain, *, batch, seq):
    T = HGRN_TILE
    nt = seq // T
    col = lambda j: pl.BlockSpec((T, A_WIDTH), lambda b, c, j=j: (b * nt + c, j))
    vec = pl.BlockSpec((1, A_WIDTH), lambda b, c: (0, 0))
    return pl.pallas_call(
        _hgrn2_kernel,
        grid=(batch, nt),
        in_specs=[col(0), col(1), col(2), col(3), vec, vec],
        out_specs=pl.BlockSpec((T, A_WIDTH), lambda b, c: (b * nt + c, 0)),
        out_shape=jax.ShapeDtypeStruct((batch * seq, A_WIDTH), BF16),
        scratch_shapes=[pltpu.VMEM((A_HEADS, A_KEY, A_KEY), F32)],
        compiler_params=_cparams(("parallel", "arbitrary")),
        name="hgrn2_mix",
    )(u, u, u, u, lb, gain)


def _rwkv7_kernel(r_ref, k_ref, v_ref, t_ref, mu_r, mu_k, mu_v, mu_t, w0_ref, w2_ref, a0_ref,
                  a2_ref, g2_ref, kk_ref, ka_ref, rk_ref, lnw_ref, lnb_ref, o_ref,
                  s_ref, cr_ref, ck_ref, cv_ref, ct_ref):
    L = RWKV_CHUNK
    T = RWKV_TILE
    N = B_HEAD
    nc = T // L

    @pl.when(pl.program_id(1) == 0)
    def _():
        s_ref[...] = jnp.zeros_like(s_ref)
        cr_ref[...] = jnp.zeros_like(cr_ref)
        ck_ref[...] = jnp.zeros_like(ck_ref)
        cv_ref[...] = jnp.zeros_like(cv_ref)
        ct_ref[...] = jnp.zeros_like(ct_ref)

    def shift_mix(x_ref, carry_ref, mu_ref):
        x = x_ref[...].astype(F32)
        row = _iota(x.shape, 0)
        prev = jnp.where(row == 0, carry_ref[7:8, :], pltpu.roll(x, 1, 0))
        carry_ref[...] = x[T - 8:T, :]
        return x + (prev - x) * mu_ref[...]

    r = shift_mix(r_ref, cr_ref, mu_r)
    k = shift_mix(k_ref, ck_ref, mu_k)
    v = shift_mix(v_ref, cv_ref, mu_v)
    tl = shift_mix(t_ref, ct_ref, mu_t)

    wl = _mm1(jnp.tanh(tl), w2_ref[...])
    al = _mm1(tl, a2_ref[...])
    gate = _mm1(_sigmoid(tl), g2_ref[...])
    w_log = -_softplus(-(w0_ref[...] + wl)) - 0.5
    lw = -jnp.exp(w_log)
    a_lr = _sigmoid(a0_ref[...] + al)

    ones_bd = jnp.where(_iota((MXU_WIDTH, MXU_WIDTH), 0) // N == _iota((MXU_WIDTH, MXU_WIDTH), 1) // N,
                        1.0, 0.0).astype(BF16)

    def head_sums(x):
        n = x.shape[0]
        parts = jnp.concatenate(_split2(x), axis=0)
        r = jnp.concatenate([_dg(parts[:, g * MXU_WIDTH:(g + 1) * MXU_WIDTH], ones_bd, _NN)
                             for g in range(B_WIDTH // MXU_WIDTH)], axis=-1)
        return r[:n] + r[n:]

    kk = k * kk_ref[...]
    k2 = k * (1.0 + (a_lr - 1.0) * ka_ref[...])
    sums = head_sums(jnp.concatenate([kk * kk, r * k2 * rk_ref[...]], axis=0))
    kk = kk / jnp.maximum(jnp.sqrt(sums[:T]), 1e-12)
    bonus = sums[T:]
    aa = -kk
    bb = kk * a_lr

    rr = _iota((T, T), 0)
    cc = _iota((T, T), 1)
    tri = jnp.where((rr >= cc) & (rr // L == cc // L), 1.0, 0.0).astype(BF16)
    c = _mm_sel(tri, lw)
    enc = jnp.exp(-c)
    rt = (r * jnp.exp(c)).astype(BF16)
    at = (aa * jnp.exp(c - lw)).astype(BF16)
    bt = (bb * enc).astype(BF16)
    kt = (k2 * enc).astype(BF16)
    vb = v.astype(BF16)

    PW = 2 * N
    pairs = range(B_HEADS // 2)
    pc = lambda p: slice(p * PW, (p + 1) * PW)
    r2 = _iota((2 * L, PW), 0)
    c2 = _iota((2 * L, PW), 1) % L
    both = ((r2 < L) & (r2 > c2)) | (r2 - L >= c2)
    left1 = _iota((L, PW), 1) < N
    left2 = (_iota((L, 2 * PW), 1) % PW) < N
    same_head = (_iota((PW, PW), 0) // N) == (_iota((PW, PW), 1) // N)

    def bdiag(z, left):
        zero = jnp.zeros_like(z)
        return jnp.concatenate([jnp.where(left, z, zero), jnp.where(left, zero, z)], axis=0)

    solved = {}
    state = {"s": [s_ref[p] for p in pairs], "y": []}

    def solve_stages(ci):
        rs = slice(ci * L, (ci + 1) * L)
        ar = [jnp.concatenate([at[rs, pc(p)], rt[rs, pc(p)]], axis=0) for p in pairs]
        vd = [bdiag(vb[rs, pc(p)], left1) for p in pairs]
        m_b = [jnp.where(both, _dg(ar[p], bdiag(bt[rs, pc(p)], left1), _NT), 0.0) for p in pairs]
        yield
        m_k = [jnp.where(both, _dg(ar[p], bdiag(kt[rs, pc(p)], left1), _NT), 0.0).astype(BF16) for p in pairs]
        yield
        mkv = [_dg(m_k[p], vd[p], _NN) for p in pairs]
        yield
        pw = [m_b[p][:L] for p in pairs]
        x = [jnp.concatenate([ar[p][:L].astype(F32), mkv[p][:L]], axis=-1) for p in pairs]
        x = [x[p] + _dg(pw[p].astype(BF16), bdiag(x[p].astype(BF16), left2), _NN) for p in pairs]
        yield
        for _ in range(int(math.log2(L)) - 1):
            pw = [_dg(pw[p].astype(BF16), bdiag(pw[p].astype(BF16), left1), _NN) for p in pairs]
            yield
            x = [x[p] + _dg(pw[p].astype(BF16), bdiag(x[p].astype(BF16), left2), _NN) for p in pairs]
            yield
        c_last = c[(ci + 1) * L - 1:(ci + 1) * L, :]
        e_tail = jnp.exp(c_last - c[rs, :])
        solved[ci] = dict(
            wr=[jnp.concatenate([x[p][:, :PW].astype(BF16), ar[p][L:]], axis=0) for p in pairs],
            u0=[x[p][:, PW:] for p in pairs],
            a_rb=[m_b[p][L:].astype(BF16) for p in pairs],
            y0=[mkv[p][L:] for p in pairs], vh=[vb[rs, pc(p)] for p in pairs], g_last=jnp.exp(c_last),
            bkl=jnp.concatenate([(bb[rs, :] * e_tail).astype(BF16),
                                 (k2[rs, :] * e_tail).astype(BF16)], axis=0))
        yield

    def state_stages(ci):
        q = solved.pop(ci)
        st = state["s"]
        ws = [_dg(q["wr"][p], st[p].astype(BF16), _NT) for p in pairs]
        yield
        ub = [(ws[p][:L] + q["u0"][p]).astype(BF16) for p in pairs]
        state["y"].append(jnp.concatenate(
            [ws[p][L:] + _dg(q["a_rb"][p], bdiag(ub[p], left1), _NN) + q["y0"][p] for p in pairs], axis=-1))
        upd = [_dg(jnp.concatenate([ub[p], q["vh"][p]], axis=0), q["bkl"][:, pc(p)], _TN) for p in pairs]
        state["s"] = [st[p] * q["g_last"][:, pc(p)] + jnp.where(same_head, upd[p], 0.0) for p in pairs]
        yield

    for _ in zip(*[solve_stages(ci) for ci in range(nc)]):
        pass
    for ci in range(nc):
        for _ in state_stages(ci):
            pass
    for p in pairs:
        s_ref[p] = state["s"][p]
    ys = state["y"]

    y = jnp.concatenate(ys, axis=0)
    mom = head_sums(jnp.concatenate([y, y * y], axis=0)) * (1.0 / N)
    mean = mom[:T]
    var = jnp.maximum(mom[T:] - mean * mean, 0.0)
    y = (y - mean) * lax.rsqrt(var + GN_EPS) * lnw_ref[...] + lnb_ref[...]
    y = y + bonus * v
    o_ref[...] = (y * gate).astype(o_ref.dtype)


def _rwkv7(u, p, *, batch, seq):
    T = RWKV_TILE
    nt = seq // T
    col = lambda j: pl.BlockSpec((T, B_WIDTH), lambda b, c, j=j: (b * nt + c, j))
    tail = pl.BlockSpec((T, B_TAIL), lambda b, c: (b * nt + c, (4 * A_WIDTH + 3 * B_WIDTH) // B_TAIL))
    full = lambda a: pl.BlockSpec(a.shape, lambda b, c: (0,) * a.ndim)
    params = [p["mu_r"], p["mu_k"], p["mu_v"], p["mu_t"], p["w0"], p["w2"], p["a0"], p["a2"],
              p["g2"], p["kk"], p["ka"], p["rk"], p["lnw"], p["lnb"]]
    return pl.pallas_call(
        _rwkv7_kernel,
        grid=(batch, nt),
        in_specs=[col(4), col(5), col(6), tail] + [full(a) for a in params],
        out_specs=pl.BlockSpec((T, B_WIDTH), lambda b, c: (b * nt + c, 0)),
        out_shape=jax.ShapeDtypeStruct((batch * seq, B_WIDTH), BF16),
        scratch_shapes=[pltpu.VMEM((B_HEADS // 2, 2 * B_HEAD, 2 * B_HEAD), F32),
                        pltpu.VMEM((8, B_WIDTH), F32), pltpu.VMEM((8, B_WIDTH), F32),
                        pltpu.VMEM((8, B_WIDTH), F32), pltpu.VMEM((8, B_TAIL), F32)],
        compiler_params=_cparams(("parallel", "arbitrary")),
        name="rwkv7_mix",
    )(u, u, u, u, *params)


def _s5_inc_kernel(u_ref, w_ref, z_ref):
    z_ref[...] = jnp.dot(u_ref[...], w_ref[...], preferred_element_type=F32)


def _s5_increments(u2, w, *, tm, tn):
    n, kdim = u2.shape
    ncol = w.shape[1]
    return pl.pallas_call(
        _s5_inc_kernel,
        grid=(n // tm, ncol // tn),
        in_specs=[pl.BlockSpec((tm, kdim), lambda i, j: (i, 0)),
                  pl.BlockSpec((kdim, tn), lambda i, j: (0, j))],
        out_specs=pl.BlockSpec((tm, tn), lambda i, j: (i, j)),
        out_shape=jax.ShapeDtypeStruct((n, ncol), F32),
        compiler_params=_cparams(("parallel", "parallel")),
        name="s5_increments",
    )(u2, w)


def _s5_carry_kernel(z_ref, lr_ref, li_ref, hp_ref, h_ref):
    n = z_ref.shape[0]
    half = z_ref.shape[1] // 2
    lr = lr_ref[...]
    li = li_ref[...]

    def body(i, carry):
        hr, hi = carry
        rows = pl.ds(pl.multiple_of(i * 8, 8), 8)
        zz = z_ref[rows, :]
        outs = []
        for j in range(8):
            outs.append(jnp.concatenate([hr, hi], axis=-1))
            hr, hi = (lr * hr - li * hi + zz[j:j + 1, :half],
                      lr * hi + li * hr + zz[j:j + 1, half:])
        h_ref[rows, :] = jnp.concatenate(outs, axis=0)
        return hr, hi

    zero = jnp.zeros((1, half), F32)
    lax.fori_loop(0, n // 8, body, (zero, zero))
    hp_ref[...] = h_ref[...].astype(hp_ref.dtype)


def _s5_carry(z, lr, li, *, batch):
    n, w = z.shape
    rows = n // batch
    vec = pl.BlockSpec((1, w // 2), lambda b: (0, 0))
    return pl.pallas_call(
        _s5_carry_kernel,
        grid=(batch,),
        in_specs=[pl.BlockSpec((rows, w), lambda b: (b, 0)), vec, vec],
        out_specs=pl.BlockSpec((rows, w), lambda b: (b, 0)),
        out_shape=jax.ShapeDtypeStruct((n, w), BF16),
        scratch_shapes=[pltpu.VMEM((rows, w), F32)],
        compiler_params=_cparams(("parallel",)),
        name="s5_carry",
    )(z, lr, li)


def _s5_out_kernel(u_ref, uj_ref, hp_ref, bd_ref, m_ref, d_ref, gw_ref, gb_ref, o_ref, acc_ref):
    t = pl.program_id(1)
    acc_ref[...] = jnp.dot(hp_ref[...], m_ref[...], preferred_element_type=F32)
    for s in range(S5_CHUNK):
        @pl.when(s <= t)
        def _():
            acc_ref[...] += jnp.dot(u_ref[:, s * C_WIDTH:(s + 1) * C_WIDTH], bd_ref[t - s],
                                    preferred_element_type=F32)
    z = _gelu_tanh(acc_ref[...] + d_ref[...] * uj_ref[...].astype(F32))
    gate = jnp.dot(z.astype(BF16), gw_ref[...], preferred_element_type=F32) + gb_ref[...]
    o_ref[...] = (z * _sigmoid(gate)).astype(o_ref.dtype)


def _s5_output(u2, hp, bd, mmat, d, gw, gb, *, tm):
    n, kdim = u2.shape
    kst = hp.shape[1]
    vec = pl.BlockSpec((1, C_WIDTH), lambda i, t: (0, 0))
    return pl.pallas_call(
        _s5_out_kernel,
        grid=(n // tm, S5_CHUNK),
        in_specs=[pl.BlockSpec((tm, kdim), lambda i, t: (i, 0)),
                  pl.BlockSpec((tm, C_WIDTH), lambda i, t: (i, t)),
                  pl.BlockSpec((tm, kst), lambda i, t: (i, 0)),
                  pl.BlockSpec(bd.shape, lambda i, t: (0, 0, 0)),
                  pl.BlockSpec((kst, C_WIDTH), lambda i, t: (0, t)),
                  vec, pl.BlockSpec((C_WIDTH, C_WIDTH), lambda i, t: (0, 0)), vec],
        out_specs=pl.BlockSpec((tm, C_WIDTH), lambda i, t: (i, t)),
        out_shape=jax.ShapeDtypeStruct((n, kdim), BF16),
        scratch_shapes=[pltpu.VMEM((tm, C_WIDTH), F32)],
        compiler_params=_cparams(("parallel", "arbitrary")),
        name="s5_output",
    )(u2, u2, hp, bd, mmat, d, gw, gb)


def _s5_params(lam_re, lam_im, log_step, b_re, b_im, c_re, c_im):
    L = S5_CHUNK
    step = jnp.exp(log_step)[:, None]
    mag = jnp.exp(lam_re * step)
    lr, li = mag * jnp.cos(lam_im * step), mag * jnp.sin(lam_im * step)
    den = lam_re * lam_re + lam_im * lam_im
    nr, ni = lr - 1.0, li
    coef_re = (nr * lam_re + ni * lam_im) / den
    coef_im = (ni * lam_re - nr * lam_im) / den
    bb_re = coef_re[..., None] * b_re - coef_im[..., None] * b_im
    bb_im = coef_re[..., None] * b_im + coef_im[..., None] * b_re
    pr, pi = [jnp.ones_like(lr)], [jnp.zeros_like(lr)]
    for _ in range(L):
        pr.append(pr[-1] * lr - pi[-1] * li)
        pi.append(pr[-2] * li + pi[-1] * lr)
    pr, pi = jnp.stack(pr), jnp.stack(pi)
    hp = lax.Precision.HIGHEST
    zr = pr[:L, :, :, None] * bb_re[None] - pi[:L, :, :, None] * bb_im[None]
    zi = pr[:L, :, :, None] * bb_im[None] + pi[:L, :, :, None] * bb_re[None]
    kern = (jnp.einsum("gop,lgpi->lgoi", c_re, zr, precision=hp)
            - jnp.einsum("gop,lgpi->lgoi", c_im, zi, precision=hp))
    nin = L * C_WIDTH
    nst = 2 * C_GROUPS * C_STATE

    def spread(a, row_group, n_inner):
        rows, cols = a.shape
        wide = cols * C_GROUPS
        cw = jnp.arange(wide)
        src = (cw // (C_GROUPS * n_inner)) * n_inner + cw % n_inner
        rep = (jnp.arange(cols)[:, None] == src[None, :]).astype(BF16)
        out = jnp.dot(a.astype(BF16), rep, preferred_element_type=F32)
        keep = row_group[:, None] == ((cw // n_inner) % C_GROUPS)[None, :]
        return jnp.where(keep, out, 0.0).astype(BF16)

    ka = kern.transpose(0, 1, 3, 2).reshape(L * C_WIDTH, C_GROUP)
    bd = spread(ka, (jnp.arange(L * C_WIDTH) // C_GROUP) % C_GROUPS, C_GROUP).reshape(L, C_WIDTH, C_WIDTH)
    wz = jnp.stack([zr[::-1], zi[::-1]])
    wa = wz.transpose(1, 2, 4, 0, 3).reshape(nin, 2 * C_STATE)
    w = spread(wa, (jnp.arange(nin) // C_GROUP) % C_GROUPS, C_STATE)
    qr, qi = pr[1:], pi[1:]
    m_re = c_re[None] * qr[:, :, None, :] - c_im[None] * qi[:, :, None, :]
    m_im = -(c_re[None] * qi[:, :, None, :] + c_im[None] * qr[:, :, None, :])
    ma = jnp.stack([m_re, m_im]).transpose(0, 2, 4, 1, 3).reshape(nst, L * C_GROUP)
    mmat = spread(ma, (jnp.arange(nst) // C_STATE) % C_GROUPS, C_GROUP)
    return {"w": w, "bd": bd, "m": mmat,
            "lr": pr[L].reshape(1, -1), "li": pi[L].reshape(1, -1)}


def _mlstm_kernel(q_ref, k_ref, v_ref, o_ref, gt_ref, cq_ref, ck_ref, gb_ref, gain_ref, y_ref,
                  c_ref, n_ref, m_ref, pq_ref, pk_ref):
    L = MLSTM_CHUNK
    P = D_HEAD_PAD

    @pl.when(pl.program_id(1) == 0)
    def _():
        c_ref[...] = jnp.zeros_like(c_ref)
        n_ref[...] = jnp.zeros_like(n_ref)
        m_ref[...] = jnp.full(m_ref.shape, M_INIT, F32)
        pq_ref[...] = jnp.zeros_like(pq_ref)
        pk_ref[...] = jnp.zeros_like(pk_ref)

    T = MLSTM_TILE
    nc = T // L
    PREV = pq_ref.shape[0]
    LE = L + PREV
    out_row = _iota((L, D_CONV * LE), 0)
    kcol = _iota((L, D_CONV * LE), 1)
    sel = jnp.where(kcol % LE == out_row + kcol // LE - (D_CONV - 1) + PREV, 1.0, 0.0).astype(BF16)

    def conv(x_ref, prev_ref, w_ref):
        x = x_ref[...]
        outs = []
        for ci in range(nc):
            prev = prev_ref[...] if ci == 0 else x[ci * L - PREV:ci * L, :]
            xe = jnp.concatenate([prev, x[ci * L:(ci + 1) * L, :]], axis=0).astype(F32)
            stack = jnp.concatenate([(xe * w_ref[j:j + 1, :]).astype(BF16) for j in range(D_CONV)], axis=0)
            outs.append(_dg(sel, stack, _NN))
        prev_ref[...] = x[T - PREV:T, :]
        return _silu(jnp.concatenate(outs, axis=0))

    q = conv(q_ref, pq_ref, cq_ref)
    k = conv(k_ref, pk_ref, ck_ref) * (1.0 / math.sqrt(D_HEAD))
    vb = v_ref[...]

    gates = gt_ref[...].astype(F32) + gb_ref[...]
    col = _iota(gates.shape, 1)
    is_f = (col >= D_HEADS) & (col < 2 * D_HEADS)
    gates = jnp.where(is_f, jnp.minimum(gates, 0.0) - jnp.log(1.0 + jnp.exp(-jnp.abs(gates))), gates)
    rr = _iota((T, T), 0)
    cc = _iota((T, T), 1)
    tri = jnp.where((rr >= cc) & (rr // L == cc // L), 1.0, 0.0).astype(BF16)
    bcum = _mm_sel(tri, gates)
    gates_t = gates.T
    bcum_t = bcum.T
    lower = _tril_mask(L)

    heads = range(D_HEADS)
    sls = [slice(h * P, (h + 1) * P) for h in heads]
    qb, kb = q.astype(BF16), k.astype(BF16)
    rows = [slice(ci * L, (ci + 1) * L) for ci in range(nc)]
    qk0 = [[_dg(qb[rs, sls[h]], kb[rs, sls[h]], _NT) for h in heads] for rs in rows]
    c_st = [c_ref[h] for h in heads]
    n_st = [n_ref[h] for h in heads]
    m_st = [m_ref[h] for h in heads]

    probs = [(ci, h) for ci in range(nc) for h in heads]
    g = {}
    for ci, h in probs:
        rs = rows[ci]
        b_col = bcum[rs, D_HEADS + h:D_HEADS + h + 1]
        b_row = bcum_t[D_HEADS + h:D_HEADS + h + 1, rs]
        dmat = jnp.where(lower, b_col - b_row + gates_t[h:h + 1, rs], -jnp.inf)
        g[ci, h] = dict(b_col=b_col, i_col=gates[rs, h:h + 1], dmat=dmat,
                        dmax=jnp.max(dmat, axis=-1, keepdims=True), b_last=b_col[L - 1:L, :])
    for ci, h in probs:
        e = g[ci, h]
        e["e_inter"] = e["b_col"] + m_st[h]
        e["m_t"] = jnp.maximum(e["e_inter"], e["dmax"])
        e["m_new"] = e["m_t"][L - 1:L, :]
        e["d_prev"] = jnp.exp(e["b_last"] + m_st[h] - e["m_new"])
        m_st[h] = e["m_new"]
    for p in probs:
        e = g[p]
        e["w_intra"] = jnp.exp(e["dmat"] - e["m_t"])
        e["s_inter"] = jnp.exp(e["e_inter"] - e["m_t"])
        e["w_last"] = jnp.exp(e["b_last"] - e["b_col"] + e["i_col"] - e["m_new"])
    qk = {(ci, h): qk0[ci][h] * g[ci, h]["w_intra"] for ci, h in probs}
    kw = {(ci, h): k[rows[ci], sls[h]] * g[ci, h]["w_last"] for ci, h in probs}
    qkv = {(ci, h): _dg(qk[ci, h].astype(BF16), vb[rows[ci], sls[h]], _NN) for ci, h in probs}
    cu = {(ci, h): _dg(kw[ci, h].astype(BF16), vb[rows[ci], sls[h]], _TN) for ci, h in probs}
    c_in, n_in = {}, {}
    for ci, h in probs:
        c_in[ci, h], n_in[ci, h] = c_st[h], n_st[h]
        c_st[h] = g[ci, h]["d_prev"] * c_st[h] + cu[ci, h]
        n_st[h] = g[ci, h]["d_prev"] * n_st[h] + jnp.sum(kw[ci, h], axis=0, keepdims=True)
    qc = {(ci, h): _dg(qb[rows[ci], sls[h]], c_in[ci, h].astype(BF16), _NN) for ci, h in probs}
    for ci, h in probs:
        e, rs, sl = g[ci, h], rows[ci], sls[h]
        num = e["s_inter"] * qc[ci, h] + qkv[ci, h]
        den = (e["s_inter"] * jnp.sum(q[rs, sl] * n_in[ci, h], axis=-1, keepdims=True)
               + jnp.sum(qk[ci, h], axis=-1, keepdims=True))
        hh = num / jnp.maximum(jnp.abs(den), jnp.exp(-e["m_t"]))
        ms = jnp.sum(hh * hh, axis=-1, keepdims=True) * (1.0 / D_HEAD)
        out = hh * lax.rsqrt(ms + RMS_EPS) * gain_ref[:, sl] * _sigmoid(o_ref[rs, sl].astype(F32))
        y_ref[rs, sl] = out.astype(y_ref.dtype)

    for h in heads:
        c_ref[h] = c_st[h]
        n_ref[h] = n_st[h]
        m_ref[h] = m_st[h]


def _mlstm(u, conv_q, conv_k, gate_bias, gain, *, batch, seq):
    L = MLSTM_TILE
    nt = seq // L
    W = D_WIDTH_PAD
    col = lambda j: pl.BlockSpec((L, W), lambda b, c, j=j: (b * nt + c, j))
    gcol = pl.BlockSpec((L, D_GATE_PAD), lambda b, c: (b * nt + c, (4 * W + C_WIDTH) // D_GATE_PAD))
    full = lambda a: pl.BlockSpec(a.shape, lambda b, c: (0,) * a.ndim)
    return pl.pallas_call(
        _mlstm_kernel,
        grid=(batch, nt),
        in_specs=[col(0), col(1), col(2), col(3), gcol, full(conv_q), full(conv_k),
                  full(gate_bias), full(gain)],
        out_specs=pl.BlockSpec((L, W), lambda b, c: (b * nt + c, 0)),
        out_shape=jax.ShapeDtypeStruct((batch * seq, W), BF16),
        scratch_shapes=[pltpu.VMEM((D_HEADS, D_HEAD_PAD, D_HEAD_PAD), F32),
                        pltpu.VMEM((D_HEADS, 1, D_HEAD_PAD), F32),
                        pltpu.VMEM((D_HEADS, 1, 1), F32),
                        pltpu.VMEM((16, W), BF16), pltpu.VMEM((16, W), BF16)],
        compiler_params=_cparams(("parallel", "arbitrary")),
        name="mlstm_mix",
    )(u, u, u, u, u, conv_q, conv_k, gate_bias, gain)


def _pad_cols(a, n):
    return jnp.pad(a, [(0, 0)] * (a.ndim - 1) + [(0, n - a.shape[-1])])


def _pad_heads(a, axis):
    zshape = a.shape[:axis] + (D_HEAD_PAD - D_HEAD,) + a.shape[axis + 1:]
    pieces = []
    for h in range(D_HEADS):
        pieces += [lax.slice_in_dim(a, h * D_HEAD, (h + 1) * D_HEAD, axis=axis), jnp.zeros(zshape, a.dtype)]
    return jnp.concatenate(pieces, axis=axis)


def _row(v):
    return v.reshape(1, -1).astype(F32)


def kernel(x, norm_mix, norm_ffn, norm_final, w_in_even, w_out_even, lb_table, a_norm, b_mu, b_w0, b_w2, b_a0, b_a2, b_g2, b_kk, b_ka, b_rk, b_ln_w, b_ln_b, w_in_odd, w_out_odd, c_lam_re, c_lam_im, c_log_step, c_b_re, c_b_im, c_c_re, c_c_im, c_d, c_glu_w, c_glu_b, d_conv_q, d_conv_k, d_i_bias, d_f_bias, d_norm, ffn_gate, ffn_up, ffn_down):
    batch, seq, d = x.shape
    m = batch * seq
    xf = x.reshape(m, d).astype(F32)
    lower_bounds = jnp.cumsum(jax.nn.softmax(lb_table.astype(F32), axis=0), axis=0)

    w_in = _pad_cols(w_in_even[0], EVEN_IN_PAD).astype(BF16)
    u = _norm_matmul(xf, _row(norm_mix[0]), w_in, tm=512, tn=3 * MXU_WIDTH)
    ya = _hgrn2(u, _row(lower_bounds[0]), _row(a_norm[0]), batch=batch, seq=seq)

    mu = b_mu[0].astype(F32)
    lora_rows = lambda w, off: jnp.pad(w.astype(F32), ((off, B_TAIL - off - w.shape[0]), (0, 0)))
    rw = {
        "mu_r": _row(mu[:B_WIDTH]), "mu_k": _row(mu[B_WIDTH:2 * B_WIDTH]),
        "mu_v": _row(mu[2 * B_WIDTH:3 * B_WIDTH]), "mu_t": _row(_pad_cols(mu[3 * B_WIDTH:], B_TAIL)),
        "w0": _row(b_w0[0]), "w2": lora_rows(b_w2[0], 0),
        "a0": _row(b_a0[0]), "a2": lora_rows(b_a2[0], B_DECAY_LORA),
        "g2": lora_rows(b_g2[0], B_DECAY_LORA + B_AAA_LORA).astype(BF16),
        "kk": _row(b_kk[0]), "ka": _row(b_ka[0]), "rk": _row(b_rk[0]),
        "lnw": _row(b_ln_w[0]), "lnb": _row(b_ln_b[0]),
    }
    yb = _rwkv7(u, rw, batch=batch, seq=seq)
    wo = w_out_even[0].astype(BF16)
    xf = _out_proj_ffn(ya, yb, wo[:A_WIDTH], wo[A_WIDTH:], xf, _row(norm_ffn[0]),
                       ffn_gate[0].astype(BF16), ffn_up[0].astype(BF16), ffn_down[0].astype(BF16),
                       _row(norm_final), tm=512, th=FFN_HIDDEN, final_norm=False)

    wi = w_in_odd[0].astype(BF16)
    seg = lambda j: _pad_heads(wi[:, C_WIDTH + j * D_WIDTH:C_WIDTH + (j + 1) * D_WIDTH], 1)
    w_in = jnp.concatenate(
        [seg(0), seg(1), seg(2), seg(3), wi[:, :C_WIDTH],
         _pad_cols(wi[:, C_WIDTH + 4 * D_WIDTH:], D_GATE_PAD)], axis=1)
    u = _norm_matmul(xf, _row(norm_mix[1]), w_in, tm=512, tn=4 * MXU_WIDTH)

    u2 = u[:, 4 * D_WIDTH_PAD:4 * D_WIDTH_PAD + C_WIDTH].reshape(m // S5_CHUNK, S5_CHUNK * C_WIDTH)
    s5p = _s5_params(c_lam_re[0].astype(F32), c_lam_im[0].astype(F32), c_log_step[0].astype(F32),
                     c_b_re[0].astype(F32), c_b_im[0].astype(F32), c_c_re[0].astype(F32),
                     c_c_im[0].astype(F32))
    tr = min(1024, m // S5_CHUNK)
    z = _s5_increments(u2, s5p["w"], tm=tr, tn=512)
    hp = _s5_carry(z, s5p["lr"], s5p["li"], batch=batch)
    yc = _s5_output(u2, hp, s5p["bd"], s5p["m"], _row(c_d[0]), c_glu_w[0].astype(BF16),
                    _row(c_glu_b[0]), tm=tr).reshape(m, C_WIDTH)

    gate_bias = _row(_pad_cols(jnp.concatenate([d_i_bias[0], d_f_bias[0]]).astype(F32), D_GATE_PAD))
    yd = _mlstm(u, _pad_heads(d_conv_q[0].astype(F32), 1), _pad_heads(d_conv_k[0].astype(F32), 1),
                gate_bias, _row(_pad_heads(d_norm[0].astype(F32), 0)), batch=batch, seq=seq)
    wo = w_out_odd[0].astype(BF16)
    xf = _out_proj_ffn(yc, yd, wo[:C_WIDTH], _pad_heads(wo[C_WIDTH:], 0), xf,
                       _row(norm_ffn[1]), ffn_gate[1].astype(BF16), ffn_up[1].astype(BF16),
                       ffn_down[1].astype(BF16), _row(norm_final), tm=512, th=FFN_HIDDEN, final_norm=True)
    return xf.reshape(batch, seq, d).astype(x.dtype)
```

```python
import functools
import math

import jax
import jax.numpy as jnp
from jax import lax
from jax.experimental import pallas as pl
from jax.experimental.pallas import tpu as pltpu

F32 = jnp.float32
BF16 = jnp.bfloat16

RMS_EPS = 1e-6
GN_EPS = 64e-5
M_INIT = -1e30

D_MODEL = 1024
A_WIDTH, A_HEADS, A_KEY = 512, 4, 128
B_WIDTH, B_HEADS, B_HEAD = 512, 8, 64
B_DECAY_LORA, B_AAA_LORA, B_GATE_LORA = 32, 32, 96
B_TAIL = 256
EVEN_IN_PAD = 4 * A_WIDTH + 3 * B_WIDTH + B_TAIL
C_WIDTH, C_GROUP, C_GROUPS, C_STATE = 256, 16, 16, 64
D_HEADS, D_HEAD, D_HEAD_PAD = 4, 192, 256
D_WIDTH = D_HEADS * D_HEAD
D_WIDTH_PAD = D_HEADS * D_HEAD_PAD
D_CONV = 4
D_GATE_PAD = 128
ODD_IN_PAD = 4 * D_WIDTH_PAD + C_WIDTH + D_GATE_PAD
FFN_HIDDEN = 2816

HGRN_TILE = 128
HGRN_SUB = 16
RWKV_CHUNK = 64
RWKV_TILE = 256
RWKV_BATCH = 2
S5_CHUNK = 16
MLSTM_CHUNK = 64
MLSTM_TILE = 256
MLSTM_BATCH = 1

VMEM_LIMIT = 56 * 1024 * 1024


def _cparams(sem):
    return pltpu.CompilerParams(dimension_semantics=sem, vmem_limit_bytes=VMEM_LIMIT)


def _split2(x):
    hi = x.astype(BF16)
    lo = (x - hi.astype(F32)).astype(BF16)
    return hi, lo


def _split3(x):
    hi = x.astype(BF16)
    r1 = x - hi.astype(F32)
    mid = r1.astype(BF16)
    lo = (r1 - mid.astype(F32)).astype(BF16)
    return hi, mid, lo


def _dg(a, b, dims):
    return lax.dot_general(a, b, (dims, ((), ())), preferred_element_type=F32)


_NN = ((1,), (0,))
_NT = ((1,), (1,))
_TN = ((0,), (0,))


def _mm1(a, b, dims=_NN):
    return _dg(a.astype(BF16), b.astype(BF16), dims)


def _mm3(a, b, dims=_NN):
    ah, al = _split2(a)
    bh, bl = _split2(b)
    return _dg(ah, bh, dims) + (_dg(ah, bl, dims) + _dg(al, bh, dims))


def _mm2(a, b, dims=_NN):
    ah, al = _split2(a)
    bh = b.astype(BF16)
    return _dg(ah, bh, dims) + _dg(al, bh, dims)


_mmi = _mm1


def _mm_sel(sel_bf16, x, dims=_NN):
    h, m, l = _split3(x)
    return _dg(sel_bf16, h, dims) + (_dg(sel_bf16, m, dims) + _dg(sel_bf16, l, dims))


def _x_sel(x, sel_bf16):
    n = x.shape[0]
    r = _dg(jnp.concatenate(_split3(x), axis=0), sel_bf16, _NN)
    return r[:n] + (r[n:2 * n] + r[2 * n:])


def _sigmoid(x):
    return 1.0 / (1.0 + jnp.exp(-x))


def _silu(x):
    return x * _sigmoid(x)


def _softplus(x):
    return jnp.maximum(x, 0.0) + jnp.log(1.0 + jnp.exp(-jnp.abs(x)))


def _gelu_tanh(x):
    c = math.sqrt(2.0 / math.pi)
    return 0.5 * x * (1.0 + jnp.tanh(c * (x + 0.044715 * (x * x * x))))


def _iota(shape, dim):
    return lax.broadcasted_iota(jnp.int32, shape, dim)


def _tril_mask(n, strict=False):
    r = _iota((n, n), 0)
    c = _iota((n, n), 1)
    return (r > c) if strict else (r >= c)


MXU_WIDTH = 256


def _col_chunks(n, target):
    bounds = list(range(0, n, target)) + [n]
    return list(zip(bounds[:-1], bounds[1:]))


def _resident(shape):
    return pl.BlockSpec(shape, lambda i: (0,) * len(shape))


def _norm_matmul_kernel(x_ref, g_ref, w_ref, o_ref, *, chunks):
    x = x_ref[...]
    ms = jnp.mean(x * x, axis=-1, keepdims=True)
    h = (x * lax.rsqrt(ms + RMS_EPS) * g_ref[...]).astype(BF16)
    for lo, hi in chunks:
        o_ref[:, lo:hi] = jnp.dot(h, w_ref[:, lo:hi], preferred_element_type=F32).astype(o_ref.dtype)


def _norm_matmul(x, g, w, *, tm, tn):
    m, d = x.shape
    n = w.shape[1]
    return pl.pallas_call(
        functools.partial(_norm_matmul_kernel, chunks=_col_chunks(n, tn)),
        grid=(m // tm,),
        in_specs=[pl.BlockSpec((tm, d), lambda i: (i, 0)), _resident((1, d)), _resident((d, n))],
        out_specs=pl.BlockSpec((tm, n), lambda i: (i, 0)),
        out_shape=jax.ShapeDtypeStruct((m, n), BF16),
        compiler_params=_cparams(("parallel",)),
        name="norm_in_proj",
    )(x, g, w)


def _ffn_kernel(a_ref, b_ref, wa_ref, wb_ref, x_ref, g_ref, wg_ref, wu_ref, wd_ref, gf_ref, o_ref,
                *, chunks, final_norm):
    x = x_ref[...] + jnp.dot(a_ref[...], wa_ref[...], preferred_element_type=F32)
    x = x + jnp.dot(b_ref[...], wb_ref[...], preferred_element_type=F32)
    ms = jnp.mean(x * x, axis=-1, keepdims=True)
    h = (x * lax.rsqrt(ms + RMS_EPS) * g_ref[...]).astype(BF16)
    y = x
    for lo, hi in chunks:
        gate = jnp.dot(h, wg_ref[:, lo:hi], preferred_element_type=F32)
        up = jnp.dot(h, wu_ref[:, lo:hi], preferred_element_type=F32)
        act = (_silu(gate) * up).astype(BF16)
        y = y + jnp.dot(act, wd_ref[lo:hi, :], preferred_element_type=F32)
    if final_norm:
        ms = jnp.mean(y * y, axis=-1, keepdims=True)
        y = y * lax.rsqrt(ms + RMS_EPS) * gf_ref[...]
    o_ref[...] = y


def _out_proj_ffn(a, b, wa, wb, x, g, wg, wu, wd, gf, *, tm, th, final_norm):
    m, d = x.shape
    hid = wg.shape[1]
    ka, kb = a.shape[1], b.shape[1]
    return pl.pallas_call(
        functools.partial(_ffn_kernel, chunks=_col_chunks(hid, th), final_norm=final_norm),
        grid=(m // tm,),
        in_specs=[
            pl.BlockSpec((tm, ka), lambda i: (i, 0)),
            pl.BlockSpec((tm, kb), lambda i: (i, 0)),
            _resident((ka, d)), _resident((kb, d)),
            pl.BlockSpec((tm, d), lambda i: (i, 0)),
            _resident((1, d)), _resident((d, hid)), _resident((d, hid)), _resident((hid, d)),
            _resident((1, d)),
        ],
        out_specs=pl.BlockSpec((tm, d), lambda i: (i, 0)),
        out_shape=jax.ShapeDtypeStruct((m, d), F32),
        compiler_params=_cparams(("parallel",)),
        name="out_proj_swiglu_ffn",
    )(a, b, wa, wb, x, g, wg, wu, wd, gf)


def _hgrn2_kernel(q_ref, f_ref, i_ref, g_ref, lb_ref, gain_ref, o_ref, s_ref):
    T, SUB = HGRN_TILE, HGRN_SUB
    nb = T // SUB

    @pl.when(pl.program_id(1) == 0)
    def _():
        s_ref[...] = jnp.zeros_like(s_ref)

    r = _iota((T, T), 0)
    c = _iota((T, T), 1)
    tri = jnp.where((r >= c) & ((r // SUB) == (c // SUB)), 1.0, 0.0).astype(BF16)
    HALF = SUB // 2
    rowh = _iota((nb, HALF, A_KEY), 1)
    heads = range(A_HEADS)
    sls = [slice(h * A_KEY, (h + 1) * A_KEY) for h in heads]
    intra, qts, kts, v3s, decs = [], [], [], [], []

    for h in heads:
        sl = sls[h]
        q = _silu(q_ref[:, sl].astype(F32))
        lb = lb_ref[:, sl]
        f = lb + (1.0 - lb) * _sigmoid(f_ref[:, sl].astype(F32))
        lf = jnp.log(f)
        k = 1.0 - f
        v = i_ref[:, sl].astype(F32)
        b = _mm_sel(tri, lf)
        b3 = b.reshape(nb, SUB, A_KEY)
        q3 = q.reshape(nb, SUB, A_KEY)
        k3 = k.reshape(nb, SUB, A_KEY)
        v3 = v.reshape(nb, SUB, A_KEY)
        log2e = 1.0 / math.log(2.0)
        b2 = b3 * log2e
        bk3 = b2 - jnp.log2(k3)
        b_up, b_lo = b2[:, :HALF], b2[:, HALF:]
        q_up, q_lo = q3[:, :HALF], q3[:, HALF:]
        acc_up = jnp.zeros((nb, HALF, A_KEY), F32)
        acc_lo = jnp.zeros((nb, HALF, A_KEY), F32)
        for s in range(HALF):
            bs = bk3[:, s:s + 1, :]
            vs = v3[:, s:s + 1, :]
            e_up = jnp.exp2(jnp.where(rowh >= s, b_up - bs, -jnp.inf))
            acc_up = acc_up + jnp.sum(q_up * e_up, axis=-1, keepdims=True) * vs
            acc_lo = acc_lo + jnp.sum(q_lo * jnp.exp2(b_lo - bs), axis=-1, keepdims=True) * vs
        for s in range(HALF):
            bs = bk3[:, HALF + s:HALF + s + 1, :]
            e_lo = jnp.exp2(jnp.where(rowh >= s, b_lo - bs, -jnp.inf))
            acc_lo = acc_lo + jnp.sum(q_lo * e_lo, axis=-1, keepdims=True) * v3[:, HALF + s:HALF + s + 1, :]
        intra.append(jnp.concatenate([acc_up, acc_lo], axis=1).reshape(T, A_KEY))

        bend = b3[:, SUB - 1:SUB, :]
        qts.append((q3 * jnp.exp(b3)).astype(BF16))
        kts.append((k3 * jnp.exp(bend - b3)).astype(BF16))
        v3s.append(v3.astype(BF16))
        decs.append(jnp.exp(bend))

    inc = [[_dg(v3s[h][i], kts[h][i], _TN) for i in range(nb)] for h in heads]
    for h in heads:
        st = s_ref[h]
        outs = []
        for i in range(nb):
            outs.append(_dg(qts[h][i], st.astype(BF16), _NT))
            st = st * decs[h][i] + inc[h][i]
        s_ref[h] = st
        o = intra[h] + jnp.concatenate(outs, axis=0)
        o = o * lax.rsqrt(jnp.mean(o * o, axis=-1, keepdims=True) + RMS_EPS) * gain_ref[:, sls[h]]
        o_ref[:, sls[h]] = (o * _silu(g_ref[:, sls[h]].astype(F32))).astype(o_ref.dtype)


def _hgrn2(u, lb, gain, *, batch, seq):
    T = HGRN_TILE
    nt = seq // T
    col = lambda j: pl.BlockSpec((T, A_WIDTH), lambda b, c, j=j: (b * nt + c, j))
    vec = pl.BlockSpec((1, A_WIDTH), lambda b, c: (0, 0))
    return pl.pallas_call(
        _hgrn2_kernel,
        grid=(batch, nt),
        in_specs=[col(0), col(1), col(2), col(3), vec, vec],
        out_specs=pl.BlockSpec((T, A_WIDTH), lambda b, c: (b * nt + c, 0)),
        out_shape=jax.ShapeDtypeStruct((batch * seq, A_WIDTH), BF16),
        scratch_shapes=[pltpu.VMEM((A_HEADS, A_KEY, A_KEY), F32)],
        compiler_params=_cparams(("parallel", "arbitrary")),
        name="hgrn2_mix",
    )(u, u, u, u, lb, gain)


def _rwkv7_kernel(r_ref, k_ref, v_ref, t_ref, mu_r, mu_k, mu_v, mu_t, w0_ref, w2_ref, a0_ref,
                  a2_ref, g2_ref, kk_ref, ka_ref, rk_ref, lnw_ref, lnb_ref, o_ref,
                  s_ref, cr_ref, ck_ref, cv_ref, ct_ref):
    L = RWKV_CHUNK
    NB = RWKV_BATCH
    TB = RWKV_TILE
    T = NB * TB
    N = B_HEAD
    nc = TB // L

    @pl.when(pl.program_id(1) == 0)
    def _():
        s_ref[...] = jnp.zeros_like(s_ref)
        cr_ref[...] = jnp.zeros_like(cr_ref)
        ck_ref[...] = jnp.zeros_like(ck_ref)
        cv_ref[...] = jnp.zeros_like(cv_ref)
        ct_ref[...] = jnp.zeros_like(ct_ref)

    def shift_mix(x_ref, carry_ref, mu_ref):
        outs = []
        for b in range(NB):
            x = x_ref[b].astype(F32)
            row = _iota(x.shape, 0)
            prev = jnp.where(row == 0, carry_ref[b, 7:8, :], pltpu.roll(x, 1, 0))
            carry_ref[b] = x[TB - 8:TB, :]
            outs.append(x + (prev - x) * mu_ref[...])
        return jnp.concatenate(outs, axis=0)

    r = shift_mix(r_ref, cr_ref, mu_r)
    k = shift_mix(k_ref, ck_ref, mu_k)
    v = shift_mix(v_ref, cv_ref, mu_v)
    tl = shift_mix(t_ref, ct_ref, mu_t)

    wl = _mm1(jnp.tanh(tl), w2_ref[...])
    al = _mm1(tl, a2_ref[...])
    gate = _mm1(_sigmoid(tl), g2_ref[...])
    w_log = -_softplus(-(w0_ref[...] + wl)) - 0.5
    lw = -jnp.exp(w_log)
    a_lr = _sigmoid(a0_ref[...] + al)

    ones_bd = jnp.where(_iota((MXU_WIDTH, MXU_WIDTH), 0) // N == _iota((MXU_WIDTH, MXU_WIDTH), 1) // N,
                        1.0, 0.0).astype(BF16)

    def head_sums(x):
        n = x.shape[0]
        parts = jnp.concatenate(_split2(x), axis=0)
        r = jnp.concatenate([_dg(parts[:, g * MXU_WIDTH:(g + 1) * MXU_WIDTH], ones_bd, _NN)
                             for g in range(B_WIDTH // MXU_WIDTH)], axis=-1)
        return r[:n] + r[n:]

    kk = k * kk_ref[...]
    k2 = k * (1.0 + (a_lr - 1.0) * ka_ref[...])
    sums = head_sums(jnp.concatenate([kk * kk, r * k2 * rk_ref[...]], axis=0))
    kk = kk / jnp.maximum(jnp.sqrt(sums[:T]), 1e-12)
    bonus = sums[T:]
    aa = -kk
    bb = kk * a_lr

    rr = _iota((TB, TB), 0)
    cc = _iota((TB, TB), 1)
    tri = jnp.where((rr >= cc) & (rr // L == cc // L), 1.0, 0.0).astype(BF16)
    c = jnp.concatenate([_mm_sel(tri, lw[b * TB:(b + 1) * TB, :]) for b in range(NB)], axis=0)
    enc = jnp.exp(-c)
    rt = (r * jnp.exp(c)).astype(BF16)
    at = (aa * jnp.exp(c - lw)).astype(BF16)
    bt = (bb * enc).astype(BF16)
    kt = (k2 * enc).astype(BF16)
    vb = v.astype(BF16)

    PW = 2 * N
    pairs = range(B_HEADS // 2)
    pc = lambda p: slice(p * PW, (p + 1) * PW)
    r2 = _iota((2 * L, PW), 0)
    c2 = _iota((2 * L, PW), 1) % L
    both = ((r2 < L) & (r2 > c2)) | (r2 - L >= c2)
    left1 = _iota((L, PW), 1) < N
    left2 = (_iota((L, 2 * PW), 1) % PW) < N
    same_head = (_iota((PW, PW), 0) // N) == (_iota((PW, PW), 1) // N)

    def bdiag(z, left):
        zero = jnp.zeros_like(z)
        return jnp.concatenate([jnp.where(left, z, zero), jnp.where(left, zero, z)], axis=0)

    solved = {}
    NP = B_HEADS // 2
    state = {"s": [[s_ref[b * NP + p] for p in pairs] for b in range(NB)], "y": {}}

    def solve_stages(ci):
        rs = slice(ci * L, (ci + 1) * L)
        ar = [jnp.concatenate([at[rs, pc(p)], rt[rs, pc(p)]], axis=0) for p in pairs]
        vd = [bdiag(vb[rs, pc(p)], left1) for p in pairs]
        m_b = [jnp.where(both, _dg(ar[p], bdiag(bt[rs, pc(p)], left1), _NT), 0.0) for p in pairs]
        yield
        m_k = [jnp.where(both, _dg(ar[p], bdiag(kt[rs, pc(p)], left1), _NT), 0.0).astype(BF16) for p in pairs]
        yield
        mkv = [_dg(m_k[p], vd[p], _NN) for p in pairs]
        yield
        pw = [m_b[p][:L] for p in pairs]
        x = [jnp.concatenate([ar[p][:L].astype(F32), mkv[p][:L]], axis=-1) for p in pairs]
        x = [x[p] + _dg(pw[p].astype(BF16), bdiag(x[p].astype(BF16), left2), _NN) for p in pairs]
        yield
        for _ in range(int(math.log2(L)) - 1):
            pw = [_dg(pw[p].astype(BF16), bdiag(pw[p].astype(BF16), left1), _NN) for p in pairs]
            yield
            x = [x[p] + _dg(pw[p].astype(BF16), bdiag(x[p].astype(BF16), left2), _NN) for p in pairs]
            yield
        c_last = c[(ci + 1) * L - 1:(ci + 1) * L, :]
        e_tail = jnp.exp(c_last - c[rs, :])
        solved[ci] = dict(
            wr=[jnp.concatenate([x[p][:, :PW].astype(BF16), ar[p][L:]], axis=0) for p in pairs],
            u0=[x[p][:, PW:] for p in pairs],
            a_rb=[m_b[p][L:].astype(BF16) for p in pairs],
            y0=[mkv[p][L:] for p in pairs], vh=[vb[rs, pc(p)] for p in pairs], g_last=jnp.exp(c_last),
            bkl=jnp.concatenate([(bb[rs, :] * e_tail).astype(BF16),
                                 (k2[rs, :] * e_tail).astype(BF16)], axis=0))
        yield

    def state_stages(ci):
        q = solved.pop(ci)
        st = state["s"][ci // nc]
        ws = [_dg(q["wr"][p], st[p].astype(BF16), _NT) for p in pairs]
        yield
        ub = [(ws[p][:L] + q["u0"][p]).astype(BF16) for p in pairs]
        state["y"][ci] = jnp.concatenate(
            [ws[p][L:] + _dg(q["a_rb"][p], bdiag(ub[p], left1), _NN) + q["y0"][p] for p in pairs], axis=-1)
        upd = [_dg(jnp.concatenate([ub[p], q["vh"][p]], axis=0), q["bkl"][:, pc(p)], _TN) for p in pairs]
        state["s"][ci // nc] = [st[p] * q["g_last"][:, pc(p)] + jnp.where(same_head, upd[p], 0.0)
                                for p in pairs]
        yield

    for _ in zip(*[solve_stages(ci) for ci in range(NB * nc)]):
        pass
    for ci in range(nc):
        for _ in zip(*[state_stages(b * nc + ci) for b in range(NB)]):
            pass
    for b in range(NB):
        for p in pairs:
            s_ref[b * NP + p] = state["s"][b][p]
    ys = [state["y"][ci] for ci in range(NB * nc)]

    y = jnp.concatenate(ys, axis=0)
    mom = head_sums(jnp.concatenate([y, y * y], axis=0)) * (1.0 / N)
    mean = mom[:T]
    var = jnp.maximum(mom[T:] - mean * mean, 0.0)
    y = (y - mean) * lax.rsqrt(var + GN_EPS) * lnw_ref[...] + lnb_ref[...]
    y = y + bonus * v
    out = (y * gate).astype(o_ref.dtype)
    for b in range(NB):
        o_ref[b] = out[b * TB:(b + 1) * TB, :]


def _rwkv7(u, p, *, batch, seq):
    T = RWKV_TILE
    NB = RWKV_BATCH
    nt = seq // T
    u = u.reshape(batch, seq, u.shape[-1])
    col = lambda j: pl.BlockSpec((NB, T, B_WIDTH), lambda b, c, j=j: (b, c, j))
    tail = pl.BlockSpec((NB, T, B_TAIL), lambda b, c: (b, c, (4 * A_WIDTH + 3 * B_WIDTH) // B_TAIL))
    full = lambda a: pl.BlockSpec(a.shape, lambda b, c: (0,) * a.ndim)
    params = [p["mu_r"], p["mu_k"], p["mu_v"], p["mu_t"], p["w0"], p["w2"], p["a0"], p["a2"],
              p["g2"], p["kk"], p["ka"], p["rk"], p["lnw"], p["lnb"]]
    return pl.pallas_call(
        _rwkv7_kernel,
        grid=(batch // NB, nt),
        in_specs=[col(4), col(5), col(6), tail] + [full(a) for a in params],
        out_specs=pl.BlockSpec((NB, T, B_WIDTH), lambda b, c: (b, c, 0)),
        out_shape=jax.ShapeDtypeStruct((batch, seq, B_WIDTH), BF16),
        scratch_shapes=[pltpu.VMEM((NB * (B_HEADS // 2), 2 * B_HEAD, 2 * B_HEAD), F32),
                        pltpu.VMEM((NB, 8, B_WIDTH), F32), pltpu.VMEM((NB, 8, B_WIDTH), F32),
                        pltpu.VMEM((NB, 8, B_WIDTH), F32), pltpu.VMEM((NB, 8, B_TAIL), F32)],
        compiler_params=_cparams(("parallel", "arbitrary")),
        name="rwkv7_mix",
    )(u, u, u, u, *params).reshape(batch * seq, B_WIDTH)


def _s5_inc_kernel(u_ref, w_ref, z_ref):
    z_ref[...] = jnp.dot(u_ref[...], w_ref[...], preferred_element_type=F32)


def _s5_increments(u2, w, *, tm, tn):
    n, kdim = u2.shape
    ncol = w.shape[1]
    return pl.pallas_call(
        _s5_inc_kernel,
        grid=(n // tm, ncol // tn),
        in_specs=[pl.BlockSpec((tm, kdim), lambda i, j: (i, 0)),
                  pl.BlockSpec((kdim, tn), lambda i, j: (0, j))],
        out_specs=pl.BlockSpec((tm, tn), lambda i, j: (i, j)),
        out_shape=jax.ShapeDtypeStruct((n, ncol), F32),
        compiler_params=_cparams(("parallel", "parallel")),
        name="s5_increments",
    )(u2, w)


def _s5_carry_kernel(z_ref, lr_ref, li_ref, hp_ref, h_ref):
    n = z_ref.shape[0]
    half = z_ref.shape[1] // 2
    lr = lr_ref[...]
    li = li_ref[...]

    def body(i, carry):
        hr, hi = carry
        rows = pl.ds(pl.multiple_of(i * 8, 8), 8)
        zz = z_ref[rows, :]
        outs = []
        for j in range(8):
            outs.append(jnp.concatenate([hr, hi], axis=-1))
            hr, hi = (lr * hr - li * hi + zz[j:j + 1, :half],
                      lr * hi + li * hr + zz[j:j + 1, half:])
        h_ref[rows, :] = jnp.concatenate(outs, axis=0)
        return hr, hi

    zero = jnp.zeros((1, half), F32)
    lax.fori_loop(0, n // 8, body, (zero, zero))
    hp_ref[...] = h_ref[...].astype(hp_ref.dtype)


def _s5_carry(z, lr, li, *, batch):
    n, w = z.shape
    rows = n // batch
    vec = pl.BlockSpec((1, w // 2), lambda b: (0, 0))
    return pl.pallas_call(
        _s5_carry_kernel,
        grid=(batch,),
        in_specs=[pl.BlockSpec((rows, w), lambda b: (b, 0)), vec, vec],
        out_specs=pl.BlockSpec((rows, w), lambda b: (b, 0)),
        out_shape=jax.ShapeDtypeStruct((n, w), BF16),
        scratch_shapes=[pltpu.VMEM((rows, w), F32)],
        compiler_params=_cparams(("parallel",)),
        name="s5_carry",
    )(z, lr, li)


def _s5_out_kernel(u_ref, uj_ref, hp_ref, bd_ref, m_ref, d_ref, gw_ref, gb_ref, o_ref, acc_ref):
    t = pl.program_id(1)
    acc_ref[...] = jnp.dot(hp_ref[...], m_ref[...], preferred_element_type=F32)
    for s in range(S5_CHUNK):
        @pl.when(s <= t)
        def _():
            acc_ref[...] += jnp.dot(u_ref[:, s * C_WIDTH:(s + 1) * C_WIDTH], bd_ref[t - s],
                                    preferred_element_type=F32)
    z = _gelu_tanh(acc_ref[...] + d_ref[...] * uj_ref[...].astype(F32))
    gate = jnp.dot(z.astype(BF16), gw_ref[...], preferred_element_type=F32) + gb_ref[...]
    o_ref[...] = (z * _sigmoid(gate)).astype(o_ref.dtype)


def _s5_output(u2, hp, bd, mmat, d, gw, gb, *, tm):
    n, kdim = u2.shape
    kst = hp.shape[1]
    vec = pl.BlockSpec((1, C_WIDTH), lambda i, t: (0, 0))
    return pl.pallas_call(
        _s5_out_kernel,
        grid=(n // tm, S5_CHUNK),
        in_specs=[pl.BlockSpec((tm, kdim), lambda i, t: (i, 0)),
                  pl.BlockSpec((tm, C_WIDTH), lambda i, t: (i, t)),
                  pl.BlockSpec((tm, kst), lambda i, t: (i, 0)),
                  pl.BlockSpec(bd.shape, lambda i, t: (0, 0, 0)),
                  pl.BlockSpec((kst, C_WIDTH), lambda i, t: (0, t)),
                  vec, pl.BlockSpec((C_WIDTH, C_WIDTH), lambda i, t: (0, 0)), vec],
        out_specs=pl.BlockSpec((tm, C_WIDTH), lambda i, t: (i, t)),
        out_shape=jax.ShapeDtypeStruct((n, kdim), BF16),
        scratch_shapes=[pltpu.VMEM((tm, C_WIDTH), F32)],
        compiler_params=_cparams(("parallel", "arbitrary")),
        name="s5_output",
    )(u2, u2, hp, bd, mmat, d, gw, gb)


def _s5_params(lam_re, lam_im, log_step, b_re, b_im, c_re, c_im):
    L = S5_CHUNK
    step = jnp.exp(log_step)[:, None]
    mag = jnp.exp(lam_re * step)
    lr, li = mag * jnp.cos(lam_im * step), mag * jnp.sin(lam_im * step)
    den = lam_re * lam_re + lam_im * lam_im
    nr, ni = lr - 1.0, li
    coef_re = (nr * lam_re + ni * lam_im) / den
    coef_im = (ni * lam_re - nr * lam_im) / den
    bb_re = coef_re[..., None] * b_re - coef_im[..., None] * b_im
    bb_im = coef_re[..., None] * b_im + coef_im[..., None] * b_re
    pr, pi = [jnp.ones_like(lr)], [jnp.zeros_like(lr)]
    for _ in range(L):
        pr.append(pr[-1] * lr - pi[-1] * li)
        pi.append(pr[-2] * li + pi[-1] * lr)
    pr, pi = jnp.stack(pr), jnp.stack(pi)
    hp = lax.Precision.HIGHEST
    zr = pr[:L, :, :, None] * bb_re[None] - pi[:L, :, :, None] * bb_im[None]
    zi = pr[:L, :, :, None] * bb_im[None] + pi[:L, :, :, None] * bb_re[None]
    kern = (jnp.einsum("gop,lgpi->lgoi", c_re, zr, precision=hp)
            - jnp.einsum("gop,lgpi->lgoi", c_im, zi, precision=hp))
    nin = L * C_WIDTH
    nst = 2 * C_GROUPS * C_STATE

    def spread(a, row_group, n_inner):
        rows, cols = a.shape
        wide = cols * C_GROUPS
        cw = jnp.arange(wide)
        src = (cw // (C_GROUPS * n_inner)) * n_inner + cw % n_inner
        rep = (jnp.arange(cols)[:, None] == src[None, :]).astype(BF16)
        out = jnp.dot(a.astype(BF16), rep, preferred_element_type=F32)
        keep = row_group[:, None] == ((cw // n_inner) % C_GROUPS)[None, :]
        return jnp.where(keep, out, 0.0).astype(BF16)

    ka = kern.transpose(0, 1, 3, 2).reshape(L * C_WIDTH, C_GROUP)
    bd = spread(ka, (jnp.arange(L * C_WIDTH) // C_GROUP) % C_GROUPS, C_GROUP).reshape(L, C_WIDTH, C_WIDTH)
    wz = jnp.stack([zr[::-1], zi[::-1]])
    wa = wz.transpose(1, 2, 4, 0, 3).reshape(nin, 2 * C_STATE)
    w = spread(wa, (jnp.arange(nin) // C_GROUP) % C_GROUPS, C_STATE)
    qr, qi = pr[1:], pi[1:]
    m_re = c_re[None] * qr[:, :, None, :] - c_im[None] * qi[:, :, None, :]
    m_im = -(c_re[None] * qi[:, :, None, :] + c_im[None] * qr[:, :, None, :])
    ma = jnp.stack([m_re, m_im]).transpose(0, 2, 4, 1, 3).reshape(nst, L * C_GROUP)
    mmat = spread(ma, (jnp.arange(nst) // C_STATE) % C_GROUPS, C_GROUP)
    return {"w": w, "bd": bd, "m": mmat,
            "lr": pr[L].reshape(1, -1), "li": pi[L].reshape(1, -1)}


def _mlstm_kernel(q_ref, k_ref, v_ref, o_ref, gt_ref, cq_ref, ck_ref, gb_ref, gain_ref, y_ref,
                  c_ref, n_ref, m_ref, pq_ref, pk_ref):
    L = MLSTM_CHUNK
    P = D_HEAD_PAD

    @pl.when(pl.program_id(1) == 0)
    def _():
        c_ref[...] = jnp.zeros_like(c_ref)
        n_ref[...] = jnp.zeros_like(n_ref)
        m_ref[...] = jnp.full(m_ref.shape, M_INIT, F32)
        pq_ref[...] = jnp.zeros_like(pq_ref)
        pk_ref[...] = jnp.zeros_like(pk_ref)

    NB = MLSTM_BATCH
    TB = MLSTM_TILE
    T = NB * TB
    nc = TB // L
    PREV = pq_ref.shape[1]
    LE = L + PREV
    out_row = _iota((L, D_CONV * LE), 0)
    kcol = _iota((L, D_CONV * LE), 1)
    sel = jnp.where(kcol % LE == out_row + kcol // LE - (D_CONV - 1) + PREV, 1.0, 0.0).astype(BF16)

    def conv(x_ref, prev_ref, w_ref):
        outs = []
        for b in range(NB):
            x = x_ref[b]
            for ci in range(nc):
                prev = prev_ref[b] if ci == 0 else x[ci * L - PREV:ci * L, :]
                xe = jnp.concatenate([prev, x[ci * L:(ci + 1) * L, :]], axis=0).astype(F32)
                stack = jnp.concatenate([(xe * w_ref[j:j + 1, :]).astype(BF16) for j in range(D_CONV)], axis=0)
                outs.append(_dg(sel, stack, _NN))
            prev_ref[b] = x[TB - PREV:TB, :]
        return _silu(jnp.concatenate(outs, axis=0))

    q = conv(q_ref, pq_ref, cq_ref)
    k = conv(k_ref, pk_ref, ck_ref) * (1.0 / math.sqrt(D_HEAD))
    vb = jnp.concatenate([v_ref[b] for b in range(NB)], axis=0)

    gates = jnp.concatenate([gt_ref[b] for b in range(NB)], axis=0).astype(F32) + gb_ref[...]
    col = _iota(gates.shape, 1)
    is_f = (col >= D_HEADS) & (col < 2 * D_HEADS)
    gates = jnp.where(is_f, jnp.minimum(gates, 0.0) - jnp.log(1.0 + jnp.exp(-jnp.abs(gates))), gates)
    rr = _iota((TB, TB), 0)
    cc = _iota((TB, TB), 1)
    tri = jnp.where((rr >= cc) & (rr // L == cc // L), 1.0, 0.0).astype(BF16)
    bcum = jnp.concatenate([_mm_sel(tri, gates[b * TB:(b + 1) * TB, :]) for b in range(NB)], axis=0)
    gates_t = gates.T
    bcum_t = bcum.T
    lower = _tril_mask(L)

    heads = range(D_HEADS)
    sls = [slice(h * P, (h + 1) * P) for h in heads]
    qb, kb = q.astype(BF16), k.astype(BF16)
    rows = [slice(ci * L, (ci + 1) * L) for ci in range(NB * nc)]
    qk0 = [[_dg(qb[rs, sls[h]], kb[rs, sls[h]], _NT) for h in heads] for rs in rows]
    sid = lambda ci, h: (ci // nc) * D_HEADS + h
    c_st = [c_ref[i] for i in range(NB * D_HEADS)]
    n_st = [n_ref[i] for i in range(NB * D_HEADS)]
    m_st = [m_ref[i] for i in range(NB * D_HEADS)]

    probs = [(ci, h) for ci in range(NB * nc) for h in heads]
    g = {}
    for ci, h in probs:
        rs = rows[ci]
        b_col = bcum[rs, D_HEADS + h:D_HEADS + h + 1]
        b_row = bcum_t[D_HEADS + h:D_HEADS + h + 1, rs]
        dmat = jnp.where(lower, b_col - b_row + gates_t[h:h + 1, rs], -jnp.inf)
        g[ci, h] = dict(b_col=b_col, i_col=gates[rs, h:h + 1], dmat=dmat,
                        dmax=jnp.max(dmat, axis=-1, keepdims=True), b_last=b_col[L - 1:L, :])
    for ci, h in probs:
        e, i = g[ci, h], sid(ci, h)
        e["e_inter"] = e["b_col"] + m_st[i]
        e["m_t"] = jnp.maximum(e["e_inter"], e["dmax"])
        e["m_new"] = e["m_t"][L - 1:L, :]
        e["d_prev"] = jnp.exp(e["b_last"] + m_st[i] - e["m_new"])
        m_st[i] = e["m_new"]
    for p in probs:
        e = g[p]
        e["w_intra"] = jnp.exp(e["dmat"] - e["m_t"])
        e["s_inter"] = jnp.exp(e["e_inter"] - e["m_t"])
        e["w_last"] = jnp.exp(e["b_last"] - e["b_col"] + e["i_col"] - e["m_new"])
    qk = {(ci, h): qk0[ci][h] * g[ci, h]["w_intra"] for ci, h in probs}
    kw = {(ci, h): k[rows[ci], sls[h]] * g[ci, h]["w_last"] for ci, h in probs}
    qkv = {(ci, h): _dg(qk[ci, h].astype(BF16), vb[rows[ci], sls[h]], _NN) for ci, h in probs}
    cu = {(ci, h): _dg(kw[ci, h].astype(BF16), vb[rows[ci], sls[h]], _TN) for ci, h in probs}
    c_in, n_in = {}, {}
    for ci, h in probs:
        i = sid(ci, h)
        c_in[ci, h], n_in[ci, h] = c_st[i], n_st[i]
        c_st[i] = g[ci, h]["d_prev"] * c_st[i] + cu[ci, h]
        n_st[i] = g[ci, h]["d_prev"] * n_st[i] + jnp.sum(kw[ci, h], axis=0, keepdims=True)
    qc = {(ci, h): _dg(qb[rows[ci], sls[h]], c_in[ci, h].astype(BF16), _NN) for ci, h in probs}
    for ci, h in probs:
        e, rs, sl = g[ci, h], rows[ci], sls[h]
        num = e["s_inter"] * qc[ci, h] + qkv[ci, h]
        den = (e["s_inter"] * jnp.sum(q[rs, sl] * n_in[ci, h], axis=-1, keepdims=True)
               + jnp.sum(qk[ci, h], axis=-1, keepdims=True))
        hh = num / jnp.maximum(jnp.abs(den), jnp.exp(-e["m_t"]))
        ms = jnp.sum(hh * hh, axis=-1, keepdims=True) * (1.0 / D_HEAD)
        b, lrs = ci // nc, slice((ci % nc) * L, (ci % nc + 1) * L)
        out = hh * lax.rsqrt(ms + RMS_EPS) * gain_ref[:, sl] * _sigmoid(o_ref[b, lrs, sl].astype(F32))
        y_ref[b, lrs, sl] = out.astype(y_ref.dtype)

    for i in range(NB * D_HEADS):
        c_ref[i] = c_st[i]
        n_ref[i] = n_st[i]
        m_ref[i] = m_st[i]


def _mlstm(u, conv_q, conv_k, gate_bias, gain, *, batch, seq):
    L = MLSTM_TILE
    NB = MLSTM_BATCH
    nt = seq // L
    W = D_WIDTH_PAD
    u = u.reshape(batch, seq, u.shape[-1])
    col = lambda j: pl.BlockSpec((NB, L, W), lambda b, c, j=j: (b, c, j))
    gcol = pl.BlockSpec((NB, L, D_GATE_PAD), lambda b, c: (b, c, (4 * W + C_WIDTH) // D_GATE_PAD))
    full = lambda a: pl.BlockSpec(a.shape, lambda b, c: (0,) * a.ndim)
    return pl.pallas_call(
        _mlstm_kernel,
        grid=(batch // NB, nt),
        in_specs=[col(0), col(1), col(2), col(3), gcol, full(conv_q), full(conv_k),
                  full(gate_bias), full(gain)],
        out_specs=pl.BlockSpec((NB, L, W), lambda b, c: (b, c, 0)),
        out_shape=jax.ShapeDtypeStruct((batch, seq, W), BF16),
        scratch_shapes=[pltpu.VMEM((NB * D_HEADS, D_HEAD_PAD, D_HEAD_PAD), F32),
                        pltpu.VMEM((NB * D_HEADS, 1, D_HEAD_PAD), F32),
                        pltpu.VMEM((NB * D_HEADS, 1, 1), F32),
                        pltpu.VMEM((NB, 16, W), BF16), pltpu.VMEM((NB, 16, W), BF16)],
        compiler_params=_cparams(("parallel", "arbitrary")),
        name="mlstm_mix",
    )(u, u, u, u, u, conv_q, conv_k, gate_bias, gain).reshape(batch * seq, W)


def _pad_cols(a, n):
    return jnp.pad(a, [(0, 0)] * (a.ndim - 1) + [(0, n - a.shape[-1])])


def _pad_heads(a, axis):
    zshape = a.shape[:axis] + (D_HEAD_PAD - D_HEAD,) + a.shape[axis + 1:]
    pieces = []
    for h in range(D_HEADS):
        pieces += [lax.slice_in_dim(a, h * D_HEAD, (h + 1) * D_HEAD, axis=axis), jnp.zeros(zshape, a.dtype)]
    return jnp.concatenate(pieces, axis=axis)


def _row(v):
    return v.reshape(1, -1).astype(F32)


def kernel(x, norm_mix, norm_ffn, norm_final, w_in_even, w_out_even, lb_table, a_norm, b_mu, b_w0, b_w2, b_a0, b_a2, b_g2, b_kk, b_ka, b_rk, b_ln_w, b_ln_b, w_in_odd, w_out_odd, c_lam_re, c_lam_im, c_log_step, c_b_re, c_b_im, c_c_re, c_c_im, c_d, c_glu_w, c_glu_b, d_conv_q, d_conv_k, d_i_bias, d_f_bias, d_norm, ffn_gate, ffn_up, ffn_down):
    batch, seq, d = x.shape
    m = batch * seq
    xf = x.reshape(m, d).astype(F32)
    lower_bounds = jnp.cumsum(jax.nn.softmax(lb_table.astype(F32), axis=0), axis=0)

    w_in = _pad_cols(w_in_even[0], EVEN_IN_PAD).astype(BF16)
    u = _norm_matmul(xf, _row(norm_mix[0]), w_in, tm=512, tn=3 * MXU_WIDTH)
    ya = _hgrn2(u, _row(lower_bounds[0]), _row(a_norm[0]), batch=batch, seq=seq)

    mu = b_mu[0].astype(F32)
    lora_rows = lambda w, off: jnp.pad(w.astype(F32), ((off, B_TAIL - off - w.shape[0]), (0, 0)))
    rw = {
        "mu_r": _row(mu[:B_WIDTH]), "mu_k": _row(mu[B_WIDTH:2 * B_WIDTH]),
        "mu_v": _row(mu[2 * B_WIDTH:3 * B_WIDTH]), "mu_t": _row(_pad_cols(mu[3 * B_WIDTH:], B_TAIL)),
        "w0": _row(b_w0[0]), "w2": lora_rows(b_w2[0], 0),
        "a0": _row(b_a0[0]), "a2": lora_rows(b_a2[0], B_DECAY_LORA),
        "g2": lora_rows(b_g2[0], B_DECAY_LORA + B_AAA_LORA).astype(BF16),
        "kk": _row(b_kk[0]), "ka": _row(b_ka[0]), "rk": _row(b_rk[0]),
        "lnw": _row(b_ln_w[0]), "lnb": _row(b_ln_b[0]),
    }
    yb = _rwkv7(u, rw, batch=batch, seq=seq)
    wo = w_out_even[0].astype(BF16)
    xf = _out_proj_ffn(ya, yb, wo[:A_WIDTH], wo[A_WIDTH:], xf, _row(norm_ffn[0]),
                       ffn_gate[0].astype(BF16), ffn_up[0].astype(BF16), ffn_down[0].astype(BF16),
                       _row(norm_final), tm=512, th=FFN_HIDDEN, final_norm=False)

    wi = w_in_odd[0].astype(BF16)
    seg = lambda j: _pad_heads(wi[:, C_WIDTH + j * D_WIDTH:C_WIDTH + (j + 1) * D_WIDTH], 1)
    w_in = jnp.concatenate(
        [seg(0), seg(1), seg(2), seg(3), wi[:, :C_WIDTH],
         _pad_cols(wi[:, C_WIDTH + 4 * D_WIDTH:], D_GATE_PAD)], axis=1)
    u = _norm_matmul(xf, _row(norm_mix[1]), w_in, tm=512, tn=4 * MXU_WIDTH)

    u2 = u[:, 4 * D_WIDTH_PAD:4 * D_WIDTH_PAD + C_WIDTH].reshape(m // S5_CHUNK, S5_CHUNK * C_WIDTH)
    s5p = _s5_params(c_lam_re[0].astype(F32), c_lam_im[0].astype(F32), c_log_step[0].astype(F32),
                     c_b_re[0].astype(F32), c_b_im[0].astype(F32), c_c_re[0].astype(F32),
                     c_c_im[0].astype(F32))
    tr = min(1024, m // S5_CHUNK)
    z = _s5_increments(u2, s5p["w"], tm=tr, tn=512)
    hp = _s5_carry(z, s5p["lr"], s5p["li"], batch=batch)
    yc = _s5_output(u2, hp, s5p["bd"], s5p["m"], _row(c_d[0]), c_glu_w[0].astype(BF16),
                    _row(c_glu_b[0]), tm=tr).reshape(m, C_WIDTH)

    gate_bias = _row(_pad_cols(jnp.concatenate([d_i_bias[0], d_f_bias[0]]).astype(F32), D_GATE_PAD))
    yd = _mlstm(u, _pad_heads(d_conv_q[0].astype(F32), 1), _pad_heads(d_conv_k[0].astype(F32), 1),
                gate_bias, _row(_pad_heads(d_norm[0].astype(F32), 0)), batch=batch, seq=seq)
    wo = w_out_odd[0].astype(BF16)
    xf = _out_proj_ffn(yc, yd, wo[:C_WIDTH], _pad_heads(wo[C_WIDTH:], 0), xf,
                       _row(norm_ffn[1]), ffn_gate[1].astype(BF16), ffn_up[1].astype(BF16),
                       ffn_down[1].astype(BF16), _row(norm_final), tm=512, th=FFN_HIDDEN, final_norm=True)
    return xf.reshape(batch, seq, d).astype(x.dtype)
```

```python
import functools
import math

import jax
import jax.numpy as jnp
from jax import lax
from jax.experimental import pallas as pl
from jax.experimental.pallas import tpu as pltpu

F32 = jnp.float32
BF16 = jnp.bfloat16

RMS_EPS = 1e-6
GN_EPS = 64e-5
M_INIT = -1e30

D_MODEL = 1024
A_WIDTH, A_HEADS, A_KEY = 512, 4, 128
B_WIDTH, B_HEADS, B_HEAD = 512, 8, 64
B_DECAY_LORA, B_AAA_LORA, B_GATE_LORA = 32, 32, 96
B_TAIL = 256
EVEN_IN_PAD = 4 * A_WIDTH + 3 * B_WIDTH + B_TAIL
C_WIDTH, C_GROUP, C_GROUPS, C_STATE = 256, 16, 16, 64
D_HEADS, D_HEAD, D_HEAD_PAD = 4, 192, 256
D_WIDTH = D_HEADS * D_HEAD
D_WIDTH_PAD = D_HEADS * D_HEAD_PAD
D_CONV = 4
D_GATE_PAD = 128
ODD_IN_PAD = 4 * D_WIDTH_PAD + C_WIDTH + D_GATE_PAD
FFN_HIDDEN = 2816

HGRN_TILE = 128
HGRN_SUB = 16
RWKV_CHUNK = 64
RWKV_TILE = 256
RWKV_BATCH = 2
S5_CHUNK = 16
MLSTM_CHUNK = 64
MLSTM_TILE = 256
MLSTM_BATCH = 1

VMEM_LIMIT = 56 * 1024 * 1024


def _cparams(sem):
    return pltpu.CompilerParams(dimension_semantics=sem, vmem_limit_bytes=VMEM_LIMIT)


def _split2(x):
    hi = x.astype(BF16)
    lo = (x - hi.astype(F32)).astype(BF16)
    return hi, lo


def _split3(x):
    hi = x.astype(BF16)
    r1 = x - hi.astype(F32)
    mid = r1.astype(BF16)
    lo = (r1 - mid.astype(F32)).astype(BF16)
    return hi, mid, lo


def _dg(a, b, dims):
    return lax.dot_general(a, b, (dims, ((), ())), preferred_element_type=F32)


_NN = ((1,), (0,))
_NT = ((1,), (1,))
_TN = ((0,), (0,))


def _mm1(a, b, dims=_NN):
    return _dg(a.astype(BF16), b.astype(BF16), dims)


def _mm3(a, b, dims=_NN):
    ah, al = _split2(a)
    bh, bl = _split2(b)
    return _dg(ah, bh, dims) + (_dg(ah, bl, dims) + _dg(al, bh, dims))


def _mm2(a, b, dims=_NN):
    ah, al = _split2(a)
    bh = b.astype(BF16)
    return _dg(ah, bh, dims) + _dg(al, bh, dims)


_mmi = _mm1


def _mm_sel(sel_bf16, x, dims=_NN):
    h, m, l = _split3(x)
    return _dg(sel_bf16, h, dims) + (_dg(sel_bf16, m, dims) + _dg(sel_bf16, l, dims))


def _x_sel(x, sel_bf16):
    n = x.shape[0]
    r = _dg(jnp.concatenate(_split3(x), axis=0), sel_bf16, _NN)
    return r[:n] + (r[n:2 * n] + r[2 * n:])


def _sigmoid(x):
    return 1.0 / (1.0 + jnp.exp(-x))


def _silu(x):
    return x * _sigmoid(x)


def _softplus(x):
    return jnp.maximum(x, 0.0) + jnp.log(1.0 + jnp.exp(-jnp.abs(x)))


def _gelu_tanh(x):
    c = math.sqrt(2.0 / math.pi)
    return 0.5 * x * (1.0 + jnp.tanh(c * (x + 0.044715 * (x * x * x))))


def _iota(shape, dim):
    return lax.broadcasted_iota(jnp.int32, shape, dim)


def _tril_mask(n, strict=False):
    r = _iota((n, n), 0)
    c = _iota((n, n), 1)
    return (r > c) if strict else (r >= c)


MXU_WIDTH = 256


def _col_chunks(n, target):
    bounds = list(range(0, n, target)) + [n]
    return list(zip(bounds[:-1], bounds[1:]))


def _resident(shape):
    return pl.BlockSpec(shape, lambda i: (0,) * len(shape))


def _norm_matmul_kernel(x_ref, g_ref, w_ref, o_ref, *, chunks):
    x = x_ref[...]
    ms = jnp.mean(x * x, axis=-1, keepdims=True)
    h = (x * lax.rsqrt(ms + RMS_EPS) * g_ref[...]).astype(BF16)
    for lo, hi in chunks:
        o_ref[:, lo:hi] = jnp.dot(h, w_ref[:, lo:hi], preferred_element_type=F32).astype(o_ref.dtype)


def _norm_matmul(x, g, w, *, tm, tn):
    m, d = x.shape
    n = w.shape[1]
    return pl.pallas_call(
        functools.partial(_norm_matmul_kernel, chunks=_col_chunks(n, tn)),
        grid=(m // tm,),
        in_specs=[pl.BlockSpec((tm, d), lambda i: (i, 0)), _resident((1, d)), _resident((d, n))],
        out_specs=pl.BlockSpec((tm, n), lambda i: (i, 0)),
        out_shape=jax.ShapeDtypeStruct((m, n), BF16),
        compiler_params=_cparams(("parallel",)),
        name="norm_in_proj",
    )(x, g, w)


def _ffn_kernel(a_ref, b_ref, wa_ref, wb_ref, x_ref, g_ref, wg_ref, wu_ref, wd_ref, gf_ref, o_ref,
                *, chunks, final_norm):
    x = x_ref[...] + jnp.dot(a_ref[...], wa_ref[...], preferred_element_type=F32)
    x = x + jnp.dot(b_ref[...], wb_ref[...], preferred_element_type=F32)
    ms = jnp.mean(x * x, axis=-1, keepdims=True)
    h = (x * lax.rsqrt(ms + RMS_EPS) * g_ref[...]).astype(BF16)
    y = x
    for lo, hi in chunks:
        gate = jnp.dot(h, wg_ref[:, lo:hi], preferred_element_type=F32)
        up = jnp.dot(h, wu_ref[:, lo:hi], preferred_element_type=F32)
        act = (_silu(gate) * up).astype(BF16)
        y = y + jnp.dot(act, wd_ref[lo:hi, :], preferred_element_type=F32)
    if final_norm:
        ms = jnp.mean(y * y, axis=-1, keepdims=True)
        y = y * lax.rsqrt(ms + RMS_EPS) * gf_ref[...]
    o_ref[...] = y


def _out_proj_ffn(a, b, wa, wb, x, g, wg, wu, wd, gf, *, tm, th, final_norm):
    m, d = x.shape
    hid = wg.shape[1]
    ka, kb = a.shape[1], b.shape[1]
    return pl.pallas_call(
        functools.partial(_ffn_kernel, chunks=_col_chunks(hid, th), final_norm=final_norm),
        grid=(m // tm,),
        in_specs=[
            pl.BlockSpec((tm, ka), lambda i: (i, 0)),
            pl.BlockSpec((tm, kb), lambda i: (i, 0)),
            _resident((ka, d)), _resident((kb, d)),
            pl.BlockSpec((tm, d), lambda i: (i, 0)),
            _resident((1, d)), _resident((d, hid)), _resident((d, hid)), _resident((hid, d)),
            _resident((1, d)),
        ],
        out_specs=pl.BlockSpec((tm, d), lambda i: (i, 0)),
        out_shape=jax.ShapeDtypeStruct((m, d), F32),
        compiler_params=_cparams(("parallel",)),
        name="out_proj_swiglu_ffn",
    )(a, b, wa, wb, x, g, wg, wu, wd, gf)


def _hgrn2_kernel(q_ref, f_ref, i_ref, g_ref, lb_ref, gain_ref, o_ref, s_ref):
    T, SUB = HGRN_TILE, HGRN_SUB
    nb = T // SUB

    @pl.when(pl.program_id(1) == 0)
    def _():
        s_ref[...] = jnp.zeros_like(s_ref)

    r = _iota((T, T), 0)
    c = _iota((T, T), 1)
    tri = jnp.where((r >= c) & ((r // SUB) == (c // SUB)), 1.0, 0.0).astype(BF16)
    HALF = SUB // 2
    rowh = _iota((nb, HALF, A_KEY), 1)
    heads = range(A_HEADS)
    sls = [slice(h * A_KEY, (h + 1) * A_KEY) for h in heads]
    intra, qts, kts, v3s, decs = [], [], [], [], []

    for h in heads:
        sl = sls[h]
        q = _silu(q_ref[:, sl].astype(F32))
        lb = lb_ref[:, sl]
        f = lb + (1.0 - lb) * _sigmoid(f_ref[:, sl].astype(F32))
        lf = jnp.log(f)
        k = 1.0 - f
        v = i_ref[:, sl].astype(F32)
        b = _mm_sel(tri, lf)
        b3 = b.reshape(nb, SUB, A_KEY)
        q3 = q.reshape(nb, SUB, A_KEY)
        k3 = k.reshape(nb, SUB, A_KEY)
        v3 = v.reshape(nb, SUB, A_KEY)
        log2e = 1.0 / math.log(2.0)
        b2 = b3 * log2e
        bk3 = b2 - jnp.log2(k3)
        b_up, b_lo = b2[:, :HALF], b2[:, HALF:]
        q_up, q_lo = q3[:, :HALF], q3[:, HALF:]
        acc_up = jnp.zeros((nb, HALF, A_KEY), F32)
        acc_lo = jnp.zeros((nb, HALF, A_KEY), F32)
        for s in range(HALF):
            bs = bk3[:, s:s + 1, :]
            vs = v3[:, s:s + 1, :]
            e_up = jnp.exp2(jnp.where(rowh >= s, b_up - bs, -jnp.inf))
            acc_up = acc_up + jnp.sum(q_up * e_up, axis=-1, keepdims=True) * vs
            acc_lo = acc_lo + jnp.sum(q_lo * jnp.exp2(b_lo - bs), axis=-1, keepdims=True) * vs
        for s in range(HALF):
            bs = bk3[:, HALF + s:HALF + s + 1, :]
            e_lo = jnp.exp2(jnp.where(rowh >= s, b_lo - bs, -jnp.inf))
            acc_lo = acc_lo + jnp.sum(q_lo * e_lo, axis=-1, keepdims=True) * v3[:, HALF + s:HALF + s + 1, :]
        intra.append(jnp.concatenate([acc_up, acc_lo], axis=1).reshape(T, A_KEY))

        bend = b3[:, SUB - 1:SUB, :]
        qts.append((q3 * jnp.exp(b3)).astype(BF16))
        kts.append((k3 * jnp.exp(bend - b3)).astype(BF16))
        v3s.append(v3.astype(BF16))
        decs.append(jnp.exp(bend))

    inc = [[_dg(v3s[h][i], kts[h][i], _TN) for i in range(nb)] for h in heads]
    for h in heads:
        st = s_ref[h]
        outs = []
        for i in range(nb):
            outs.append(_dg(qts[h][i], st.astype(BF16), _NT))
            st = st * decs[h][i] + inc[h][i]
        s_ref[h] = st
        o = intra[h] + jnp.concatenate(outs, axis=0)
        o = o * lax.rsqrt(jnp.mean(o * o, axis=-1, keepdims=True) + RMS_EPS) * gain_ref[:, sls[h]]
        o_ref[:, sls[h]] = (o * _silu(g_ref[:, sls[h]].astype(F32))).astype(o_ref.dtype)


def _hgrn2(u, lb, gain, *, batch, seq):
    T = HGRN_TILE
    nt = seq // T
    col = lambda j: pl.BlockSpec((T, A_WIDTH), lambda b, c, j=j: (b * nt + c, j))
    vec = pl.BlockSpec((1, A_WIDTH), lambda b, c: (0, 0))
    return pl.pallas_call(
        _hgrn2_kernel,
        grid=(batch, nt),
        in_specs=[col(0), col(1), col(2), col(3), vec, vec],
        out_specs=pl.BlockSpec((T, A_WIDTH), lambda b, c: (b * nt + c, 0)),
        out_shape=jax.ShapeDtypeStruct((batch * seq, A_WIDTH), BF16),
        scratch_shapes=[pltpu.VMEM((A_HEADS, A_KEY, A_KEY), F32)],
        compiler_params=_cparams(("parallel", "arbitrary")),
        name="hgrn2_mix",
    )(u, u, u, u, lb, gain)


def _rwkv7_kernel(r_ref, k_ref, v_ref, t_ref, mu_r, mu_k, mu_v, mu_t, w0_ref, w2_ref, a0_ref,
                  a2_ref, g2_ref, kk_ref, ka_ref, rk_ref, lnw_ref, lnb_ref, o_ref,
                  s_ref, cr_ref, ck_ref, cv_ref, ct_ref):
    L = RWKV_CHUNK
    NB = RWKV_BATCH
    TB = RWKV_TILE
    T = NB * TB
    N = B_HEAD
    nc = TB // L

    @pl.when(pl.program_id(1) == 0)
    def _():
        s_ref[...] = jnp.zeros_like(s_ref)
        cr_ref[...] = jnp.zeros_like(cr_ref)
        ck_ref[...] = jnp.zeros_like(ck_ref)
        cv_ref[...] = jnp.zeros_like(cv_ref)
        ct_ref[...] = jnp.zeros_like(ct_ref)

    def shift_mix(x_ref, carry_ref, mu_ref):
        outs = []
        for b in range(NB):
            x = x_ref[b].astype(F32)
            row = _iota(x.shape, 0)
            prev = jnp.where(row == 0, carry_ref[b, 7:8, :], pltpu.roll(x, 1, 0))
            carry_ref[b] = x[TB - 8:TB, :]
            outs.append(x + (prev - x) * mu_ref[...])
        return jnp.concatenate(outs, axis=0)

    r = shift_mix(r_ref, cr_ref, mu_r)
    k = shift_mix(k_ref, ck_ref, mu_k)
    v = shift_mix(v_ref, cv_ref, mu_v)
    tl = shift_mix(t_ref, ct_ref, mu_t)

    wl = _mm1(jnp.tanh(tl), w2_ref[...])
    al = _mm1(tl, a2_ref[...])
    gate = _mm1(_sigmoid(tl), g2_ref[...])
    w_log = -_softplus(-(w0_ref[...] + wl)) - 0.5
    lw = -jnp.exp(w_log)
    a_lr = _sigmoid(a0_ref[...] + al)

    ones_bd = jnp.where(_iota((MXU_WIDTH, MXU_WIDTH), 0) // N == _iota((MXU_WIDTH, MXU_WIDTH), 1) // N,
                        1.0, 0.0).astype(BF16)

    def head_sums(x):
        n = x.shape[0]
        parts = jnp.concatenate(_split2(x), axis=0)
        r = jnp.concatenate([_dg(parts[:, g * MXU_WIDTH:(g + 1) * MXU_WIDTH], ones_bd, _NN)
                             for g in range(B_WIDTH // MXU_WIDTH)], axis=-1)
        return r[:n] + r[n:]

    kk = k * kk_ref[...]
    k2 = k * (1.0 + (a_lr - 1.0) * ka_ref[...])
    sums = head_sums(jnp.concatenate([kk * kk, r * k2 * rk_ref[...]], axis=0))
    kk = kk / jnp.maximum(jnp.sqrt(sums[:T]), 1e-12)
    bonus = sums[T:]
    aa = -kk
    bb = kk * a_lr

    rr = _iota((TB, TB), 0)
    cc = _iota((TB, TB), 1)
    tri = jnp.where((rr >= cc) & (rr // L == cc // L), 1.0, 0.0).astype(BF16)
    c = jnp.concatenate([_mm_sel(tri, lw[b * TB:(b + 1) * TB, :]) for b in range(NB)], axis=0)
    enc = jnp.exp(-c)
    rt = (r * jnp.exp(c)).astype(BF16)
    at = (aa * jnp.exp(c - lw)).astype(BF16)
    bt = (bb * enc).astype(BF16)
    kt = (k2 * enc).astype(BF16)
    vb = v.astype(BF16)

    PW = 2 * N
    pairs = range(B_HEADS // 2)
    pc = lambda p: slice(p * PW, (p + 1) * PW)
    r2 = _iota((2 * L, PW), 0)
    c2 = _iota((2 * L, PW), 1) % L
    both = ((r2 < L) & (r2 > c2)) | (r2 - L >= c2)
    left1 = _iota((L, PW), 1) < N
    left2 = (_iota((L, 2 * PW), 1) % PW) < N
    same_head = (_iota((PW, PW), 0) // N) == (_iota((PW, PW), 1) // N)

    def bdiag(z, left):
        zero = jnp.zeros_like(z)
        return jnp.concatenate([jnp.where(left, z, zero), jnp.where(left, zero, z)], axis=0)

    solved = {}
    NP = B_HEADS // 2
    state = {"s": [[s_ref[b * NP + p] for p in pairs] for b in range(NB)], "y": {}}

    def solve_stages(ci):
        rs = slice(ci * L, (ci + 1) * L)
        ar = [jnp.concatenate([at[rs, pc(p)], rt[rs, pc(p)]], axis=0) for p in pairs]
        vd = [bdiag(vb[rs, pc(p)], left1) for p in pairs]
        m_b = [jnp.where(both, _dg(ar[p], bdiag(bt[rs, pc(p)], left1), _NT), 0.0) for p in pairs]
        yield
        m_k = [jnp.where(both, _dg(ar[p], bdiag(kt[rs, pc(p)], left1), _NT), 0.0).astype(BF16) for p in pairs]
        yield
        mkv = [_dg(m_k[p], vd[p], _NN) for p in pairs]
        yield
        pw = [m_b[p][:L] for p in pairs]
        x = [jnp.concatenate([ar[p][:L].astype(F32), mkv[p][:L]], axis=-1) for p in pairs]
        x = [x[p] + _dg(pw[p].astype(BF16), bdiag(x[p].astype(BF16), left2), _NN) for p in pairs]
        yield
        for _ in range(int(math.log2(L)) - 1):
            pw = [_dg(pw[p].astype(BF16), bdiag(pw[p].astype(BF16), left1), _NN) for p in pairs]
            yield
            x = [x[p] + _dg(pw[p].astype(BF16), bdiag(x[p].astype(BF16), left2), _NN) for p in pairs]
            yield
        c_last = c[(ci + 1) * L - 1:(ci + 1) * L, :]
        e_tail = jnp.exp(c_last - c[rs, :])
        solved[ci] = dict(
            wr=[jnp.concatenate([x[p][:, :PW].astype(BF16), ar[p][L:]], axis=0) for p in pairs],
            u0=[x[p][:, PW:] for p in pairs],
            a_rb=[m_b[p][L:].astype(BF16) for p in pairs],
            y0=[mkv[p][L:] for p in pairs], vh=[vb[rs, pc(p)] for p in pairs], g_last=jnp.exp(c_last),
            bkl=jnp.concatenate([(bb[rs, :] * e_tail).astype(BF16),
                                 (k2[rs, :] * e_tail).astype(BF16)], axis=0))
        yield

    def state_stages(ci):
        q = solved.pop(ci)
        st = state["s"][ci // nc]
        ws = [_dg(q["wr"][p], st[p].astype(BF16), _NT) for p in pairs]
        yield
        ub = [(ws[p][:L] + q["u0"][p]).astype(BF16) for p in pairs]
        state["y"][ci] = jnp.concatenate(
            [ws[p][L:] + _dg(q["a_rb"][p], bdiag(ub[p], left1), _NN) + q["y0"][p] for p in pairs], axis=-1)
        upd = [_dg(jnp.concatenate([ub[p], q["vh"][p]], axis=0), q["bkl"][:, pc(p)], _TN) for p in pairs]
        state["s"][ci // nc] = [st[p] * q["g_last"][:, pc(p)] + jnp.where(same_head, upd[p], 0.0)
                                for p in pairs]
        yield

    for _ in zip(*[solve_stages(ci) for ci in range(NB * nc)]):
        pass
    for ci in range(nc):
        for _ in zip(*[state_stages(b * nc + ci) for b in range(NB)]):
            pass
    for b in range(NB):
        for p in pairs:
            s_ref[b * NP + p] = state["s"][b][p]
    ys = [state["y"][ci] for ci in range(NB * nc)]

    y = jnp.concatenate(ys, axis=0)
    mom = head_sums(jnp.concatenate([y, y * y], axis=0)) * (1.0 / N)
    mean = mom[:T]
    var = jnp.maximum(mom[T:] - mean * mean, 0.0)
    y = (y - mean) * lax.rsqrt(var + GN_EPS) * lnw_ref[...] + lnb_ref[...]
    y = y + bonus * v
    out = (y * gate).astype(o_ref.dtype)
    for b in range(NB):
        o_ref[b] = out[b * TB:(b + 1) * TB, :]


def _rwkv7(u, p, *, batch, seq):
    T = RWKV_TILE
    NB = RWKV_BATCH
    nt = seq // T
    u = u.reshape(batch, seq, u.shape[-1])
    col = lambda j: pl.BlockSpec((NB, T, B_WIDTH), lambda b, c, j=j: (b, c, j))
    tail = pl.BlockSpec((NB, T, B_TAIL), lambda b, c: (b, c, (4 * A_WIDTH + 3 * B_WIDTH) // B_TAIL))
    full = lambda a: pl.BlockSpec(a.shape, lambda b, c: (0,) * a.ndim)
    params = [p["mu_r"], p["mu_k"], p["mu_v"], p["mu_t"], p["w0"], p["w2"], p["a0"], p["a2"],
              p["g2"], p["kk"], p["ka"], p["rk"], p["lnw"], p["lnb"]]
    return pl.pallas_call(
        _rwkv7_kernel,
        grid=(batch // NB, nt),
        in_specs=[col(4), col(5), col(6), tail] + [full(a) for a in params],
        out_specs=pl.BlockSpec((NB, T, B_WIDTH), lambda b, c: (b, c, 0)),
        out_shape=jax.ShapeDtypeStruct((batch, seq, B_WIDTH), BF16),
        scratch_shapes=[pltpu.VMEM((NB * (B_HEADS // 2), 2 * B_HEAD, 2 * B_HEAD), F32),
                        pltpu.VMEM((NB, 8, B_WIDTH), F32), pltpu.VMEM((NB, 8, B_WIDTH), F32),
                        pltpu.VMEM((NB, 8, B_WIDTH), F32), pltpu.VMEM((NB, 8, B_TAIL), F32)],
        compiler_params=_cparams(("parallel", "arbitrary")),
        name="rwkv7_mix",
    )(u, u, u, u, *params).reshape(batch * seq, B_WIDTH)


def _s5_inc_kernel(u_ref, w_ref, z_ref):
    z_ref[...] = jnp.dot(u_ref[...], w_ref[...], preferred_element_type=F32)


def _s5_increments(u2, w, *, tm, tn):
    n, kdim = u2.shape
    ncol = w.shape[1]
    return pl.pallas_call(
        _s5_inc_kernel,
        grid=(n // tm, ncol // tn),
        in_specs=[pl.BlockSpec((tm, kdim), lambda i, j: (i, 0)),
                  pl.BlockSpec((kdim, tn), lambda i, j: (0, j))],
        out_specs=pl.BlockSpec((tm, tn), lambda i, j: (i, j)),
        out_shape=jax.ShapeDtypeStruct((n, ncol), F32),
        compiler_params=_cparams(("parallel", "parallel")),
        name="s5_increments",
    )(u2, w)


def _s5_carry_kernel(z_ref, lr_ref, li_ref, hp_ref, h_ref):
    n = z_ref.shape[0]
    half = z_ref.shape[1] // 2
    lr = lr_ref[...]
    li = li_ref[...]

    def body(i, carry):
        hr, hi = carry
        rows = pl.ds(pl.multiple_of(i * 8, 8), 8)
        zz = z_ref[rows, :]
        outs = []
        for j in range(8):
            outs.append(jnp.concatenate([hr, hi], axis=-1))
            hr, hi = (lr * hr - li * hi + zz[j:j + 1, :half],
                      lr * hi + li * hr + zz[j:j + 1, half:])
        h_ref[rows, :] = jnp.concatenate(outs, axis=0)
        return hr, hi

    zero = jnp.zeros((1, half), F32)
    lax.fori_loop(0, n // 8, body, (zero, zero))
    hp_ref[...] = h_ref[...].astype(hp_ref.dtype)


def _s5_carry(z, lr, li, *, batch):
    n, w = z.shape
    rows = n // batch
    vec = pl.BlockSpec((1, w // 2), lambda b: (0, 0))
    return pl.pallas_call(
        _s5_carry_kernel,
        grid=(batch,),
        in_specs=[pl.BlockSpec((rows, w), lambda b: (b, 0)), vec, vec],
        out_specs=pl.BlockSpec((rows, w), lambda b: (b, 0)),
        out_shape=jax.ShapeDtypeStruct((n, w), BF16),
        scratch_shapes=[pltpu.VMEM((rows, w), F32)],
        compiler_params=_cparams(("parallel",)),
        name="s5_carry",
    )(z, lr, li)


def _s5_out_kernel(u_ref, uj_ref, hp_ref, bd_ref, m_ref, d_ref, gw_ref, gb_ref, o_ref, acc_ref):
    t = pl.program_id(1)
    acc_ref[...] = jnp.dot(hp_ref[...], m_ref[...], preferred_element_type=F32)
    for s in range(S5_CHUNK):
        @pl.when(s <= t)
        def _():
            acc_ref[...] += jnp.dot(u_ref[:, s * C_WIDTH:(s + 1) * C_WIDTH], bd_ref[t - s],
                                    preferred_element_type=F32)
    z = _gelu_tanh(acc_ref[...] + d_ref[...] * uj_ref[...].astype(F32))
    gate = jnp.dot(z.astype(BF16), gw_ref[...], preferred_element_type=F32) + gb_ref[...]
    o_ref[...] = (z * _sigmoid(gate)).astype(o_ref.dtype)


def _s5_output(u2, hp, bd, mmat, d, gw, gb, *, tm):
    n, kdim = u2.shape
    kst = hp.shape[1]
    vec = pl.BlockSpec((1, C_WIDTH), lambda i, t: (0, 0))
    return pl.pallas_call(
        _s5_out_kernel,
        grid=(n // tm, S5_CHUNK),
        in_specs=[pl.BlockSpec((tm, kdim), lambda i, t: (i, 0)),
                  pl.BlockSpec((tm, C_WIDTH), lambda i, t: (i, t)),
                  pl.BlockSpec((tm, kst), lambda i, t: (i, 0)),
                  pl.BlockSpec(bd.shape, lambda i, t: (0, 0, 0)),
                  pl.BlockSpec((kst, C_WIDTH), lambda i, t: (0, t)),
                  vec, pl.BlockSpec((C_WIDTH, C_WIDTH), lambda i, t: (0, 0)), vec],
        out_specs=pl.BlockSpec((tm, C_WIDTH), lambda i, t: (i, t)),
        out_shape=jax.ShapeDtypeStruct((n, kdim), BF16),
        scratch_shapes=[pltpu.VMEM((tm, C_WIDTH), F32)],
        compiler_params=_cparams(("parallel", "arbitrary")),
        name="s5_output",
    )(u2, u2, hp, bd, mmat, d, gw, gb)


def _s5_params(lam_re, lam_im, log_step, b_re, b_im, c_re, c_im):
    L = S5_CHUNK
    step = jnp.exp(log_step)[:, None]
    mag = jnp.exp(lam_re * step)
    lr, li = mag * jnp.cos(lam_im * step), mag * jnp.sin(lam_im * step)
    den = lam_re * lam_re + lam_im * lam_im
    nr, ni = lr - 1.0, li
    coef_re = (nr * lam_re + ni * lam_im) / den
    coef_im = (ni * lam_re - nr * lam_im) / den
    bb_re = coef_re[..., None] * b_re - coef_im[..., None] * b_im
    bb_im = coef_re[..., None] * b_im + coef_im[..., None] * b_re
    pr, pi = [jnp.ones_like(lr)], [jnp.zeros_like(lr)]
    for _ in range(L):
        pr.append(pr[-1] * lr - pi[-1] * li)
        pi.append(pr[-2] * li + pi[-1] * lr)
    pr, pi = jnp.stack(pr), jnp.stack(pi)
    hp = lax.Precision.HIGHEST
    zr = pr[:L, :, :, None] * bb_re[None] - pi[:L, :, :, None] * bb_im[None]
    zi = pr[:L, :, :, None] * bb_im[None] + pi[:L, :, :, None] * bb_re[None]
    kern = (jnp.einsum("gop,lgpi->lgoi", c_re, zr, precision=hp)
            - jnp.einsum("gop,lgpi->lgoi", c_im, zi, precision=hp))
    nin = L * C_WIDTH
    nst = 2 * C_GROUPS * C_STATE

    def spread(a, row_group, n_inner):
        rows, cols = a.shape
        wide = cols * C_GROUPS
        cw = jnp.arange(wide)
        src = (cw // (C_GROUPS * n_inner)) * n_inner + cw % n_inner
        rep = (jnp.arange(cols)[:, None] == src[None, :]).astype(BF16)
        out = jnp.dot(a.astype(BF16), rep, preferred_element_type=F32)
        keep = row_group[:, None] == ((cw // n_inner) % C_GROUPS)[None, :]
        return jnp.where(keep, out, 0.0).astype(BF16)

    ka = kern.transpose(0, 1, 3, 2).reshape(L * C_WIDTH, C_GROUP)
    bd = spread(ka, (jnp.arange(L * C_WIDTH) // C_GROUP) % C_GROUPS, C_GROUP).reshape(L, C_WIDTH, C_WIDTH)
    wz = jnp.stack([zr[::-1], zi[::-1]])
    wa = wz.transpose(1, 2, 4, 0, 3).reshape(nin, 2 * C_STATE)
    w = spread(wa, (jnp.arange(nin) // C_GROUP) % C_GROUPS, C_STATE)
    qr, qi = pr[1:], pi[1:]
    m_re = c_re[None] * qr[:, :, None, :] - c_im[None] * qi[:, :, None, :]
    m_im = -(c_re[None] * qi[:, :, None, :] + c_im[None] * qr[:, :, None, :])
    ma = jnp.stack([m_re, m_im]).transpose(0, 2, 4, 1, 3).reshape(nst, L * C_GROUP)
    mmat = spread(ma, (jnp.arange(nst) // C_STATE) % C_GROUPS, C_GROUP)
    return {"w": w, "bd": bd, "m": mmat,
            "lr": pr[L].reshape(1, -1), "li": pi[L].reshape(1, -1)}


def _mlstm_kernel(q_ref, k_ref, v_ref, o_ref, gt_ref, cq_ref, ck_ref, gb_ref, gain_ref, y_ref,
                  c_ref, n_ref, m_ref, pq_ref, pk_ref):
    L = MLSTM_CHUNK
    P = D_HEAD_PAD

    @pl.when(pl.program_id(1) == 0)
    def _():
        c_ref[...] = jnp.zeros_like(c_ref)
        n_ref[...] = jnp.zeros_like(n_ref)
        m_ref[...] = jnp.full(m_ref.shape, M_INIT, F32)
        pq_ref[...] = jnp.zeros_like(pq_ref)
        pk_ref[...] = jnp.zeros_like(pk_ref)

    NB = MLSTM_BATCH
    TB = MLSTM_TILE
    T = NB * TB
    nc = TB // L
    PREV = pq_ref.shape[1]
    LE = L + PREV
    out_row = _iota((L, D_CONV * LE), 0)
    kcol = _iota((L, D_CONV * LE), 1)
    sel = jnp.where(kcol % LE == out_row + kcol // LE - (D_CONV - 1) + PREV, 1.0, 0.0).astype(BF16)

    def conv(x_ref, prev_ref, w_ref):
        outs = []
        for b in range(NB):
            x = x_ref[b]
            for ci in range(nc):
                prev = prev_ref[b] if ci == 0 else x[ci * L - PREV:ci * L, :]
                xe = jnp.concatenate([prev, x[ci * L:(ci + 1) * L, :]], axis=0).astype(F32)
                stack = jnp.concatenate([(xe * w_ref[j:j + 1, :]).astype(BF16) for j in range(D_CONV)], axis=0)
                outs.append(_dg(sel, stack, _NN))
            prev_ref[b] = x[TB - PREV:TB, :]
        return _silu(jnp.concatenate(outs, axis=0))

    q = conv(q_ref, pq_ref, cq_ref)
    k = conv(k_ref, pk_ref, ck_ref) * (1.0 / math.sqrt(D_HEAD))
    vb = jnp.concatenate([v_ref[b] for b in range(NB)], axis=0)

    gates = jnp.concatenate([gt_ref[b] for b in range(NB)], axis=0).astype(F32) + gb_ref[...]
    col = _iota(gates.shape, 1)
    is_f = (col >= D_HEADS) & (col < 2 * D_HEADS)
    gates = jnp.where(is_f, jnp.minimum(gates, 0.0) - jnp.log(1.0 + jnp.exp(-jnp.abs(gates))), gates)
    rr = _iota((TB, TB), 0)
    cc = _iota((TB, TB), 1)
    tri = jnp.where((rr >= cc) & (rr // L == cc // L), 1.0, 0.0).astype(BF16)
    bcum = jnp.concatenate([_mm_sel(tri, gates[b * TB:(b + 1) * TB, :]) for b in range(NB)], axis=0)
    gates_t = gates.T
    bcum_t = bcum.T
    lower = _tril_mask(L)

    heads = range(D_HEADS)
    sls = [slice(h * P, (h + 1) * P) for h in heads]
    qb, kb = q.astype(BF16), k.astype(BF16)
    rows = [slice(ci * L, (ci + 1) * L) for ci in range(NB * nc)]
    qk0 = [[_dg(qb[rs, sls[h]], kb[rs, sls[h]], _NT) for h in heads] for rs in rows]
    sid = lambda ci, h: (ci // nc) * D_HEADS + h
    c_st = [c_ref[i] for i in range(NB * D_HEADS)]
    n_st = [n_ref[i] for i in range(NB * D_HEADS)]
    m_st = [m_ref[i] for i in range(NB * D_HEADS)]

    probs = [(ci, h) for ci in range(NB * nc) for h in heads]
    g = {}
    for ci, h in probs:
        rs = rows[ci]
        b_col = bcum[rs, D_HEADS + h:D_HEADS + h + 1]
        b_row = bcum_t[D_HEADS + h:D_HEADS + h + 1, rs]
        dmat = jnp.where(lower, b_col - b_row + gates_t[h:h + 1, rs], -jnp.inf)
        g[ci, h] = dict(b_col=b_col, i_col=gates[rs, h:h + 1], dmat=dmat,
                        dmax=jnp.max(dmat, axis=-1, keepdims=True), b_last=b_col[L - 1:L, :])
    for ci, h in probs:
        e, i = g[ci, h], sid(ci, h)
        e["e_inter"] = e["b_col"] + m_st[i]
        e["m_t"] = jnp.maximum(e["e_inter"], e["dmax"])
        e["m_new"] = e["m_t"][L - 1:L, :]
        e["d_prev"] = jnp.exp(e["b_last"] + m_st[i] - e["m_new"])
        m_st[i] = e["m_new"]
    for p in probs:
        e = g[p]
        e["w_intra"] = jnp.exp(e["dmat"] - e["m_t"])
        e["s_inter"] = jnp.exp(e["e_inter"] - e["m_t"])
        e["w_last"] = jnp.exp(e["b_last"] - e["b_col"] + e["i_col"] - e["m_new"])
    qk = {(ci, h): qk0[ci][h] * g[ci, h]["w_intra"] for ci, h in probs}
    kw = {(ci, h): k[rows[ci], sls[h]] * g[ci, h]["w_last"] for ci, h in probs}
    qkv = {(ci, h): _dg(qk[ci, h].astype(BF16), vb[rows[ci], sls[h]], _NN) for ci, h in probs}
    cu = {(ci, h): _dg(kw[ci, h].astype(BF16), vb[rows[ci], sls[h]], _TN) for ci, h in probs}
    c_in, n_in = {}, {}
    for ci, h in probs:
        i = sid(ci, h)
        c_in[ci, h], n_in[ci, h] = c_st[i], n_st[i]
        c_st[i] = g[ci, h]["d_prev"] * c_st[i] + cu[ci, h]
        n_st[i] = g[ci, h]["d_prev"] * n_st[i] + jnp.sum(kw[ci, h], axis=0, keepdims=True)
    qc = {(ci, h): _dg(qb[rows[ci], sls[h]], c_in[ci, h].astype(BF16), _NN) for ci, h in probs}
    qn = {(ci, h): jnp.sum(q[rows[ci], sls[h]] * n_in[ci, h], axis=-1, keepdims=True) for ci, h in probs}
    qks = {p: jnp.sum(qk[p], axis=-1, keepdims=True) for p in probs}
    den = {p: g[p]["s_inter"] * qn[p] + qks[p] for p in probs}
    hh = {p: (g[p]["s_inter"] * qc[p] + qkv[p]) / jnp.maximum(jnp.abs(den[p]), jnp.exp(-g[p]["m_t"]))
          for p in probs}
    ms = {p: jnp.sum(hh[p] * hh[p], axis=-1, keepdims=True) * (1.0 / D_HEAD) for p in probs}
    for ci, h in probs:
        sl = sls[h]
        b, lrs = ci // nc, slice((ci % nc) * L, (ci % nc + 1) * L)
        out = (hh[ci, h] * lax.rsqrt(ms[ci, h] + RMS_EPS) * gain_ref[:, sl]
               * _sigmoid(o_ref[b, lrs, sl].astype(F32)))
        y_ref[b, lrs, sl] = out.astype(y_ref.dtype)

    for i in range(NB * D_HEADS):
        c_ref[i] = c_st[i]
        n_ref[i] = n_st[i]
        m_ref[i] = m_st[i]


def _mlstm(u, conv_q, conv_k, gate_bias, gain, *, batch, seq):
    L = MLSTM_TILE
    NB = MLSTM_BATCH
    nt = seq // L
    W = D_WIDTH_PAD
    u = u.reshape(batch, seq, u.shape[-1])
    col = lambda j: pl.BlockSpec((NB, L, W), lambda b, c, j=j: (b, c, j))
    gcol = pl.BlockSpec((NB, L, D_GATE_PAD), lambda b, c: (b, c, (4 * W + C_WIDTH) // D_GATE_PAD))
    full = lambda a: pl.BlockSpec(a.shape, lambda b, c: (0,) * a.ndim)
    return pl.pallas_call(
        _mlstm_kernel,
        grid=(batch // NB, nt),
        in_specs=[col(0), col(1), col(2), col(3), gcol, full(conv_q), full(conv_k),
                  full(gate_bias), full(gain)],
        out_specs=pl.BlockSpec((NB, L, W), lambda b, c: (b, c, 0)),
        out_shape=jax.ShapeDtypeStruct((batch, seq, W), BF16),
        scratch_shapes=[pltpu.VMEM((NB * D_HEADS, D_HEAD_PAD, D_HEAD_PAD), F32),
                        pltpu.VMEM((NB * D_HEADS, 1, D_HEAD_PAD), F32),
                        pltpu.VMEM((NB * D_HEADS, 1, 1), F32),
                        pltpu.VMEM((NB, 16, W), BF16), pltpu.VMEM((NB, 16, W), BF16)],
        compiler_params=_cparams(("parallel", "arbitrary")),
        name="mlstm_mix",
    )(u, u, u, u, u, conv_q, conv_k, gate_bias, gain).reshape(batch * seq, W)


def _pad_cols(a, n):
    return jnp.pad(a, [(0, 0)] * (a.ndim - 1) + [(0, n - a.shape[-1])])


def _pad_heads(a, axis):
    zshape = a.shape[:axis] + (D_HEAD_PAD - D_HEAD,) + a.shape[axis + 1:]
    pieces = []
    for h in range(D_HEADS):
        pieces += [lax.slice_in_dim(a, h * D_HEAD, (h + 1) * D_HEAD, axis=axis), jnp.zeros(zshape, a.dtype)]
    return jnp.concatenate(pieces, axis=axis)


def _row(v):
    return v.reshape(1, -1).astype(F32)


def kernel(x, norm_mix, norm_ffn, norm_final, w_in_even, w_out_even, lb_table, a_norm, b_mu, b_w0, b_w2, b_a0, b_a2, b_g2, b_kk, b_ka, b_rk, b_ln_w, b_ln_b, w_in_odd, w_out_odd, c_lam_re, c_lam_im, c_log_step, c_b_re, c_b_im, c_c_re, c_c_im, c_d, c_glu_w, c_glu_b, d_conv_q, d_conv_k, d_i_bias, d_f_bias, d_norm, ffn_gate, ffn_up, ffn_down):
    batch, seq, d = x.shape
    m = batch * seq
    xf = x.reshape(m, d).astype(F32)
    lower_bounds = jnp.cumsum(jax.nn.softmax(lb_table.astype(F32), axis=0), axis=0)

    w_in = _pad_cols(w_in_even[0], EVEN_IN_PAD).astype(BF16)
    u = _norm_matmul(xf, _row(norm_mix[0]), w_in, tm=512, tn=3 * MXU_WIDTH)
    ya = _hgrn2(u, _row(lower_bounds[0]), _row(a_norm[0]), batch=batch, seq=seq)

    mu = b_mu[0].astype(F32)
    lora_rows = lambda w, off: jnp.pad(w.astype(F32), ((off, B_TAIL - off - w.shape[0]), (0, 0)))
    rw = {
        "mu_r": _row(mu[:B_WIDTH]), "mu_k": _row(mu[B_WIDTH:2 * B_WIDTH]),
        "mu_v": _row(mu[2 * B_WIDTH:3 * B_WIDTH]), "mu_t": _row(_pad_cols(mu[3 * B_WIDTH:], B_TAIL)),
        "w0": _row(b_w0[0]), "w2": lora_rows(b_w2[0], 0),
        "a0": _row(b_a0[0]), "a2": lora_rows(b_a2[0], B_DECAY_LORA),
        "g2": lora_rows(b_g2[0], B_DECAY_LORA + B_AAA_LORA).astype(BF16),
        "kk": _row(b_kk[0]), "ka": _row(b_ka[0]), "rk": _row(b_rk[0]),
        "lnw": _row(b_ln_w[0]), "lnb": _row(b_ln_b[0]),
    }
    yb = _rwkv7(u, rw, batch=batch, seq=seq)
    wo = w_out_even[0].astype(BF16)
    xf = _out_proj_ffn(ya, yb, wo[:A_WIDTH], wo[A_WIDTH:], xf, _row(norm_ffn[0]),
                       ffn_gate[0].astype(BF16), ffn_up[0].astype(BF16), ffn_down[0].astype(BF16),
                       _row(norm_final), tm=512, th=FFN_HIDDEN, final_norm=False)

    wi = w_in_odd[0].astype(BF16)
    seg = lambda j: _pad_heads(wi[:, C_WIDTH + j * D_WIDTH:C_WIDTH + (j + 1) * D_WIDTH], 1)
    w_in = jnp.concatenate(
        [seg(0), seg(1), seg(2), seg(3), wi[:, :C_WIDTH],
         _pad_cols(wi[:, C_WIDTH + 4 * D_WIDTH:], D_GATE_PAD)], axis=1)
    u = _norm_matmul(xf, _row(norm_mix[1]), w_in, tm=512, tn=4 * MXU_WIDTH)

    u2 = u[:, 4 * D_WIDTH_PAD:4 * D_WIDTH_PAD + C_WIDTH].reshape(m // S5_CHUNK, S5_CHUNK * C_WIDTH)
    s5p = _s5_params(c_lam_re[0].astype(F32), c_lam_im[0].astype(F32), c_log_step[0].astype(F32),
                     c_b_re[0].astype(F32), c_b_im[0].astype(F32), c_c_re[0].astype(F32),
                     c_c_im[0].astype(F32))
    tr = min(1024, m // S5_CHUNK)
    z = _s5_increments(u2, s5p["w"], tm=tr, tn=512)
    hp = _s5_carry(z, s5p["lr"], s5p["li"], batch=batch)
    yc = _s5_output(u2, hp, s5p["bd"], s5p["m"], _row(c_d[0]), c_glu_w[0].astype(BF16),
                    _row(c_glu_b[0]), tm=tr).reshape(m, C_WIDTH)

    gate_bias = _row(_pad_cols(jnp.concatenate([d_i_bias[0], d_f_bias[0]]).astype(F32), D_GATE_PAD))
    yd = _mlstm(u, _pad_heads(d_conv_q[0].astype(F32), 1), _pad_heads(d_conv_k[0].astype(F32), 1),
                gate_bias, _row(_pad_heads(d_norm[0].astype(F32), 0)), batch=batch, seq=seq)
    wo = w_out_odd[0].astype(BF16)
    xf = _out_proj_ffn(yc, yd, wo[:C_WIDTH], _pad_heads(wo[C_WIDTH:], 0), xf,
                       _row(norm_ffn[1]), ffn_gate[1].astype(BF16), ffn_up[1].astype(BF16),
                       ffn_down[1].astype(BF16), _row(norm_final), tm=512, th=FFN_HIDDEN, final_norm=True)
    return xf.reshape(batch, seq, d).astype(x.dtype)
```

```python
import functools
import math

import jax
import jax.numpy as jnp
from jax import lax
from jax.experimental import pallas as pl
from jax.experimental.pallas import tpu as pltpu

F32 = jnp.float32
BF16 = jnp.bfloat16

RMS_EPS = 1e-6
GN_EPS = 64e-5
M_INIT = -1e30

D_MODEL = 1024
A_WIDTH, A_HEADS, A_KEY = 512, 4, 128
B_WIDTH, B_HEADS, B_HEAD = 512, 8, 64
B_DECAY_LORA, B_AAA_LORA, B_GATE_LORA = 32, 32, 96
B_TAIL = 256
EVEN_IN_PAD = 4 * A_WIDTH + 3 * B_WIDTH + B_TAIL
C_WIDTH, C_GROUP, C_GROUPS, C_STATE = 256, 16, 16, 64
D_HEADS, D_HEAD, D_HEAD_PAD = 4, 192, 256
D_WIDTH = D_HEADS * D_HEAD
D_WIDTH_PAD = D_HEADS * D_HEAD_PAD
D_CONV = 4
D_GATE_PAD = 128
ODD_IN_PAD = 4 * D_WIDTH_PAD + C_WIDTH + D_GATE_PAD
FFN_HIDDEN = 2816

HGRN_TILE = 128
HGRN_SUB = 16
RWKV_CHUNK = 64
RWKV_TILE = 256
RWKV_BATCH = 2
S5_CHUNK = 16
MLSTM_CHUNK = 64
MLSTM_TILE = 256
MLSTM_BATCH = 1

VMEM_LIMIT = 56 * 1024 * 1024


def _cparams(sem):
    return pltpu.CompilerParams(dimension_semantics=sem, vmem_limit_bytes=VMEM_LIMIT)


def _split2(x):
    hi = x.astype(BF16)
    lo = (x - hi.astype(F32)).astype(BF16)
    return hi, lo


def _split3(x):
    hi = x.astype(BF16)
    r1 = x - hi.astype(F32)
    mid = r1.astype(BF16)
    lo = (r1 - mid.astype(F32)).astype(BF16)
    return hi, mid, lo


def _dg(a, b, dims):
    return lax.dot_general(a, b, (dims, ((), ())), preferred_element_type=F32)


_NN = ((1,), (0,))
_NT = ((1,), (1,))
_TN = ((0,), (0,))


def _mm1(a, b, dims=_NN):
    return _dg(a.astype(BF16), b.astype(BF16), dims)


def _mm3(a, b, dims=_NN):
    ah, al = _split2(a)
    bh, bl = _split2(b)
    return _dg(ah, bh, dims) + (_dg(ah, bl, dims) + _dg(al, bh, dims))


def _mm2(a, b, dims=_NN):
    ah, al = _split2(a)
    bh = b.astype(BF16)
    return _dg(ah, bh, dims) + _dg(al, bh, dims)


_mmi = _mm1


def _mm_sel(sel_bf16, x, dims=_NN):
    h, m, l = _split3(x)
    return _dg(sel_bf16, h, dims) + (_dg(sel_bf16, m, dims) + _dg(sel_bf16, l, dims))


def _x_sel(x, sel_bf16):
    n = x.shape[0]
    r = _dg(jnp.concatenate(_split3(x), axis=0), sel_bf16, _NN)
    return r[:n] + (r[n:2 * n] + r[2 * n:])


def _sigmoid(x):
    return 1.0 / (1.0 + jnp.exp(-x))


def _silu(x):
    return x * _sigmoid(x)


def _softplus(x):
    return jnp.maximum(x, 0.0) + jnp.log(1.0 + jnp.exp(-jnp.abs(x)))


def _gelu_tanh(x):
    c = math.sqrt(2.0 / math.pi)
    return 0.5 * x * (1.0 + jnp.tanh(c * (x + 0.044715 * (x * x * x))))


def _iota(shape, dim):
    return lax.broadcasted_iota(jnp.int32, shape, dim)


def _tril_mask(n, strict=False):
    r = _iota((n, n), 0)
    c = _iota((n, n), 1)
    return (r > c) if strict else (r >= c)


MXU_WIDTH = 256


def _col_chunks(n, target):
    bounds = list(range(0, n, target)) + [n]
    return list(zip(bounds[:-1], bounds[1:]))


def _resident(shape):
    return pl.BlockSpec(shape, lambda i: (0,) * len(shape))


def _norm_matmul_kernel(x_ref, g_ref, w_ref, o_ref, *, chunks):
    x = x_ref[...]
    ms = jnp.mean(x * x, axis=-1, keepdims=True)
    h = (x * lax.rsqrt(ms + RMS_EPS) * g_ref[...]).astype(BF16)
    for lo, hi in chunks:
        o_ref[:, lo:hi] = jnp.dot(h, w_ref[:, lo:hi], preferred_element_type=F32).astype(o_ref.dtype)


def _norm_matmul(x, g, w, *, tm, tn):
    m, d = x.shape
    n = w.shape[1]
    return pl.pallas_call(
        functools.partial(_norm_matmul_kernel, chunks=_col_chunks(n, tn)),
        grid=(m // tm,),
        in_specs=[pl.BlockSpec((tm, d), lambda i: (i, 0)), _resident((1, d)), _resident((d, n))],
        out_specs=pl.BlockSpec((tm, n), lambda i: (i, 0)),
        out_shape=jax.ShapeDtypeStruct((m, n), BF16),
        compiler_params=_cparams(("parallel",)),
        name="norm_in_proj",
    )(x, g, w)


def _ffn_kernel(a_ref, b_ref, wa_ref, wb_ref, x_ref, g_ref, wg_ref, wu_ref, wd_ref, gf_ref, o_ref,
                *, chunks, final_norm):
    x = x_ref[...] + jnp.dot(a_ref[...], wa_ref[...], preferred_element_type=F32)
    x = x + jnp.dot(b_ref[...], wb_ref[...], preferred_element_type=F32)
    ms = jnp.mean(x * x, axis=-1, keepdims=True)
    h = (x * lax.rsqrt(ms + RMS_EPS) * g_ref[...]).astype(BF16)
    y = x
    for lo, hi in chunks:
        gate = jnp.dot(h, wg_ref[:, lo:hi], preferred_element_type=F32)
        up = jnp.dot(h, wu_ref[:, lo:hi], preferred_element_type=F32)
        act = (_silu(gate) * up).astype(BF16)
        y = y + jnp.dot(act, wd_ref[lo:hi, :], preferred_element_type=F32)
    if final_norm:
        ms = jnp.mean(y * y, axis=-1, keepdims=True)
        y = y * lax.rsqrt(ms + RMS_EPS) * gf_ref[...]
    o_ref[...] = y


def _out_proj_ffn(a, b, wa, wb, x, g, wg, wu, wd, gf, *, tm, th, final_norm):
    m, d = x.shape
    hid = wg.shape[1]
    ka, kb = a.shape[1], b.shape[1]
    return pl.pallas_call(
        functools.partial(_ffn_kernel, chunks=_col_chunks(hid, th), final_norm=final_norm),
        grid=(m // tm,),
        in_specs=[
            pl.BlockSpec((tm, ka), lambda i: (i, 0)),
            pl.BlockSpec((tm, kb), lambda i: (i, 0)),
            _resident((ka, d)), _resident((kb, d)),
            pl.BlockSpec((tm, d), lambda i: (i, 0)),
            _resident((1, d)), _resident((d, hid)), _resident((d, hid)), _resident((hid, d)),
            _resident((1, d)),
        ],
        out_specs=pl.BlockSpec((tm, d), lambda i: (i, 0)),
        out_shape=jax.ShapeDtypeStruct((m, d), F32),
        compiler_params=_cparams(("parallel",)),
        name="out_proj_swiglu_ffn",
    )(a, b, wa, wb, x, g, wg, wu, wd, gf)


def _hgrn2_kernel(q_ref, f_ref, i_ref, g_ref, lb_ref, gain_ref, o_ref, s_ref):
    T, SUB = HGRN_TILE, HGRN_SUB
    nb = T // SUB

    @pl.when(pl.program_id(1) == 0)
    def _():
        s_ref[...] = jnp.zeros_like(s_ref)

    r = _iota((T, T), 0)
    c = _iota((T, T), 1)
    tri = jnp.where((r >= c) & ((r // SUB) == (c // SUB)), 1.0, 0.0).astype(BF16)
    HALF = SUB // 2
    rowh = _iota((nb, HALF, A_KEY), 1)
    heads = range(A_HEADS)
    sls = [slice(h * A_KEY, (h + 1) * A_KEY) for h in heads]
    intra, qts, kts, v3s, decs = [], [], [], [], []

    for h in heads:
        sl = sls[h]
        q = _silu(q_ref[:, sl].astype(F32))
        lb = lb_ref[:, sl]
        f = lb + (1.0 - lb) * _sigmoid(f_ref[:, sl].astype(F32))
        lf = jnp.log(f)
        k = 1.0 - f
        v = i_ref[:, sl].astype(F32)
        b = _mm_sel(tri, lf)
        b3 = b.reshape(nb, SUB, A_KEY)
        q3 = q.reshape(nb, SUB, A_KEY)
        k3 = k.reshape(nb, SUB, A_KEY)
        v3 = v.reshape(nb, SUB, A_KEY)
        log2e = 1.0 / math.log(2.0)
        b2 = b3 * log2e
        bk3 = b2 - jnp.log2(k3)
        b_up, b_lo = b2[:, :HALF], b2[:, HALF:]
        q_up, q_lo = q3[:, :HALF], q3[:, HALF:]
        acc_up = jnp.zeros((nb, HALF, A_KEY), F32)
        acc_lo = jnp.zeros((nb, HALF, A_KEY), F32)
        for s in range(HALF):
            bs = bk3[:, s:s + 1, :]
            vs = v3[:, s:s + 1, :]
            e_up = jnp.exp2(jnp.where(rowh >= s, b_up - bs, -jnp.inf))
            acc_up = acc_up + jnp.sum(q_up * e_up, axis=-1, keepdims=True) * vs
            acc_lo = acc_lo + jnp.sum(q_lo * jnp.exp2(b_lo - bs), axis=-1, keepdims=True) * vs
        for s in range(HALF):
            bs = bk3[:, HALF + s:HALF + s + 1, :]
            e_lo = jnp.exp2(jnp.where(rowh >= s, b_lo - bs, -jnp.inf))
            acc_lo = acc_lo + jnp.sum(q_lo * e_lo, axis=-1, keepdims=True) * v3[:, HALF + s:HALF + s + 1, :]
        intra.append(jnp.concatenate([acc_up, acc_lo], axis=1).reshape(T, A_KEY))

        bend = b3[:, SUB - 1:SUB, :]
        qts.append((q3 * jnp.exp(b3)).astype(BF16))
        kts.append((k3 * jnp.exp(bend - b3)).astype(BF16))
        v3s.append(v3.astype(BF16))
        decs.append(jnp.exp(bend))

    inc = [[_dg(v3s[h][i], kts[h][i], _TN) for i in range(nb)] for h in heads]
    for h in heads:
        st = s_ref[h]
        outs = []
        for i in range(nb):
            outs.append(_dg(qts[h][i], st.astype(BF16), _NT))
            st = st * decs[h][i] + inc[h][i]
        s_ref[h] = st
        o = intra[h] + jnp.concatenate(outs, axis=0)
        o = o * lax.rsqrt(jnp.mean(o * o, axis=-1, keepdims=True) + RMS_EPS) * gain_ref[:, sls[h]]
        o_ref[:, sls[h]] = (o * _silu(g_ref[:, sls[h]].astype(F32))).astype(o_ref.dtype)


def _hgrn2(u, lb, gain, *, batch, seq):
    T = HGRN_TILE
    nt = seq // T
    col = lambda j: pl.BlockSpec((T, A_WIDTH), lambda b, c, j=j: (b * nt + c, j))
    vec = pl.BlockSpec((1, A_WIDTH), lambda b, c: (0, 0))
    return pl.pallas_call(
        _hgrn2_kernel,
        grid=(batch, nt),
        in_specs=[col(0), col(1), col(2), col(3), vec, vec],
        out_specs=pl.BlockSpec((T, A_WIDTH), lambda b, c: (b * nt + c, 0)),
        out_shape=jax.ShapeDtypeStruct((batch * seq, A_WIDTH), BF16),
        scratch_shapes=[pltpu.VMEM((A_HEADS, A_KEY, A_KEY), F32)],
        compiler_params=_cparams(("parallel", "arbitrary")),
        name="hgrn2_mix",
    )(u, u, u, u, lb, gain)


def _rwkv7_kernel(r_ref, k_ref, v_ref, t_ref, mu_r, mu_k, mu_v, mu_t, w0_ref, w2_ref, a0_ref,
                  a2_ref, g2_ref, kk_ref, ka_ref, rk_ref, lnw_ref, lnb_ref, o_ref,
                  s_ref, cr_ref, ck_ref, cv_ref, ct_ref):
    L = RWKV_CHUNK
    NB = RWKV_BATCH
    TB = RWKV_TILE
    T = NB * TB
    N = B_HEAD
    nc = TB // L

    @pl.when(pl.program_id(1) == 0)
    def _():
        s_ref[...] = jnp.zeros_like(s_ref)
        cr_ref[...] = jnp.zeros_like(cr_ref)
        ck_ref[...] = jnp.zeros_like(ck_ref)
        cv_ref[...] = jnp.zeros_like(cv_ref)
        ct_ref[...] = jnp.zeros_like(ct_ref)

    def shift_mix(x_ref, carry_ref, mu_ref):
        outs = []
        for b in range(NB):
            x = x_ref[b].astype(F32)
            row = _iota(x.shape, 0)
            prev = jnp.where(row == 0, carry_ref[b, 7:8, :], pltpu.roll(x, 1, 0))
            carry_ref[b] = x[TB - 8:TB, :]
            outs.append(x + (prev - x) * mu_ref[...])
        return jnp.concatenate(outs, axis=0)

    r = shift_mix(r_ref, cr_ref, mu_r)
    k = shift_mix(k_ref, ck_ref, mu_k)
    v = shift_mix(v_ref, cv_ref, mu_v)
    tl = shift_mix(t_ref, ct_ref, mu_t)

    wl = _mm1(jnp.tanh(tl), w2_ref[...])
    al = _mm1(tl, a2_ref[...])
    gate = _mm1(_sigmoid(tl), g2_ref[...])
    w_log = -_softplus(-(w0_ref[...] + wl)) - 0.5
    lw = -jnp.exp(w_log)
    a_lr = _sigmoid(a0_ref[...] + al)

    ones_bd = jnp.where(_iota((MXU_WIDTH, MXU_WIDTH), 0) // N == _iota((MXU_WIDTH, MXU_WIDTH), 1) // N,
                        1.0, 0.0).astype(BF16)

    def head_sums(x):
        n = x.shape[0]
        parts = jnp.concatenate(_split2(x), axis=0)
        r = jnp.concatenate([_dg(parts[:, g * MXU_WIDTH:(g + 1) * MXU_WIDTH], ones_bd, _NN)
                             for g in range(B_WIDTH // MXU_WIDTH)], axis=-1)
        return r[:n] + r[n:]

    kk = k * kk_ref[...]
    k2 = k * (1.0 + (a_lr - 1.0) * ka_ref[...])
    sums = head_sums(jnp.concatenate([kk * kk, r * k2 * rk_ref[...]], axis=0))
    kk = kk / jnp.maximum(jnp.sqrt(sums[:T]), 1e-12)
    bonus = sums[T:]
    aa = -kk
    bb = kk * a_lr

    rr = _iota((TB, TB), 0)
    cc = _iota((TB, TB), 1)
    tri = jnp.where((rr >= cc) & (rr // L == cc // L), 1.0, 0.0).astype(BF16)
    c = jnp.concatenate([_mm_sel(tri, lw[b * TB:(b + 1) * TB, :]) for b in range(NB)], axis=0)
    enc = jnp.exp(-c)
    rt = (r * jnp.exp(c)).astype(BF16)
    at = (aa * jnp.exp(c - lw)).astype(BF16)
    bt = (bb * enc).astype(BF16)
    kt = (k2 * enc).astype(BF16)
    vb = v.astype(BF16)

    PW = 2 * N
    pairs = range(B_HEADS // 2)
    pc = lambda p: slice(p * PW, (p + 1) * PW)
    r2 = _iota((2 * L, PW), 0)
    c2 = _iota((2 * L, PW), 1) % L
    both = ((r2 < L) & (r2 > c2)) | (r2 - L >= c2)
    left1 = _iota((L, PW), 1) < N
    left2 = (_iota((L, 2 * PW), 1) % PW) < N
    same_head = (_iota((PW, PW), 0) // N) == (_iota((PW, PW), 1) // N)

    def bdiag(z, left):
        zero = jnp.zeros_like(z)
        return jnp.concatenate([jnp.where(left, z, zero), jnp.where(left, zero, z)], axis=0)

    solved = {}
    NP = B_HEADS // 2
    state = {"s": [[s_ref[b * NP + p] for p in pairs] for b in range(NB)], "y": {}}

    def solve_stages(ci):
        rs = slice(ci * L, (ci + 1) * L)
        ar = [jnp.concatenate([at[rs, pc(p)], rt[rs, pc(p)]], axis=0) for p in pairs]
        vd = [bdiag(vb[rs, pc(p)], left1) for p in pairs]
        m_b = [jnp.where(both, _dg(ar[p], bdiag(bt[rs, pc(p)], left1), _NT), 0.0) for p in pairs]
        yield
        m_k = [jnp.where(both, _dg(ar[p], bdiag(kt[rs, pc(p)], left1), _NT), 0.0).astype(BF16) for p in pairs]
        yield
        mkv = [_dg(m_k[p], vd[p], _NN) for p in pairs]
        yield
        pw = [m_b[p][:L] for p in pairs]
        x = [jnp.concatenate([ar[p][:L].astype(F32), mkv[p][:L]], axis=-1) for p in pairs]
        x = [x[p] + _dg(pw[p].astype(BF16), bdiag(x[p].astype(BF16), left2), _NN) for p in pairs]
        yield
        for _ in range(int(math.log2(L)) - 1):
            pw = [_dg(pw[p].astype(BF16), bdiag(pw[p].astype(BF16), left1), _NN) for p in pairs]
            yield
            x = [x[p] + _dg(pw[p].astype(BF16), bdiag(x[p].astype(BF16), left2), _NN) for p in pairs]
            yield
        c_last = c[(ci + 1) * L - 1:(ci + 1) * L, :]
        e_tail = jnp.exp(c_last - c[rs, :])
        solved[ci] = dict(
            wr=[jnp.concatenate([x[p][:, :PW].astype(BF16), ar[p][L:]], axis=0) for p in pairs],
            u0=[x[p][:, PW:] for p in pairs],
            a_rb=[m_b[p][L:].astype(BF16) for p in pairs],
            y0=[mkv[p][L:] for p in pairs], vh=[vb[rs, pc(p)] for p in pairs], g_last=jnp.exp(c_last),
            bkl=jnp.concatenate([(bb[rs, :] * e_tail).astype(BF16),
                                 (k2[rs, :] * e_tail).astype(BF16)], axis=0))
        yield

    def state_stages(ci):
        q = solved.pop(ci)
        st = state["s"][ci // nc]
        ws = [_dg(q["wr"][p], st[p].astype(BF16), _NT) for p in pairs]
        yield
        ub = [(ws[p][:L] + q["u0"][p]).astype(BF16) for p in pairs]
        state["y"][ci] = jnp.concatenate(
            [ws[p][L:] + _dg(q["a_rb"][p], bdiag(ub[p], left1), _NN) + q["y0"][p] for p in pairs], axis=-1)
        upd = [_dg(jnp.concatenate([ub[p], q["vh"][p]], axis=0), q["bkl"][:, pc(p)], _TN) for p in pairs]
        state["s"][ci // nc] = [st[p] * q["g_last"][:, pc(p)] + jnp.where(same_head, upd[p], 0.0)
                                for p in pairs]
        yield

    for _ in zip(*[solve_stages(ci) for ci in range(NB * nc)]):
        pass
    for ci in range(nc):
        for _ in zip(*[state_stages(b * nc + ci) for b in range(NB)]):
            pass
    for b in range(NB):
        for p in pairs:
            s_ref[b * NP + p] = state["s"][b][p]
    ys = [state["y"][ci] for ci in range(NB * nc)]

    y = jnp.concatenate(ys, axis=0)
    mom = head_sums(jnp.concatenate([y, y * y], axis=0)) * (1.0 / N)
    mean = mom[:T]
    var = jnp.maximum(mom[T:] - mean * mean, 0.0)
    y = (y - mean) * lax.rsqrt(var + GN_EPS) * lnw_ref[...] + lnb_ref[...]
    y = y + bonus * v
    out = (y * gate).astype(o_ref.dtype)
    for b in range(NB):
        o_ref[b] = out[b * TB:(b + 1) * TB, :]


def _rwkv7(u, p, *, batch, seq):
    T = RWKV_TILE
    NB = RWKV_BATCH
    nt = seq // T
    u = u.reshape(batch, seq, u.shape[-1])
    col = lambda j: pl.BlockSpec((NB, T, B_WIDTH), lambda b, c, j=j: (b, c, j))
    tail = pl.BlockSpec((NB, T, B_TAIL), lambda b, c: (b, c, (4 * A_WIDTH + 3 * B_WIDTH) // B_TAIL))
    full = lambda a: pl.BlockSpec(a.shape, lambda b, c: (0,) * a.ndim)
    params = [p["mu_r"], p["mu_k"], p["mu_v"], p["mu_t"], p["w0"], p["w2"], p["a0"], p["a2"],
              p["g2"], p["kk"], p["ka"], p["rk"], p["lnw"], p["lnb"]]
    return pl.pallas_call(
        _rwkv7_kernel,
        grid=(batch // NB, nt),
        in_specs=[col(4), col(5), col(6), tail] + [full(a) for a in params],
        out_specs=pl.BlockSpec((NB, T, B_WIDTH), lambda b, c: (b, c, 0)),
        out_shape=jax.ShapeDtypeStruct((batch, seq, B_WIDTH), BF16),
        scratch_shapes=[pltpu.VMEM((NB * (B_HEADS // 2), 2 * B_HEAD, 2 * B_HEAD), F32),
                        pltpu.VMEM((NB, 8, B_WIDTH), F32), pltpu.VMEM((NB, 8, B_WIDTH), F32),
                        pltpu.VMEM((NB, 8, B_WIDTH), F32), pltpu.VMEM((NB, 8, B_TAIL), F32)],
        compiler_params=_cparams(("parallel", "arbitrary")),
        name="rwkv7_mix",
    )(u, u, u, u, *params).reshape(batch * seq, B_WIDTH)


def _s5_inc_kernel(u_ref, w_ref, z_ref):
    z_ref[...] = jnp.dot(u_ref[...], w_ref[...], preferred_element_type=F32)


def _s5_increments(u2, w, *, tm, tn):
    n, kdim = u2.shape
    ncol = w.shape[1]
    return pl.pallas_call(
        _s5_inc_kernel,
        grid=(n // tm, ncol // tn),
        in_specs=[pl.BlockSpec((tm, kdim), lambda i, j: (i, 0)),
                  pl.BlockSpec((kdim, tn), lambda i, j: (0, j))],
        out_specs=pl.BlockSpec((tm, tn), lambda i, j: (i, j)),
        out_shape=jax.ShapeDtypeStruct((n, ncol), F32),
        compiler_params=_cparams(("parallel", "parallel")),
        name="s5_increments",
    )(u2, w)


def _s5_carry_kernel(z_ref, lr_ref, li_ref, hp_ref, h_ref):
    n = z_ref.shape[0]
    half = z_ref.shape[1] // 2
    lr = lr_ref[...]
    li = li_ref[...]

    def body(i, carry):
        hr, hi = carry
        rows = pl.ds(pl.multiple_of(i * 8, 8), 8)
        zz = z_ref[rows, :]
        outs = []
        for j in range(8):
            outs.append(jnp.concatenate([hr, hi], axis=-1))
            hr, hi = (lr * hr - li * hi + zz[j:j + 1, :half],
                      lr * hi + li * hr + zz[j:j + 1, half:])
        h_ref[rows, :] = jnp.concatenate(outs, axis=0)
        return hr, hi

    zero = jnp.zeros((1, half), F32)
    lax.fori_loop(0, n // 8, body, (zero, zero))
    hp_ref[...] = h_ref[...].astype(hp_ref.dtype)


def _s5_carry(z, lr, li, *, batch):
    n, w = z.shape
    rows = n // batch
    vec = pl.BlockSpec((1, w // 2), lambda b: (0, 0))
    return pl.pallas_call(
        _s5_carry_kernel,
        grid=(batch,),
        in_specs=[pl.BlockSpec((rows, w), lambda b: (b, 0)), vec, vec],
        out_specs=pl.BlockSpec((rows, w), lambda b: (b, 0)),
        out_shape=jax.ShapeDtypeStruct((n, w), BF16),
        scratch_shapes=[pltpu.VMEM((rows, w), F32)],
        compiler_params=_cparams(("parallel",)),
        name="s5_carry",
    )(z, lr, li)


def _s5_out_kernel(u_ref, hp_ref, bd_ref, m_ref, d_ref, gw_ref, gb_ref, o_ref):
    hp = hp_ref[...]
    steps = range(S5_CHUNK)
    cs = [slice(t * C_WIDTH, (t + 1) * C_WIDTH) for t in steps]
    ys = [jnp.dot(hp, m_ref[:, cs[t]], preferred_element_type=F32) for t in steps]
    ys = [ys[t] + jnp.dot(u_ref[:, :(t + 1) * C_WIDTH],
                          jnp.concatenate([bd_ref[t - s] for s in range(t + 1)], axis=0),
                          preferred_element_type=F32) for t in steps]
    zs = [_gelu_tanh(ys[t] + d_ref[...] * u_ref[:, cs[t]].astype(F32)) for t in steps]
    gates = [jnp.dot(zs[t].astype(BF16), gw_ref[...], preferred_element_type=F32) + gb_ref[...] for t in steps]
    for t in steps:
        o_ref[:, cs[t]] = (zs[t] * _sigmoid(gates[t])).astype(o_ref.dtype)


def _s5_output(u2, hp, bd, mmat, d, gw, gb, *, tm):
    n, kdim = u2.shape
    kst = hp.shape[1]
    return pl.pallas_call(
        _s5_out_kernel,
        grid=(n // tm,),
        in_specs=[pl.BlockSpec((tm, kdim), lambda i: (i, 0)),
                  pl.BlockSpec((tm, kst), lambda i: (i, 0)),
                  _resident(bd.shape), _resident(mmat.shape),
                  _resident((1, C_WIDTH)), _resident((C_WIDTH, C_WIDTH)), _resident((1, C_WIDTH))],
        out_specs=pl.BlockSpec((tm, kdim), lambda i: (i, 0)),
        out_shape=jax.ShapeDtypeStruct((n, kdim), BF16),
        compiler_params=_cparams(("parallel",)),
        name="s5_output",
    )(u2, hp, bd, mmat, d, gw, gb)


def _s5_params(lam_re, lam_im, log_step, b_re, b_im, c_re, c_im):
    L = S5_CHUNK
    step = jnp.exp(log_step)[:, None]
    mag = jnp.exp(lam_re * step)
    lr, li = mag * jnp.cos(lam_im * step), mag * jnp.sin(lam_im * step)
    den = lam_re * lam_re + lam_im * lam_im
    nr, ni = lr - 1.0, li
    coef_re = (nr * lam_re + ni * lam_im) / den
    coef_im = (ni * lam_re - nr * lam_im) / den
    bb_re = coef_re[..., None] * b_re - coef_im[..., None] * b_im
    bb_im = coef_re[..., None] * b_im + coef_im[..., None] * b_re
    pr, pi = [jnp.ones_like(lr)], [jnp.zeros_like(lr)]
    for _ in range(L):
        pr.append(pr[-1] * lr - pi[-1] * li)
        pi.append(pr[-2] * li + pi[-1] * lr)
    pr, pi = jnp.stack(pr), jnp.stack(pi)
    hp = lax.Precision.HIGHEST
    zr = pr[:L, :, :, None] * bb_re[None] - pi[:L, :, :, None] * bb_im[None]
    zi = pr[:L, :, :, None] * bb_im[None] + pi[:L, :, :, None] * bb_re[None]
    kern = (jnp.einsum("gop,lgpi->lgoi", c_re, zr, precision=hp)
            - jnp.einsum("gop,lgpi->lgoi", c_im, zi, precision=hp))
    nin = L * C_WIDTH
    nst = 2 * C_GROUPS * C_STATE

    def spread(a, row_group, n_inner):
        rows, cols = a.shape
        wide = cols * C_GROUPS
        cw = jnp.arange(wide)
        src = (cw // (C_GROUPS * n_inner)) * n_inner + cw % n_inner
        rep = (jnp.arange(cols)[:, None] == src[None, :]).astype(BF16)
        out = jnp.dot(a.astype(BF16), rep, preferred_element_type=F32)
        keep = row_group[:, None] == ((cw // n_inner) % C_GROUPS)[None, :]
        return jnp.where(keep, out, 0.0).astype(BF16)

    ka = kern.transpose(0, 1, 3, 2).reshape(L * C_WIDTH, C_GROUP)
    bd = spread(ka, (jnp.arange(L * C_WIDTH) // C_GROUP) % C_GROUPS, C_GROUP).reshape(L, C_WIDTH, C_WIDTH)
    wz = jnp.stack([zr[::-1], zi[::-1]])
    wa = wz.transpose(1, 2, 4, 0, 3).reshape(nin, 2 * C_STATE)
    w = spread(wa, (jnp.arange(nin) // C_GROUP) % C_GROUPS, C_STATE)
    qr, qi = pr[1:], pi[1:]
    m_re = c_re[None] * qr[:, :, None, :] - c_im[None] * qi[:, :, None, :]
    m_im = -(c_re[None] * qi[:, :, None, :] + c_im[None] * qr[:, :, None, :])
    ma = jnp.stack([m_re, m_im]).transpose(0, 2, 4, 1, 3).reshape(nst, L * C_GROUP)
    mmat = spread(ma, (jnp.arange(nst) // C_STATE) % C_GROUPS, C_GROUP)
    return {"w": w, "bd": bd, "m": mmat,
            "lr": pr[L].reshape(1, -1), "li": pi[L].reshape(1, -1)}


def _mlstm_kernel(q_ref, k_ref, v_ref, o_ref, gt_ref, cq_ref, ck_ref, gb_ref, gain_ref, y_ref,
                  c_ref, n_ref, m_ref, pq_ref, pk_ref):
    L = MLSTM_CHUNK
    P = D_HEAD_PAD

    @pl.when(pl.program_id(1) == 0)
    def _():
        c_ref[...] = jnp.zeros_like(c_ref)
        n_ref[...] = jnp.zeros_like(n_ref)
        m_ref[...] = jnp.full(m_ref.shape, M_INIT, F32)
        pq_ref[...] = jnp.zeros_like(pq_ref)
        pk_ref[...] = jnp.zeros_like(pk_ref)

    NB = MLSTM_BATCH
    TB = MLSTM_TILE
    T = NB * TB
    nc = TB // L
    PREV = pq_ref.shape[1]
    LE = L + PREV
    out_row = _iota((L, D_CONV * LE), 0)
    kcol = _iota((L, D_CONV * LE), 1)
    sel = jnp.where(kcol % LE == out_row + kcol // LE - (D_CONV - 1) + PREV, 1.0, 0.0).astype(BF16)

    def conv(x_ref, prev_ref, w_ref):
        outs = []
        for b in range(NB):
            x = x_ref[b]
            for ci in range(nc):
                prev = prev_ref[b] if ci == 0 else x[ci * L - PREV:ci * L, :]
                xe = jnp.concatenate([prev, x[ci * L:(ci + 1) * L, :]], axis=0).astype(F32)
                stack = jnp.concatenate([(xe * w_ref[j:j + 1, :]).astype(BF16) for j in range(D_CONV)], axis=0)
                outs.append(_dg(sel, stack, _NN))
            prev_ref[b] = x[TB - PREV:TB, :]
        return _silu(jnp.concatenate(outs, axis=0))

    q = conv(q_ref, pq_ref, cq_ref)
    k = conv(k_ref, pk_ref, ck_ref) * (1.0 / math.sqrt(D_HEAD))
    vb = jnp.concatenate([v_ref[b] for b in range(NB)], axis=0)

    gates = jnp.concatenate([gt_ref[b] for b in range(NB)], axis=0).astype(F32) + gb_ref[...]
    col = _iota(gates.shape, 1)
    is_f = (col >= D_HEADS) & (col < 2 * D_HEADS)
    gates = jnp.where(is_f, jnp.minimum(gates, 0.0) - jnp.log(1.0 + jnp.exp(-jnp.abs(gates))), gates)
    rr = _iota((TB, TB), 0)
    cc = _iota((TB, TB), 1)
    tri = jnp.where((rr >= cc) & (rr // L == cc // L), 1.0, 0.0).astype(BF16)
    bcum = jnp.concatenate([_mm_sel(tri, gates[b * TB:(b + 1) * TB, :]) for b in range(NB)], axis=0)
    gates_t = gates.T
    bcum_t = bcum.T
    lower = _tril_mask(L)

    heads = range(D_HEADS)
    sls = [slice(h * P, (h + 1) * P) for h in heads]
    qb, kb = q.astype(BF16), k.astype(BF16)
    rows = [slice(ci * L, (ci + 1) * L) for ci in range(NB * nc)]
    qk0 = [[_dg(qb[rs, sls[h]], kb[rs, sls[h]], _NT) for h in heads] for rs in rows]
    sid = lambda ci, h: (ci // nc) * D_HEADS + h
    c_st = [c_ref[i] for i in range(NB * D_HEADS)]
    n_st = [n_ref[i] for i in range(NB * D_HEADS)]
    m_st = [m_ref[i] for i in range(NB * D_HEADS)]

    probs = [(ci, h) for ci in range(NB * nc) for h in heads]
    g = {}
    for ci, h in probs:
        rs = rows[ci]
        b_col = bcum[rs, D_HEADS + h:D_HEADS + h + 1]
        b_row = bcum_t[D_HEADS + h:D_HEADS + h + 1, rs]
        dmat = jnp.where(lower, b_col - b_row + gates_t[h:h + 1, rs], -jnp.inf)
        g[ci, h] = dict(b_col=b_col, i_col=gates[rs, h:h + 1], dmat=dmat,
                        dmax=jnp.max(dmat, axis=-1, keepdims=True), b_last=b_col[L - 1:L, :])
    for ci, h in probs:
        e, i = g[ci, h], sid(ci, h)
        e["e_inter"] = e["b_col"] + m_st[i]
        e["m_t"] = jnp.maximum(e["e_inter"], e["dmax"])
        e["m_new"] = e["m_t"][L - 1:L, :]
        e["d_prev"] = jnp.exp(e["b_last"] + m_st[i] - e["m_new"])
        m_st[i] = e["m_new"]
    for p in probs:
        e = g[p]
        e["w_intra"] = jnp.exp(e["dmat"] - e["m_t"])
        e["s_inter"] = jnp.exp(e["e_inter"] - e["m_t"])
        e["w_last"] = jnp.exp(e["b_last"] - e["b_col"] + e["i_col"] - e["m_new"])
    qk = {(ci, h): qk0[ci][h] * g[ci, h]["w_intra"] for ci, h in probs}
    kw = {(ci, h): k[rows[ci], sls[h]] * g[ci, h]["w_last"] for ci, h in probs}
    qkv = {(ci, h): _dg(qk[ci, h].astype(BF16), vb[rows[ci], sls[h]], _NN) for ci, h in probs}
    cu = {(ci, h): _dg(kw[ci, h].astype(BF16), vb[rows[ci], sls[h]], _TN) for ci, h in probs}
    c_in, n_in = {}, {}
    for ci, h in probs:
        i = sid(ci, h)
        c_in[ci, h], n_in[ci, h] = c_st[i], n_st[i]
        c_st[i] = g[ci, h]["d_prev"] * c_st[i] + cu[ci, h]
        n_st[i] = g[ci, h]["d_prev"] * n_st[i] + jnp.sum(kw[ci, h], axis=0, keepdims=True)
    qc = {(ci, h): _dg(qb[rows[ci], sls[h]], c_in[ci, h].astype(BF16), _NN) for ci, h in probs}
    qn = {(ci, h): jnp.sum(q[rows[ci], sls[h]] * n_in[ci, h], axis=-1, keepdims=True) for ci, h in probs}
    qks = {p: jnp.sum(qk[p], axis=-1, keepdims=True) for p in probs}
    den = {p: g[p]["s_inter"] * qn[p] + qks[p] for p in probs}
    hh = {p: (g[p]["s_inter"] * qc[p] + qkv[p]) / jnp.maximum(jnp.abs(den[p]), jnp.exp(-g[p]["m_t"]))
          for p in probs}
    ms = {p: jnp.sum(hh[p] * hh[p], axis=-1, keepdims=True) * (1.0 / D_HEAD) for p in probs}
    for ci, h in probs:
        sl = sls[h]
        b, lrs = ci // nc, slice((ci % nc) * L, (ci % nc + 1) * L)
        out = (hh[ci, h] * lax.rsqrt(ms[ci, h] + RMS_EPS) * gain_ref[:, sl]
               * _sigmoid(o_ref[b, lrs, sl].astype(F32)))
        y_ref[b, lrs, sl] = out.astype(y_ref.dtype)

    for i in range(NB * D_HEADS):
        c_ref[i] = c_st[i]
        n_ref[i] = n_st[i]
        m_ref[i] = m_st[i]


def _mlstm(u, conv_q, conv_k, gate_bias, gain, *, batch, seq):
    L = MLSTM_TILE
    NB = MLSTM_BATCH
    nt = seq // L
    W = D_WIDTH_PAD
    u = u.reshape(batch, seq, u.shape[-1])
    col = lambda j: pl.BlockSpec((NB, L, W), lambda b, c, j=j: (b, c, j))
    gcol = pl.BlockSpec((NB, L, D_GATE_PAD), lambda b, c: (b, c, (4 * W + C_WIDTH) // D_GATE_PAD))
    full = lambda a: pl.BlockSpec(a.shape, lambda b, c: (0,) * a.ndim)
    return pl.pallas_call(
        _mlstm_kernel,
        grid=(batch // NB, nt),
        in_specs=[col(0), col(1), col(2), col(3), gcol, full(conv_q), full(conv_k),
                  full(gate_bias), full(gain)],
        out_specs=pl.BlockSpec((NB, L, W), lambda b, c: (b, c, 0)),
        out_shape=jax.ShapeDtypeStruct((batch, seq, W), BF16),
        scratch_shapes=[pltpu.VMEM((NB * D_HEADS, D_HEAD_PAD, D_HEAD_PAD), F32),
                        pltpu.VMEM((NB * D_HEADS, 1, D_HEAD_PAD), F32),
                        pltpu.VMEM((NB * D_HEADS, 1, 1), F32),
                        pltpu.VMEM((NB, 16, W), BF16), pltpu.VMEM((NB, 16, W), BF16)],
        compiler_params=_cparams(("parallel", "arbitrary")),
        name="mlstm_mix",
    )(u, u, u, u, u, conv_q, conv_k, gate_bias, gain).reshape(batch * seq, W)


def _pad_cols(a, n):
    return jnp.pad(a, [(0, 0)] * (a.ndim - 1) + [(0, n - a.shape[-1])])


def _pad_heads(a, axis):
    zshape = a.shape[:axis] + (D_HEAD_PAD - D_HEAD,) + a.shape[axis + 1:]
    pieces = []
    for h in range(D_HEADS):
        pieces += [lax.slice_in_dim(a, h * D_HEAD, (h + 1) * D_HEAD, axis=axis), jnp.zeros(zshape, a.dtype)]
    return jnp.concatenate(pieces, axis=axis)


def _row(v):
    return v.reshape(1, -1).astype(F32)


def kernel(x, norm_mix, norm_ffn, norm_final, w_in_even, w_out_even, lb_table, a_norm, b_mu, b_w0, b_w2, b_a0, b_a2, b_g2, b_kk, b_ka, b_rk, b_ln_w, b_ln_b, w_in_odd, w_out_odd, c_lam_re, c_lam_im, c_log_step, c_b_re, c_b_im, c_c_re, c_c_im, c_d, c_glu_w, c_glu_b, d_conv_q, d_conv_k, d_i_bias, d_f_bias, d_norm, ffn_gate, ffn_up, ffn_down):
    batch, seq, d = x.shape
    m = batch * seq
    xf = x.reshape(m, d).astype(F32)
    lower_bounds = jnp.cumsum(jax.nn.softmax(lb_table.astype(F32), axis=0), axis=0)

    w_in = _pad_cols(w_in_even[0], EVEN_IN_PAD).astype(BF16)
    u = _norm_matmul(xf, _row(norm_mix[0]), w_in, tm=512, tn=3 * MXU_WIDTH)
    ya = _hgrn2(u, _row(lower_bounds[0]), _row(a_norm[0]), batch=batch, seq=seq)

    mu = b_mu[0].astype(F32)
    lora_rows = lambda w, off: jnp.pad(w.astype(F32), ((off, B_TAIL - off - w.shape[0]), (0, 0)))
    rw = {
        "mu_r": _row(mu[:B_WIDTH]), "mu_k": _row(mu[B_WIDTH:2 * B_WIDTH]),
        "mu_v": _row(mu[2 * B_WIDTH:3 * B_WIDTH]), "mu_t": _row(_pad_cols(mu[3 * B_WIDTH:], B_TAIL)),
        "w0": _row(b_w0[0]), "w2": lora_rows(b_w2[0], 0),
        "a0": _row(b_a0[0]), "a2": lora_rows(b_a2[0], B_DECAY_LORA),
        "g2": lora_rows(b_g2[0], B_DECAY_LORA + B_AAA_LORA).astype(BF16),
        "kk": _row(b_kk[0]), "ka": _row(b_ka[0]), "rk": _row(b_rk[0]),
        "lnw": _row(b_ln_w[0]), "lnb": _row(b_ln_b[0]),
    }
    yb = _rwkv7(u, rw, batch=batch, seq=seq)
    wo = w_out_even[0].astype(BF16)
    xf = _out_proj_ffn(ya, yb, wo[:A_WIDTH], wo[A_WIDTH:], xf, _row(norm_ffn[0]),
                       ffn_gate[0].astype(BF16), ffn_up[0].astype(BF16), ffn_down[0].astype(BF16),
                       _row(norm_final), tm=512, th=FFN_HIDDEN, final_norm=False)

    wi = w_in_odd[0].astype(BF16)
    seg = lambda j: _pad_heads(wi[:, C_WIDTH + j * D_WIDTH:C_WIDTH + (j + 1) * D_WIDTH], 1)
    w_in = jnp.concatenate(
        [seg(0), seg(1), seg(2), seg(3), wi[:, :C_WIDTH],
         _pad_cols(wi[:, C_WIDTH + 4 * D_WIDTH:], D_GATE_PAD)], axis=1)
    u = _norm_matmul(xf, _row(norm_mix[1]), w_in, tm=512, tn=4 * MXU_WIDTH)

    u2 = u[:, 4 * D_WIDTH_PAD:4 * D_WIDTH_PAD + C_WIDTH].reshape(m // S5_CHUNK, S5_CHUNK * C_WIDTH)
    s5p = _s5_params(c_lam_re[0].astype(F32), c_lam_im[0].astype(F32), c_log_step[0].astype(F32),
                     c_b_re[0].astype(F32), c_b_im[0].astype(F32), c_c_re[0].astype(F32),
                     c_c_im[0].astype(F32))
    tr = min(1024, m // S5_CHUNK)
    z = _s5_increments(u2, s5p["w"], tm=tr, tn=512)
    hp = _s5_carry(z, s5p["lr"], s5p["li"], batch=batch)
    yc = _s5_output(u2, hp, s5p["bd"], s5p["m"], _row(c_d[0]), c_glu_w[0].astype(BF16),
                    _row(c_glu_b[0]), tm=min(512, tr)).reshape(m, C_WIDTH)

    gate_bias = _row(_pad_cols(jnp.concatenate([d_i_bias[0], d_f_bias[0]]).astype(F32), D_GATE_PAD))
    yd = _mlstm(u, _pad_heads(d_conv_q[0].astype(F32), 1), _pad_heads(d_conv_k[0].astype(F32), 1),
                gate_bias, _row(_pad_heads(d_norm[0].astype(F32), 0)), batch=batch, seq=seq)
    wo = w_out_odd[0].astype(BF16)
    xf = _out_proj_ffn(yc, yd, wo[:C_WIDTH], _pad_heads(wo[C_WIDTH:], 0), xf,
                       _row(norm_ffn[1]), ffn_gate[1].astype(BF16), ffn_up[1].astype(BF16),
                       ffn_down[1].astype(BF16), _row(norm_final), tm=512, th=FFN_HIDDEN, final_norm=True)
    return xf.reshape(batch, seq, d).astype(x.dtype)
```
